```python
import jax
import jax.numpy as jnp
from jax import lax
import numpy as np

D_MODEL = 2048
BATCH = 16
SEQ = 2048
DEPTH = 2

CTX_LEN = 256
GRID_W = 64
N_MIXERS = 4
GROUP_W = D_MODEL // N_MIXERS
N_HEADS = 4
HEAD_DIM = GROUP_W // N_HEADS
NA_WIN_ROWS = 8
NA_WIN_COLS = 16
NA_COL_BLOCK = 16
NA_KEY_COLS = 2 * NA_WIN_COLS
GLA_DK = HEAD_DIM // 2
GLA_GATE_RANK = 16
GLA_GATE_TAU = 16.0
GDN_DK = HEAD_DIM
GDN_CONV = 3
RET_DK = HEAD_DIM // 2
ROPE_BASE = 10000.0
CHUNK = 64
MLP_HIDDEN = 4 * D_MODEL
EPS = 1e-6
NEG_INF = -1e30

NA_WIDTHS = (GROUP_W, GROUP_W, GROUP_W)
GLA_WIDTHS = (N_HEADS * GLA_DK, N_HEADS * GLA_DK, GROUP_W, GROUP_W, 2 * GLA_GATE_RANK)
GDN_WIDTHS = (3 * GROUP_W, GROUP_W, 2 * N_HEADS, 2 * N_HEADS)
RET_WIDTHS = (N_HEADS * RET_DK, N_HEADS * RET_DK, GROUP_W, GROUP_W)
GROUP_COLS = (sum(NA_WIDTHS), sum(GLA_WIDTHS), sum(GDN_WIDTHS), sum(RET_WIDTHS))
IN_COLS = sum(GROUP_COLS)

kernel_name = 'hybrid_parallel_heads_dit_block'


def _rmsnorm(t, w):
    tf = t.astype(jnp.float32)
    y = tf * lax.rsqrt(jnp.mean(tf * tf, axis=-1, keepdims=True) + EPS)
    return (y * w.astype(jnp.float32)).astype(t.dtype)


def _merge(t):
    b, h, n, d = t.shape
    return t.transpose(0, 2, 1, 3).reshape(b, n, h * d)


def _heads(t):
    b, n, _ = t.shape
    return t.reshape(b, n, N_HEADS, -1).transpose(0, 2, 1, 3)


def _groupnorm(t, w_flat):
    tf = t.astype(jnp.float32)
    mu = jnp.mean(tf, axis=-1, keepdims=True)
    var = jnp.mean(jnp.square(tf - mu), axis=-1, keepdims=True)
    return _merge((tf - mu) * lax.rsqrt(var + EPS)) * w_flat.astype(jnp.float32)


def _l2norm(t):
    return t * lax.rsqrt(jnp.sum(t * t, axis=-1, keepdims=True) + EPS)


def _split(t, widths):
    return jnp.split(t, [int(i) for i in np.cumsum(widths)[:-1]], axis=-1)


def _modulate(h, shift, scale):
    return h * (1.0 + scale) + shift


def _sq_relu_mlp(h, w1, w2):
    return jnp.square(jax.nn.relu(h @ w1)) @ w2


def _rope_1d(t, pos):
    half = t.shape[-1] // 2
    freqs = ROPE_BASE ** (-jnp.arange(half, dtype=jnp.float32) / half)
    ang = pos.astype(jnp.float32)[:, None] * freqs[None, :]
    cos, sin = jnp.cos(ang), jnp.sin(ang)
    t1, t2 = t[..., :half], t[..., half:]
    return jnp.concatenate([t1 * cos - t2 * sin, t2 * cos + t1 * sin], axis=-1)


def _axial_rope(t):
    pos = jnp.arange(t.shape[2])
    half = t.shape[-1] // 2
    return jnp.concatenate([_rope_1d(t[..., :half], pos // GRID_W), _rope_1d(t[..., half:], pos % GRID_W)], axis=-1)


def _short_conv(t, w):
    return lax.conv_general_dilated(t, w[:, None, :].astype(t.dtype), window_strides=(1,),
                                    padding=[(GDN_CONV // 2, GDN_CONV // 2)],
                                    dimension_numbers=('NWC', 'WIO', 'NWC'), feature_group_count=t.shape[-1])


def _chunks(t, n):
    return t.reshape(t.shape[:2] + (n, CHUNK) + t.shape[3:])


def _to_scan(t):
    return jnp.moveaxis(t, 2, 0)


def _from_scan(t):
    t = jnp.moveaxis(t, 0, 2)
    return t.reshape(t.shape[:2] + (-1,) + t.shape[4:])


def _decay_matrix(g, strict):
    mask = np.tril(np.ones((CHUNK, CHUNK), dtype=bool), k=-1 if strict else 0)
    return jnp.exp(jnp.where(mask, g[..., :, None] - g[..., None, :], -jnp.inf))


def _gla_chunk(q, k, v, lg, s0):
    b, h, T, _ = q.shape
    n = T // CHUNK
    q, k, v, lg = (_chunks(t, n) for t in (q, k, v, lg))
    cum = jnp.cumsum(lg, axis=3)
    qd = q * jnp.exp(cum)
    kd = k * jnp.exp(-cum)
    mask = np.tril(np.ones((CHUNK, CHUNK), dtype=bool))
    att = jnp.where(mask, jnp.einsum('bhncd,bhnsd->bhncs', qd, kd), 0.0)
    o_intra = jnp.einsum('bhncs,bhnsv->bhncv', att, v)
    k_end = k * jnp.exp(cum[:, :, :, -1:] - cum)
    g_end = jnp.exp(cum[:, :, :, -1])

    def step(S, xs):
        qd_n, ke_n, v_n, ge_n = xs
        o = jnp.einsum('bhcd,bhdv->bhcv', qd_n, S)
        S = ge_n[..., None] * S + jnp.einsum('bhcd,bhcv->bhdv', ke_n, v_n)
        return S, o

    s_fin, o_inter = lax.scan(step, s0, tuple(_to_scan(t) for t in (qd, k_end, v, g_end)))
    return o_intra.reshape(b, h, T, -1) + _from_scan(o_inter), s_fin


def _gdn_chunk(q, k, v, beta, lg, s0):
    n = q.shape[2] // CHUNK
    q, k, v, beta, lg = (_chunks(t, n) for t in (q, k, v, beta, lg))
    g = jnp.cumsum(lg, axis=-1)
    dec_incl = _decay_matrix(g, strict=False)
    dec_strict = _decay_matrix(g, strict=True)
    kk = jnp.einsum('bhnid,bhnjd->bhnij', k, k)
    m = jnp.eye(CHUNK, dtype=kk.dtype) + kk * dec_strict * beta[..., None, :]
    gam = jnp.exp(g)
    u = lax.linalg.triangular_solve(m, v, left_side=True, lower=True, unit_diagonal=True)
    w = lax.linalg.triangular_solve(m, k * gam[..., None], left_side=True, lower=True, unit_diagonal=True)
    a_qk = jnp.einsum('bhnid,bhnjd->bhnij', q, k) * dec_incl * beta[..., None, :]
    q_g = q * gam[..., None]
    k_end = k * (jnp.exp(g[..., -1:] - g) * beta)[..., None]
    g_end = jnp.exp(g[..., -1])

    def step(S, xs):
        u_n, w_n, qg_n, a_n, ke_n, ge_n = xs
        delta = u_n - jnp.einsum('bhck,bhkv->bhcv', w_n, S)
        o = jnp.einsum('bhck,bhkv->bhcv', qg_n, S) + jnp.einsum('bhcs,bhsv->bhcv', a_n, delta)
        S = ge_n[..., None, None] * S + jnp.einsum('bhck,bhcv->bhkv', ke_n, delta)
        return S, o

    s_fin, o = lax.scan(step, s0, tuple(_to_scan(t) for t in (u, w, q_g, a_qk, k_end, g_end)))
    return _from_scan(o), s_fin


def _decay_chunk(q, k, v, lg, s0):
    b, h, T, _ = q.shape
    n = T // CHUNK
    q, k, v, lg = (_chunks(t, n) for t in (q, k, v, lg))
    g = jnp.cumsum(lg, axis=-1)
    att = jnp.einsum('bhncd,bhnsd->bhncs', q, k) * _decay_matrix(g, strict=False)
    o_intra = jnp.einsum('bhncs,bhnsv->bhncv', att, v)
    q_g = q * jnp.exp(g)[..., None]
    k_end = k * jnp.exp(g[..., -1:] - g)[..., None]
    g_end = jnp.exp(g[..., -1])

    def step(S, xs):
        qg_n, ke_n, v_n, ge_n = xs
        o = jnp.einsum('bhcd,bhdv->bhcv', qg_n, S)
        S = ge_n[..., None, None] * S + jnp.einsum('bhcd,bhcv->bhdv', ke_n, v_n)
        return S, o

    s_fin, o_inter = lax.scan(step, s0, tuple(_to_scan(t) for t in (q_g, k_end, v, g_end)))
    return o_intra.reshape(b, h, T, -1) + _from_scan(o_inter), s_fin


def _flip(t):
    return jnp.flip(t, axis=2)


def _identity(t):
    return t


def _bidir(scan_fn, ctx_shared, ctx_dirs, lat_shared, lat_dirs, s0, need_ctx):
    o_ctx, o_lat = [], []
    for d in range(2):
        fl = _flip if d == 1 else _identity
        oc, s_ctx = scan_fn(*[fl(t) for t in ctx_shared + ctx_dirs[d]], s0)
        ol, _ = scan_fn(*[fl(t) for t in lat_shared + lat_dirs[d]], s_ctx)
        o_ctx.append(fl(oc))
        o_lat.append(fl(ol))
    return (o_ctx[0] + o_ctx[1] if need_ctx else None), o_lat[0] + o_lat[1]


def _natten_mixer(p_c, p_l, q_norm, k_norm, rpb, need_ctx):
    dtype = p_l.dtype
    scale = HEAD_DIM ** -0.5

    def qkv(p):
        q, k, v = _split(p, NA_WIDTHS)
        return _rmsnorm(_heads(q), q_norm), _rmsnorm(_heads(k), k_norm), _heads(v)

    q_c, k_c, v_c = qkv(p_c)
    q_l, k_l, v_l = qkv(p_l)
    o_c = None
    if need_ctx:
        s = jnp.einsum('bhqd,bhkd->bhqk', q_c, k_c).astype(jnp.float32) * scale
        o_c = _merge(jnp.einsum('bhqk,bhkd->bhqd', jax.nn.softmax(s, axis=-1).astype(dtype), v_c))
    b, h, S, d = q_l.shape
    rows = S // GRID_W
    kh = min(NA_WIN_ROWS, rows)
    r = np.arange(rows)
    row_idx = np.clip(r - kh // 2, 0, rows - kh)[:, None] + np.arange(kh)[None, :]
    row_off = row_idx - r[:, None] + (NA_WIN_ROWS - 1)
    qg = q_l.reshape(b, h, rows, GRID_W, d)
    kr = k_l.reshape(b, h, rows, GRID_W, d)[:, :, row_idx]
    vr = v_l.reshape(b, h, rows, GRID_W, d)[:, :, row_idx]
    nw = kh * NA_KEY_COLS
    outs = []
    for qc0 in range(0, GRID_W, NA_COL_BLOCK):
        qcols = qc0 + np.arange(NA_COL_BLOCK)
        cs = np.clip(qcols - NA_WIN_COLS // 2, 0, GRID_W - NA_WIN_COLS)
        kc0 = int(np.clip(qc0 - NA_WIN_COLS // 2, 0, GRID_W - NA_KEY_COLS))
        kcols = kc0 + np.arange(NA_KEY_COLS)
        valid = (kcols[None, :] >= cs[:, None]) & (kcols[None, :] < cs[:, None] + NA_WIN_COLS)
        col_off = np.clip(kcols[None, :] - qcols[:, None], -(NA_WIN_COLS - 1), NA_WIN_COLS - 1) + NA_WIN_COLS - 1
        bias = rpb[:, row_off[:, None, :, None], col_off[None, :, None, :]].astype(jnp.float32)
        bias = jnp.where(valid[None, None, :, None, :], bias, NEG_INF)
        qb = qg[:, :, :, qc0:qc0 + NA_COL_BLOCK]
        kb = kr[:, :, :, :, kc0:kc0 + NA_KEY_COLS]
        vb = vr[:, :, :, :, kc0:kc0 + NA_KEY_COLS]
        s_win = jnp.einsum('bhrqd,bhrikd->bhrqik', qb, kb).astype(jnp.float32) * scale + bias[None]
        s_ctx = jnp.einsum('bhrqd,bhcd->bhrqc', qb, k_c).astype(jnp.float32) * scale
        pr = jax.nn.softmax(jnp.concatenate([s_win.reshape(b, h, rows, NA_COL_BLOCK, nw), s_ctx], axis=-1), axis=-1).astype(dtype)
        o = (jnp.einsum('bhrqik,bhrikd->bhrqd', pr[..., :nw].reshape(b, h, rows, NA_COL_BLOCK, kh, NA_KEY_COLS), vb)
             + jnp.einsum('bhrqc,bhcd->bhrqd', pr[..., nw:], v_c))
        outs.append(o)
    o_l = jnp.concatenate(outs, axis=3).reshape(b, h, S, d)
    return o_c, _merge(o_l)


def _gla_mixer(p_c, p_l, gate_up, gate_b, o_norm, need_ctx):
    def prep(p):
        q, k, v, g, rk = _split(p, GLA_WIDTHS)
        rk = rk.astype(jnp.float32)
        dirs = tuple((_heads(jax.nn.log_sigmoid(rk[..., d * GLA_GATE_RANK:(d + 1) * GLA_GATE_RANK] @ gate_up[d] + gate_b[d]) / GLA_GATE_TAU),)
                     for d in range(2))
        shared = (_heads(q.astype(jnp.float32)) * GLA_DK ** -0.5, _heads(k.astype(jnp.float32)), _heads(v.astype(jnp.float32)))
        return shared, dirs, g

    sh_c, dir_c, g_c = prep(p_c)
    sh_l, dir_l, g_l = prep(p_l)
    s0 = jnp.zeros((p_l.shape[0], N_HEADS, GLA_DK, HEAD_DIM), jnp.float32)
    o_c, o_l = _bidir(_gla_chunk, sh_c, dir_c, sh_l, dir_l, s0, need_ctx)

    def finish(o, g):
        return (_merge(_rmsnorm(o, o_norm)) * jax.nn.silu(g.astype(jnp.float32))).astype(g.dtype)

    return (finish(o_c, g_c) if need_ctx else None), finish(o_l, g_l)


def _gdn_mixer(p_c, p_l, conv_w, a_log, dt_bias, o_norm, need_ctx):
    a_log = a_log.astype(jnp.float32)
    dt_bias = dt_bias.astype(jnp.float32)

    def prep(p):
        qkv, z, a, bt = _split(p, GDN_WIDTHS)
        qkv = jax.nn.silu(_short_conv(qkv, conv_w).astype(jnp.float32))
        q, k, v = jnp.split(qkv, 3, axis=-1)
        a = a.astype(jnp.float32)
        bt = bt.astype(jnp.float32)
        dirs = tuple((jnp.swapaxes(jax.nn.sigmoid(bt[..., d * N_HEADS:(d + 1) * N_HEADS]), 1, 2),
                      jnp.swapaxes(-jnp.exp(a_log[d]) * jax.nn.softplus(a[..., d * N_HEADS:(d + 1) * N_HEADS] + dt_bias[d]), 1, 2))
                     for d in range(2))
        shared = (_l2norm(_heads(q)) * GDN_DK ** -0.5, _l2norm(_heads(k)), _heads(v))
        return shared, dirs, z

    sh_c, dir_c, z_c = prep(p_c)
    sh_l, dir_l, z_l = prep(p_l)
    s0 = jnp.zeros((p_l.shape[0], N_HEADS, GDN_DK, HEAD_DIM), jnp.float32)
    o_c, o_l = _bidir(_gdn_chunk, sh_c, dir_c, sh_l, dir_l, s0, need_ctx)

    def finish(o, z):
        return (_merge(_rmsnorm(o, o_norm)) * jax.nn.silu(z.astype(jnp.float32))).astype(z.dtype)

    return (finish(o_c, z_c) if need_ctx else None), finish(o_l, z_l)


def _ret_mixer(p_c, p_l, decay_logit, gn_w, need_ctx):
    lgam = jax.nn.log_sigmoid(decay_logit.astype(jnp.float32))

    def prep(p, rotate):
        q, k, v, g = _split(p, RET_WIDTHS)
        q = _heads(q.astype(jnp.float32))
        k = _heads(k.astype(jnp.float32)) * RET_DK ** -0.5
        if rotate:
            q, k = _axial_rope(q), _axial_rope(k)
        dirs = tuple((jnp.broadcast_to(lgam[d][None, :, None], q.shape[:3]),) for d in range(2))
        return (q, k, _heads(v.astype(jnp.float32))), dirs, g

    sh_c, dir_c, g_c = prep(p_c, False)
    sh_l, dir_l, g_l = prep(p_l, True)
    s0 = jnp.zeros((p_l.shape[0], N_HEADS, RET_DK, HEAD_DIM), jnp.float32)
    o_c, o_l = _bidir(_decay_chunk, sh_c, dir_c, sh_l, dir_l, s0, need_ctx)

    def finish(o, g):
        return (_groupnorm(o, gn_w) * jax.nn.silu(g.astype(jnp.float32))).astype(g.dtype)

    return (finish(o_c, g_c) if need_ctx else None), finish(o_l, g_l)


def setup_inputs(seed: int = 0) -> dict:
    key = jax.random.key(seed)
    ks = jax.random.split(key, 24)
    f = jnp.float32
    L, D, H = DEPTH, D_MODEL, N_HEADS

    def nrm(i, shape, scale):
        return jax.random.normal(ks[i], shape, f) * scale

    gamma0 = 1.0 - 2.0 ** (-5.0 - np.arange(H, dtype=np.float32))
    decay_logit0 = jnp.asarray(np.log(gamma0 / (1.0 - gamma0)), f)
    dt = jnp.exp(jax.random.uniform(ks[16], (L, 2, H), f, np.log(1e-3), np.log(1e-1)))
    return {
        'x': nrm(0, (BATCH, SEQ, D), 1.0),
        'c': nrm(1, (BATCH, D), 1.0),
        'ctx': nrm(2, (BATCH, CTX_LEN, D), 1.0),
        'c_ctx': nrm(3, (D,), 1.0),
        'ada_w': nrm(4, (L, D, 6 * D), 0.5 * D ** -0.5),
        'ada_b': nrm(5, (L, 6 * D), 0.02),
        'norm1_w': 1.0 + nrm(6, (L, D), 0.02),
        'norm2_w': 1.0 + nrm(7, (L, D), 0.02),
        'w_in': nrm(8, (L, D, IN_COLS), D ** -0.5),
        'w_out': nrm(9, (L, N_MIXERS * GROUP_W, D), (N_MIXERS * GROUP_W) ** -0.5),
        'na_q_norm': 1.0 + nrm(10, (L, HEAD_DIM), 0.02),
        'na_k_norm': 1.0 + nrm(11, (L, HEAD_DIM), 0.02),
        'na_rpb': nrm(12, (L, H, 2 * NA_WIN_ROWS - 1, 2 * NA_WIN_COLS - 1), 0.1),
        'gla_gate_up': nrm(13, (L, 2, GLA_GATE_RANK, H * GLA_DK), GLA_GATE_RANK ** -0.5),
        'gla_gate_b': nrm(14, (L, 2, H * GLA_DK), 0.1),
        'gla_o_norm': 1.0 + nrm(15, (L, HEAD_DIM), 0.02),
        'gdn_conv_w': nrm(17, (L, GDN_CONV, 3 * GROUP_W), GDN_CONV ** -0.5),
        'gdn_a_log': jnp.log(jax.random.uniform(ks[18], (L, 2, H), f, 1.0, 16.0)),
        'gdn_dt_bias': dt + jnp.log(-jnp.expm1(-dt)),
        'gdn_o_norm': 1.0 + nrm(19, (L, HEAD_DIM), 0.02),
        'ret_decay_logit': decay_logit0 + nrm(20, (L, 2, H), 0.05),
        'ret_gn_w': 1.0 + nrm(21, (L, GROUP_W), 0.02),
        'mlp_w1': nrm(22, (L, D, MLP_HIDDEN), D ** -0.5),
        'mlp_w2': nrm(23, (L, MLP_HIDDEN, D), MLP_HIDDEN ** -0.5),
    }


def reference(x, c, ctx, c_ctx, ada_w, ada_b, norm1_w, norm2_w, w_in, w_out,
              na_q_norm, na_k_norm, na_rpb, gla_gate_up, gla_gate_b, gla_o_norm,
              gdn_conv_w, gdn_a_log, gdn_dt_bias, gdn_o_norm, ret_decay_logit, ret_gn_w,
              mlp_w1, mlp_w2):
    xl, xc = x, ctx
    for layer in range(DEPTH):
        need_ctx = layer < DEPTH - 1
        mod_l = [m[:, None, :] for m in jnp.split(jax.nn.silu(c) @ ada_w[layer] + ada_b[layer], 6, axis=-1)]
        mod_c = jnp.split(jax.nn.silu(c_ctx) @ ada_w[layer] + ada_b[layer], 6, axis=-1)
        hl = _modulate(_rmsnorm(xl, norm1_w[layer]), mod_l[0], mod_l[1])
        hc = _modulate(_rmsnorm(xc, norm1_w[layer]), mod_c[0], mod_c[1])
        pl = _split(hl @ w_in[layer], GROUP_COLS)
        pc = _split(hc @ w_in[layer], GROUP_COLS)
        na_c, na_l = _natten_mixer(pc[0], pl[0], na_q_norm[layer], na_k_norm[layer], na_rpb[layer], need_ctx)
        gl_c, gl_l = _gla_mixer(pc[1], pl[1], gla_gate_up[layer], gla_gate_b[layer], gla_o_norm[layer], need_ctx)
        gd_c, gd_l = _gdn_mixer(pc[2], pl[2], gdn_conv_w[layer], gdn_a_log[layer], gdn_dt_bias[layer], gdn_o_norm[layer], need_ctx)
        rt_c, rt_l = _ret_mixer(pc[3], pl[3], ret_decay_logit[layer], ret_gn_w[layer], need_ctx)
        xl = xl + mod_l[2] * (jnp.concatenate([na_l, gl_l, gd_l, rt_l], axis=-1) @ w_out[layer])
        xl = xl + mod_l[5] * _sq_relu_mlp(_modulate(_rmsnorm(xl, norm2_w[layer]), mod_l[3], mod_l[4]), mlp_w1[layer], mlp_w2[layer])
        if need_ctx:
            xc = xc + mod_c[2] * (jnp.concatenate([na_c, gl_c, gd_c, rt_c], axis=-1) @ w_out[layer])
            xc = xc + mod_c[5] * _sq_relu_mlp(_modulate(_rmsnorm(xc, norm2_w[layer]), mod_c[3], mod_c[4]), mlp_w1[layer], mlp_w2[layer])
    return xl
```

```python
import functools

import numpy as np
import jax
import jax.numpy as jnp
from jax import lax
from jax.experimental import pallas as pl
from jax.experimental.pallas import tpu as pltpu

BF = jnp.bfloat16
F32 = jnp.float32

N_HEADS = 4
HEAD_DIM = 128
GROUP_W = N_HEADS * HEAD_DIM
GRID_W = 64
NA_WIN_ROWS = 8
NA_WIN_COLS = 16
GLA_DK = 64
GLA_GATE_RANK = 16
GLA_GATE_TAU = 16.0
GDN_DK = 128
RET_DK = 64
ROPE_BASE = 10000.0
CHUNK = 64
EPS = 1e-6
NEG_INF = -1e30

LANE = 128
N_BIG = 52 * LANE
N_SMALL = LANE
NA_Q, NA_K, NA_V = 0, 4, 8
GLA_Q, GLA_K, GLA_V, GLA_G = 12, 14, 16, 20
GDN_Q, GDN_K, GDN_V, GDN_Z = 24, 28, 32, 36
RET_Q, RET_K, RET_V, RET_G = 40, 42, 44, 48
SM_RK, SM_A, SM_BT = 0, 32, 40

VMEM_LIMIT = 56 * 1024 * 1024


def _cp(sem, vmem=VMEM_LIMIT):
    return pltpu.CompilerParams(dimension_semantics=sem, vmem_limit_bytes=vmem)


def _dot(a, b):
    return jnp.dot(a, b, preferred_element_type=F32)


def _dot_nt(a, b):
    return lax.dot_general(a, b, (((1,), (1,)), ((), ())), preferred_element_type=F32)


def _dot_tn(a, b):
    return lax.dot_general(a, b, (((0,), (0,)), ((), ())), preferred_element_type=F32)


def _split3(x):
    hi = x.astype(BF)
    r1 = x - hi.astype(F32)
    mid = r1.astype(BF)
    lo = (r1 - mid.astype(F32)).astype(BF)
    return hi, mid, lo


def _exact_dot(m_bf, x):
    hi, mid, lo = _split3(x)
    return _dot(m_bf, hi) + _dot(m_bf, mid) + _dot(m_bf, lo)


def _sigmoid(x):
    return 1.0 / (1.0 + jnp.exp(-x))


def _silu(x):
    return x * _sigmoid(x)


def _log_sigmoid(x):
    return jnp.minimum(x, 0.0) - jnp.log(1.0 + jnp.exp(-jnp.abs(x)))


def _softplus(x):
    return jnp.maximum(x, 0.0) + jnp.log(1.0 + jnp.exp(-jnp.abs(x)))


def _ln_mod(x, nw, shift, scale):
    ms = jnp.mean(x * x, axis=-1, keepdims=True)
    return (x * lax.rsqrt(ms + EPS) * nw) * (1.0 + scale) + shift


def _tri_consts():
    ri = lax.broadcasted_iota(jnp.int32, (CHUNK, CHUNK), 0)
    ci = lax.broadcasted_iota(jnp.int32, (CHUNK, CHUNK), 1)
    incl = (ri >= ci, ri <= ci)
    strict = (ri > ci, ri < ci)
    tri_bf = tuple(jnp.where(m, 1.0, 0.0).astype(BF) for m in incl)
    return ri, ci, incl, strict, tri_bf


def _ada_kernel(c_ref, w_ref, b_ref, o_ref):
    sc = _silu(c_ref[...]).astype(BF)
    o_ref[...] = _dot(sc, w_ref[...].astype(BF)) + b_ref[...]


def _ada(cc, ada_w, ada_b):
    L, D, N6 = ada_w.shape
    R = cc.shape[0]
    tn = 1024 if N6 % 1024 == 0 else N6
    return pl.pallas_call(
        _ada_kernel,
        grid=(L, N6 // tn),
        in_specs=[pl.BlockSpec((R, D), lambda l, j: (0, 0)),
                  pl.BlockSpec((None, D, tn), lambda l, j: (l, 0, j)),
                  pl.BlockSpec((None, 1, tn), lambda l, j: (l, 0, j))],
        out_specs=pl.BlockSpec((None, R, tn), lambda l, j: (l, 0, j)),
        out_shape=jax.ShapeDtypeStruct((L, R, N6), F32),
        compiler_params=_cp(("parallel", "parallel")),
        name="ada_ln",
    )(cc, ada_w, ada_b.reshape(L, 1, N6))


def _inproj_kernel(x_ref, mod_ref, nw_ref, w_ref, ws_ref, o_ref, os_ref, h_ref):
    @pl.when(pl.program_id(1) == 0)
    def _():
        h = _ln_mod(x_ref[...], nw_ref[...], mod_ref[0:1, :], mod_ref[1:2, :]).astype(BF)
        h_ref[...] = h
        os_ref[...] = _dot(h, ws_ref[...])

    o_ref[...] = _dot(h_ref[...], w_ref[...]).astype(o_ref.dtype)


def _in_proj(x3, mod, const_row, nw, w_big, w_small, tm, tn):
    Bn, Tn, D = x3.shape
    nt = Tn // tm
    if const_row is None:
        mod_map = lambda i, j: (i // nt, 0, 0)
    else:
        mod_map = lambda i, j: (const_row, 0, 0)
    return pl.pallas_call(
        _inproj_kernel,
        grid=(Bn * nt, N_BIG // tn),
        in_specs=[pl.BlockSpec((None, tm, D), lambda i, j: (i // nt, i % nt, 0)),
                  pl.BlockSpec((None, 6, D), mod_map),
                  pl.BlockSpec((1, D), lambda i, j: (0, 0)),
                  pl.BlockSpec((D, tn), lambda i, j: (0, j)),
                  pl.BlockSpec((D, N_SMALL), lambda i, j: (0, 0))],
        out_specs=[pl.BlockSpec((None, tm, tn), lambda i, j: (i // nt, i % nt, j)),
                   pl.BlockSpec((None, tm, N_SMALL), lambda i, j: (i // nt, i % nt, 0))],
        out_shape=[jax.ShapeDtypeStruct((Bn, Tn, N_BIG), BF),
                   jax.ShapeDtypeStruct((Bn, Tn, N_SMALL), F32)],
        scratch_shapes=[pltpu.VMEM((tm, D), BF)],
        compiler_params=_cp(("parallel", "arbitrary")),
        name="in_proj",
    )(x3, mod, nw, w_big, w_small)


def _outproj_kernel(x_ref, y0, y1, y2, y3, w_ref, mod_ref, o_ref):
    acc = _dot(y0[...], w_ref[0 * GROUP_W:1 * GROUP_W, :])
    acc += _dot(y1[...], w_ref[1 * GROUP_W:2 * GROUP_W, :])
    acc += _dot(y2[...], w_ref[2 * GROUP_W:3 * GROUP_W, :])
    acc += _dot(y3[...], w_ref[3 * GROUP_W:4 * GROUP_W, :])
    o_ref[...] = x_ref[...] + mod_ref[2:3, :] * acc


def _out_proj(x3, ys, w_out, mod, const_row, tm):
    Bn, Tn, D = x3.shape
    nt = Tn // tm
    if const_row is None:
        mod_map = lambda i: (i // nt, 0, 0)
    else:
        mod_map = lambda i: (const_row, 0, 0)
    row_map = lambda i: (i // nt, i % nt, 0)
    return pl.pallas_call(
        _outproj_kernel,
        grid=(Bn * nt,),
        in_specs=[pl.BlockSpec((None, tm, D), row_map)]
                 + [pl.BlockSpec((None, tm, GROUP_W), row_map)] * 4
                 + [pl.BlockSpec((4 * GROUP_W, D), lambda i: (0, 0)),
                    pl.BlockSpec((None, 6, D), mod_map)],
        out_specs=pl.BlockSpec((None, tm, D), row_map),
        out_shape=jax.ShapeDtypeStruct((Bn, Tn, D), F32),
        compiler_params=_cp(("parallel",)),
        name="out_proj",
    )(x3, *ys, w_out, mod)


def _mlp_kernel(x_ref, mod_ref, nw_ref, w1_ref, w2_ref, o_ref, h_ref, *, nk):
    k = pl.program_id(1)

    @pl.when(k == 0)
    def _():
        h_ref[...] = _ln_mod(x_ref[...], nw_ref[...], mod_ref[3:4, :], mod_ref[4:5, :]).astype(BF)
        o_ref[...] = jnp.zeros_like(o_ref)

    hid = jnp.maximum(_dot(h_ref[...], w1_ref[...]), 0.0)
    o_ref[...] += _dot((hid * hid).astype(BF), w2_ref[...])

    @pl.when(k == nk - 1)
    def _():
        o_ref[...] = x_ref[...] + mod_ref[5:6, :] * o_ref[...]


def _mlp(x3, mod, const_row, nw, w1, w2, tm, th):
    Bn, Tn, D = x3.shape
    Hd = w1.shape[1]
    nt = Tn // tm
    nk = Hd // th
    if const_row is None:
        mod_map = lambda i, k: (i // nt, 0, 0)
    else:
        mod_map = lambda i, k: (const_row, 0, 0)
    row_map = lambda i, k: (i // nt, i % nt, 0)
    return pl.pallas_call(
        functools.partial(_mlp_kernel, nk=nk),
        grid=(Bn * nt, nk),
        in_specs=[pl.BlockSpec((None, tm, D), row_map),
                  pl.BlockSpec((None, 6, D), mod_map),
                  pl.BlockSpec((1, D), lambda i, k: (0, 0)),
                  pl.BlockSpec((D, th), lambda i, k: (0, k)),
                  pl.BlockSpec((th, D), lambda i, k: (k, 0))],
        out_specs=pl.BlockSpec((None, tm, D), row_map),
        out_shape=jax.ShapeDtypeStruct((Bn, Tn, D), F32),
        scratch_shapes=[pltpu.VMEM((tm, D), BF)],
        compiler_params=_cp(("parallel", "arbitrary")),
        name="mlp",
    )(x3, mod, nw, w1, w2)


def _rms_head(x, w):
    x = x.astype(F32)
    return x * lax.rsqrt(jnp.mean(x * x, axis=-1, keepdims=True) + EPS) * w


def _natten_kernel(ql_ref, kl_ref, vl_ref, qc_ref, kc_ref, vc_ref, qw_ref, kw_ref, bias_ref, *rest,
                   need_ctx, rows, kh):
    if need_ctx:
        ol_ref, oc_ref, qs, ks = rest
    else:
        ol_ref, qs, ks = rest
    scale = HEAD_DIM ** -0.5
    qw = qw_ref[...]
    kw = kw_ref[...]
    qs[...] = (_rms_head(ql_ref[...], qw) * scale).astype(BF)
    ks[...] = _rms_head(kl_ref[...], kw).astype(BF)
    kc = _rms_head(kc_ref[...], kw).astype(BF)
    vc = vc_ref[...]
    if need_ctx:
        qc = (_rms_head(qc_ref[...], qw) * scale).astype(BF)
        s = _dot_nt(qc, kc)
        p = jnp.exp(s - jnp.max(s, axis=-1, keepdims=True))
        l = jnp.sum(p, axis=-1, keepdims=True)
        oc_ref[...] = (_dot(p.astype(BF), vc) / l).astype(oc_ref.dtype)

    def body(r, carry):
        rs = jnp.clip(r - kh // 2, 0, rows - kh)
        d = r - rs
        q0 = pl.multiple_of(r * GRID_W, GRID_W)
        k0 = pl.multiple_of(rs * GRID_W, GRID_W)
        q = qs[pl.ds(q0, GRID_W), :]
        kwin = ks[pl.ds(k0, kh * GRID_W), :]
        vwin = vl_ref[pl.ds(k0, kh * GRID_W), :]
        sw = _dot_nt(q, kwin) + bias_ref[d]
        sc = _dot_nt(q, kc)
        m = jnp.maximum(jnp.max(sw, axis=-1, keepdims=True), jnp.max(sc, axis=-1, keepdims=True))
        pw = jnp.exp(sw - m)
        pc = jnp.exp(sc - m)
        l = jnp.sum(pw, axis=-1, keepdims=True) + jnp.sum(pc, axis=-1, keepdims=True)
        o = (_dot(pw.astype(BF), vwin) + _dot(pc.astype(BF), vc)) / l
        ol_ref[pl.ds(q0, GRID_W), :] = o.astype(ol_ref.dtype)
        return carry

    lax.fori_loop(0, rows, body, 0)


def _natten_bias(rpb, kh):
    d = np.arange(kh)[:, None, None, None]
    q = np.arange(GRID_W)[None, :, None, None]
    i = np.arange(kh)[None, None, :, None]
    kc = np.arange(GRID_W)[None, None, None, :]
    row_off = np.broadcast_to(i - d + (NA_WIN_ROWS - 1), (kh, GRID_W, kh, GRID_W))
    cs = np.clip(q - NA_WIN_COLS // 2, 0, GRID_W - NA_WIN_COLS)
    valid = np.broadcast_to((kc >= cs) & (kc < cs + NA_WIN_COLS), (kh, GRID_W, kh, GRID_W))
    col_off = np.broadcast_to(np.clip(kc - q, -(NA_WIN_COLS - 1), NA_WIN_COLS - 1) + NA_WIN_COLS - 1,
                              (kh, GRID_W, kh, GRID_W))
    bias = rpb[:, row_off, col_off].astype(F32)
    bias = jnp.where(valid[None], bias, NEG_INF)
    return bias.reshape(rpb.shape[0], kh, GRID_W, kh * GRID_W)


def _natten(p_l, p_c, qw, kw, rpb, need_ctx):
    B, S, _ = p_l.shape
    CT = p_c.shape[1]
    rows = S // GRID_W
    kh = min(NA_WIN_ROWS, rows)
    bias = _natten_bias(rpb, kh)
    lat = lambda off: pl.BlockSpec((None, S, LANE), lambda b, h: (b, 0, off + h))
    ctx = lambda off: pl.BlockSpec((None, CT, LANE), lambda b, h: (b, 0, off + h))
    vec = pl.BlockSpec((1, LANE), lambda b, h: (0, 0))
    out_specs = [pl.BlockSpec((None, S, LANE), lambda b, h: (b, 0, h))]
    out_shape = [jax.ShapeDtypeStruct((B, S, GROUP_W), BF)]
    if need_ctx:
        out_specs.append(pl.BlockSpec((None, CT, LANE), lambda b, h: (b, 0, h)))
        out_shape.append(jax.ShapeDtypeStruct((B, CT, GROUP_W), BF))
    res = pl.pallas_call(
        functools.partial(_natten_kernel, need_ctx=need_ctx, rows=rows, kh=kh),
        grid=(B, N_HEADS),
        in_specs=[lat(NA_Q), lat(NA_K), lat(NA_V), ctx(NA_Q), ctx(NA_K), ctx(NA_V), vec, vec,
                  pl.BlockSpec((None, kh, GRID_W, kh * GRID_W), lambda b, h: (h, 0, 0, 0))],
        out_specs=out_specs,
        out_shape=out_shape,
        scratch_shapes=[pltpu.VMEM((S, LANE), BF), pltpu.VMEM((S, LANE), BF)],
        compiler_params=_cp(("parallel", "parallel")),
        name="natten",
    )(p_l, p_l, p_l, p_c, p_c, p_c, qw.reshape(1, LANE), kw.reshape(1, LANE), bias)
    return (res[1] if need_ctx else None), res[0]


def _head_masks():
    lane = lax.broadcasted_iota(jnp.int32, (CHUNK, LANE), 1)
    return (lane < 64, lane >= 64)


def _finish_rms(o, nw, g):
    y = o * lax.rsqrt(jnp.mean(o * o, axis=-1, keepdims=True) + EPS) * nw
    return y * _silu(g.astype(F32))


def _gla_kernel(qc_ref, kc_ref, vc_ref, gc_ref, sc_ref, ql_ref, kl_ref, vl_ref, gl_ref, sl_ref,
                gup_ref, gb_ref, nw_ref, *rest, need_ctx, nc_ctx, nc_lat):
    if need_ctx:
        ol_ref, oc_ref, lg_ref, st_ref, of_ref, ob_ref = rest
    else:
        ol_ref, lg_ref, st_ref, of_ref, ob_ref = rest
        oc_ref = None
    CT = nc_ctx * CHUNK
    S = nc_lat * CHUNK
    _, _, incl, _, tri_bf = _tri_consts()
    hmask = _head_masks()

    for d in range(2):
        for (s_ref, r0, n) in ((sc_ref, 0, CT), (sl_ref, CT, S)):
            z = _dot(s_ref[...].astype(BF), gup_ref[d]) + gb_ref[d]
            lg_ref[d, r0:r0 + n, :] = _log_sigmoid(z) * (1.0 / GLA_GATE_TAU)
    st_ref[...] = jnp.zeros_like(st_ref)

    def segment(q_ref, k_ref, v_ref, row0, nch, write):
        def body(n, carry):
            for d in range(2):
                c = n if d == 0 else nch - 1 - n
                r0 = pl.multiple_of(c * CHUNK, CHUNK)
                q2 = q_ref[pl.ds(r0, CHUNK), :].astype(F32)
                k2 = k_ref[pl.ds(r0, CHUNK), :].astype(F32)
                lg = lg_ref[d, pl.ds(row0 + r0, CHUNK), :]
                cum = _exact_dot(tri_bf[d], lg)
                tot = cum[CHUNK - 1:CHUNK, :] if d == 0 else cum[0:1, :]
                qd = q2 * (jnp.exp(cum) * (GLA_DK ** -0.5))
                kd = (k2 * jnp.exp(-cum)).astype(BF)
                ke = (k2 * jnp.exp(tot - cum)).astype(BF)
                ge = jnp.exp(tot)
                for hh in range(2):
                    vh = v_ref[pl.ds(r0, CHUNK), hh * LANE:(hh + 1) * LANE]
                    st = st_ref[d, hh]
                    if write:
                        qdh = jnp.where(hmask[hh], qd, 0.0).astype(BF)
                        att = jnp.where(incl[d], _dot_nt(qdh, kd), 0.0).astype(BF)
                        o = _dot(att, vh) + _dot_nt(qdh, st.astype(BF))
                        oref = of_ref if d == 0 else ob_ref
                        oref[pl.ds(row0 + r0, CHUNK), hh * LANE:(hh + 1) * LANE] = o
                    st_ref[d, hh] = st * ge + _dot_tn(vh, ke)
            return carry

        lax.fori_loop(0, nch, body, 0)

    segment(qc_ref, kc_ref, vc_ref, 0, nc_ctx, need_ctx)
    segment(ql_ref, kl_ref, vl_ref, CT, nc_lat, True)

    nw = nw_ref[...]
    for hh in range(2):
        sl = slice(hh * LANE, (hh + 1) * LANE)
        o = of_ref[CT:CT + S, sl] + ob_ref[CT:CT + S, sl]
        ol_ref[:, sl] = _finish_rms(o, nw, gl_ref[:, sl]).astype(ol_ref.dtype)
        if need_ctx:
            o = of_ref[0:CT, sl] + ob_ref[0:CT, sl]
            oc_ref[:, sl] = _finish_rms(o, nw, gc_ref[:, sl]).astype(oc_ref.dtype)


def _gla(p_l, ps_l, p_c, ps_c, gate_up, gate_b, o_norm, need_ctx):
    B, S, _ = p_l.shape
    CT = p_c.shape[1]
    T = CT + S
    gup = jnp.zeros((2, N_SMALL, N_HEADS * GLA_DK), F32)
    for d in range(2):
        gup = gup.at[d, SM_RK + d * GLA_GATE_RANK:SM_RK + (d + 1) * GLA_GATE_RANK].set(gate_up[d])
    gup = gup.astype(BF)
    gb = gate_b.reshape(2, 1, N_HEADS * GLA_DK)

    def spec(n, width, off):
        return pl.BlockSpec((None, n, width), lambda b, p: (b, 0, off(p)))

    in_specs = []
    for n in (CT, S):
        in_specs += [spec(n, LANE, lambda p: GLA_Q + p), spec(n, LANE, lambda p: GLA_K + p),
                     spec(n, 2 * LANE, lambda p: GLA_V // 2 + p), spec(n, 2 * LANE, lambda p: GLA_G // 2 + p),
                     spec(n, N_SMALL, lambda p: 0)]
    in_specs += [pl.BlockSpec((2, N_SMALL, LANE), lambda b, p: (0, 0, p)),
                 pl.BlockSpec((2, 1, LANE), lambda b, p: (0, 0, p)),
                 pl.BlockSpec((1, LANE), lambda b, p: (0, 0))]
    out_specs = [pl.BlockSpec((None, S, 2 * LANE), lambda b, p: (b, 0, p))]
    out_shape = [jax.ShapeDtypeStruct((B, S, GROUP_W), BF)]
    if need_ctx:
        out_specs.append(pl.BlockSpec((None, CT, 2 * LANE), lambda b, p: (b, 0, p)))
        out_shape.append(jax.ShapeDtypeStruct((B, CT, GROUP_W), BF))
    res = pl.pallas_call(
        functools.partial(_gla_kernel, need_ctx=need_ctx, nc_ctx=CT // CHUNK, nc_lat=S // CHUNK),
        grid=(B, 2),
        in_specs=in_specs,
        out_specs=out_specs,
        out_shape=out_shape,
        scratch_shapes=[pltpu.VMEM((2, T, LANE), F32),
                        pltpu.VMEM((2, 2, LANE, LANE), F32),
                        pltpu.VMEM((T, 2 * LANE), F32),
                        pltpu.VMEM((T, 2 * LANE), F32)],
        compiler_params=_cp(("parallel", "parallel")),
        name="gla",
    )(p_c, p_c, p_c, p_c, ps_c, p_l, p_l, p_l, p_l, ps_l, gup, gb, o_norm.reshape(1, LANE))
    return (res[1] if need_ctx else None), res[0]


def _rope_tables(S):
    pos = np.arange(S)
    half = RET_DK // 2
    quarter = half // 2
    freqs = ROPE_BASE ** (-np.arange(quarter, dtype=np.float64) / quarter)
    cos = np.zeros((S, LANE), np.float64)
    sin_dn = np.zeros((S, LANE), np.float64)
    sin_up = np.zeros((S, LANE), np.float64)
    for head in range(2):
        for part, p in enumerate((pos // GRID_W, pos % GRID_W)):
            ang = p[:, None].astype(np.float64) * freqs[None, :]
            base = head * RET_DK + part * half
            cos[:, base:base + quarter] = np.cos(ang)
            cos[:, base + quarter:base + half] = np.cos(ang)
            sin_dn[:, base:base + quarter] = -np.sin(ang)
            sin_up[:, base + quarter:base + half] = np.sin(ang)
    return tuple(jnp.asarray(t, F32) for t in (cos, sin_dn, sin_up))


def _ret_kernel(qc_ref, kc_ref, vc_ref, gc_ref, ql_ref, kl_ref, vl_ref, gl_ref,
                cos_ref, sdn_ref, sup_ref, dl_ref, gnw_ref, *rest, need_ctx, nc_ctx, nc_lat):
    if need_ctx:
        ol_ref, oc_ref, qr_ref, kr_ref, st_ref, of_ref, ob_ref = rest
    else:
        ol_ref, qr_ref, kr_ref, st_ref, of_ref, ob_ref = rest
        oc_ref = None
    CT = nc_ctx * CHUNK
    S = nc_lat * CHUNK
    ri, ci, incl, _, _ = _tri_consts()
    hmask = _head_masks()
    pp = pl.program_id(1)

    def rope(x):
        quarter = RET_DK // 4
        return (x * cos_ref[...] + pltpu.roll(x, LANE - quarter, 1) * sdn_ref[...]
                + pltpu.roll(x, quarter, 1) * sup_ref[...])

    qr_ref[0:CT, :] = qc_ref[...].astype(F32)
    kr_ref[0:CT, :] = kc_ref[...].astype(F32) * (RET_DK ** -0.5)
    qr_ref[CT:CT + S, :] = rope(ql_ref[...].astype(F32))
    kr_ref[CT:CT + S, :] = rope(kl_ref[...].astype(F32) * (RET_DK ** -0.5))
    st_ref[...] = jnp.zeros_like(st_ref)

    dmat, qfac, kfac, gend = {}, {}, {}, {}
    rif = ri.astype(F32)
    cif = ci.astype(F32)
    col = lax.broadcasted_iota(jnp.int32, (CHUNK, 1), 0).astype(F32)
    for d in range(2):
        for hh in range(2):
            lgam = _log_sigmoid(dl_ref[pl.ds(d * N_HEADS + 2 * pp + hh, 1), 0:1])
            dist = (rif - cif) if d == 0 else (cif - rif)
            dmat[d, hh] = jnp.exp(jnp.where(incl[d], dist * lgam, -jnp.inf))
            steps = (col + 1.0) if d == 0 else (CHUNK - col)
            qfac[d, hh] = jnp.exp(steps * lgam)
            kfac[d, hh] = jnp.exp((CHUNK - steps) * lgam)
            gend[d, hh] = jnp.exp(CHUNK * lgam)

    def segment(v_ref, row0, nch, write):
        def body(n, carry):
            for d in range(2):
                c = n if d == 0 else nch - 1 - n
                r0 = pl.multiple_of(c * CHUNK, CHUNK)
                q2 = qr_ref[pl.ds(row0 + r0, CHUNK), :]
                k2 = kr_ref[pl.ds(row0 + r0, CHUNK), :]
                k2b = k2.astype(BF)
                for hh in range(2):
                    vh = v_ref[pl.ds(r0, CHUNK), hh * LANE:(hh + 1) * LANE]
                    st = st_ref[d, hh]
                    if write:
                        qh = jnp.where(hmask[hh], q2, 0.0)
                        att = (_dot_nt(qh.astype(BF), k2b) * dmat[d, hh]).astype(BF)
                        o = _dot(att, vh) + _dot_nt((qh * qfac[d, hh]).astype(BF), st.astype(BF))
                        oref = of_ref if d == 0 else ob_ref
                        oref[pl.ds(row0 + r0, CHUNK), hh * LANE:(hh + 1) * LANE] = o
                    st_ref[d, hh] = st * gend[d, hh] + _dot_tn(vh, (k2 * kfac[d, hh]).astype(BF))
            return carry

        lax.fori_loop(0, nch, body, 0)

    segment(vc_ref, 0, nc_ctx, need_ctx)
    segment(vl_ref, CT, nc_lat, True)

    def finish(o, w, g):
        mu = jnp.mean(o, axis=-1, keepdims=True)
        oc = o - mu
        var = jnp.mean(oc * oc, axis=-1, keepdims=True)
        return oc * lax.rsqrt(var + EPS) * w * _silu(g.astype(F32))

    for hh in range(2):
        sl = slice(hh * LANE, (hh + 1) * LANE)
        w = gnw_ref[:, sl]
        o = of_ref[CT:CT + S, sl] + ob_ref[CT:CT + S, sl]
        ol_ref[:, sl] = finish(o, w, gl_ref[:, sl]).astype(ol_ref.dtype)
        if need_ctx:
            o = of_ref[0:CT, sl] + ob_ref[0:CT, sl]
            oc_ref[:, sl] = finish(o, w, gc_ref[:, sl]).astype(oc_ref.dtype)


def _ret(p_l, p_c, decay_logit, gn_w, need_ctx):
    B, S, _ = p_l.shape
    CT = p_c.shape[1]
    T = CT + S
    cos, sdn, sup = _rope_tables(S)
    dl = jnp.broadcast_to(decay_logit.reshape(2 * N_HEADS, 1).astype(F32), (2 * N_HEADS, LANE))

    def spec(n, width, off):
        return pl.BlockSpec((None, n, width), lambda b, p: (b, 0, off(p)))

    in_specs = []
    for n in (CT, S):
        in_specs += [spec(n, LANE, lambda p: RET_Q + p), spec(n, LANE, lambda p: RET_K + p),
                     spec(n, 2 * LANE, lambda p: RET_V // 2 + p), spec(n, 2 * LANE, lambda p: RET_G // 2 + p)]
    tab = pl.BlockSpec((S, LANE), lambda b, p: (0, 0))
    in_specs += [tab, tab, tab,
                 pl.BlockSpec((2 * N_HEADS, LANE), lambda b, p: (0, 0)),
                 pl.BlockSpec((1, 2 * LANE), lambda b, p: (0, p))]
    out_specs = [pl.BlockSpec((None, S, 2 * LANE), lambda b, p: (b, 0, p))]
    out_shape = [jax.ShapeDtypeStruct((B, S, GROUP_W), BF)]
    if need_ctx:
        out_specs.append(pl.BlockSpec((None, CT, 2 * LANE), lambda b, p: (b, 0, p)))
        out_shape.append(jax.ShapeDtypeStruct((B, CT, GROUP_W), BF))
    res = pl.pallas_call(
        functools.partial(_ret_kernel, need_ctx=need_ctx, nc_ctx=CT // CHUNK, nc_lat=S // CHUNK),
        grid=(B, 2),
        in_specs=in_specs,
        out_specs=out_specs,
        out_shape=out_shape,
        scratch_shapes=[pltpu.VMEM((T, LANE), F32),
                        pltpu.VMEM((T, LANE), F32),
                        pltpu.VMEM((2, 2, LANE, LANE), F32),
                        pltpu.VMEM((T, 2 * LANE), F32),
                        pltpu.VMEM((T, 2 * LANE), F32)],
        compiler_params=_cp(("parallel", "parallel")),
        name="retention",
    )(p_c, p_c, p_c, p_c, p_l, p_l, p_l, p_l, cos, sdn, sup, dl, gn_w.reshape(1, GROUP_W))
    return (res[1] if need_ctx else None), res[0]


def _gdn_kernel(qc_ref, kc_ref, vc_ref, zc_ref, sc_ref, ql_ref, kl_ref, vl_ref, zl_ref, sl_ref,
                cwq_ref, cwk_ref, cwv_ref, alog_ref, dtb_ref, nw_ref, *rest, need_ctx, nc_ctx, nc_lat):
    if need_ctx:
        ol_ref, oc_ref = rest[:2]
        rest = rest[2:]
    else:
        ol_ref = rest[0]
        oc_ref = None
        rest = rest[1:]
    qs, ks, vs, u_ref, wq_ref, a_ref, ke_ref, ge_ref, st_ref, of_ref, ob_ref = rest
    CT = nc_ctx * CHUNK
    S = nc_lat * CHUNK
    ri, ci, incl, strict, tri_bf = _tri_consts()
    pp = pl.program_id(1)
    eye = ri == ci
    ones_bf = jnp.ones((CHUNK, CHUNK), BF)
    lane = lax.broadcasted_iota(jnp.int32, (CHUNK, LANE), 1)
    lvl_masks = []
    s = 1
    while s < CHUNK:
        lvl_masks.append(((ri // (2 * s)) == (ci // (2 * s))) & ((ri // s) != (ci // s)))
        s *= 2

    def conv_silu(x_ref, w_ref, n):
        x = x_ref[...].astype(F32)
        row = lax.broadcasted_iota(jnp.int32, x.shape, 0)
        xp = jnp.where(row == 0, 0.0, pltpu.roll(x, 1, 0))
        xn = jnp.where(row == n - 1, 0.0, pltpu.roll(x, n - 1, 0))
        return _silu(xp * w_ref[0:1, :] + x * w_ref[1:2, :] + xn * w_ref[2:3, :])

    def l2n(x):
        return x * lax.rsqrt(jnp.sum(x * x, axis=-1, keepdims=True) + EPS)

    for (q_ref, k_ref, v_ref, r0, n) in ((qc_ref, kc_ref, vc_ref, 0, CT), (ql_ref, kl_ref, vl_ref, CT, S)):
        q = conv_silu(q_ref, cwq_ref, n)
        k = conv_silu(k_ref, cwk_ref, n)
        v = conv_silu(v_ref, cwv_ref, n)
        for hh in range(2):
            sl = slice(hh * LANE, (hh + 1) * LANE)
            qs[r0:r0 + n, sl] = l2n(q[:, sl]) * (GDN_DK ** -0.5)
            ks[r0:r0 + n, sl] = l2n(k[:, sl])
        vs[r0:r0 + n, :] = v.astype(BF)

    neg_a = -jnp.exp(alog_ref[...])
    dtb = dtb_ref[...]

    def phase1(small_ref, row0, nch):
        def body(c, carry):
            r0 = pl.multiple_of(c * CHUNK, CHUNK)
            g0 = pl.multiple_of(row0 + r0, CHUNK)
            sm = small_ref[pl.ds(r0, CHUNK), :]
            lg_all = neg_a * _softplus(sm + dtb)
            lb_all = _log_sigmoid(sm)
            cum_dirs = tuple(_exact_dot(tri_bf[d], lg_all) for d in range(2))
            for hh in range(2):
                h = 2 * pp + hh
                sl = slice(hh * LANE, (hh + 1) * LANE)
                k = ks[pl.ds(g0, CHUNK), sl]
                q = qs[pl.ds(g0, CHUNK), sl]
                v = vs[pl.ds(g0, CHUNK), sl]
                kb = k.astype(BF)
                kk = _dot_nt(kb, kb)
                qk = _dot_nt(q.astype(BF), kb)
                for d in range(2):
                    cum_all = cum_dirs[d]
                    g = jnp.sum(jnp.where(lane == SM_A + d * N_HEADS + h, cum_all, 0.0), axis=-1, keepdims=True)
                    lb = jnp.sum(jnp.where(lane == SM_BT + d * N_HEADS + h, lb_all, 0.0), axis=-1, keepdims=True)
                    tot = g[CHUNK - 1:CHUNK, :] if d == 0 else g[0:1, :]
                    hcol = g - lb
                    hrow = _exact_dot(ones_bf, jnp.where(eye, hcol, 0.0))
                    e_in = jnp.exp(jnp.where(incl[d], g - hrow, -jnp.inf))
                    a = kk * jnp.where(strict[d], e_in, 0.0)
                    x = jnp.where(eye, 1.0, 0.0) - a * jnp.where(lvl_masks[0], 1.0, 0.0)
                    for m in lvl_masks[1:]:
                        t = (a * jnp.where(m, 1.0, 0.0)).astype(BF)
                        xb = x.astype(BF)
                        x = x - _dot(xb, _dot(t, xb).astype(BF))
                    gam = jnp.exp(g)
                    rhs = jnp.concatenate([v, (k * gam).astype(BF)], axis=1)
                    uw = _dot(x.astype(BF), rhs)
                    u_ref[d, hh, pl.ds(g0, CHUNK), :] = uw[:, 0:LANE]
                    wq_ref[d, hh, pl.ds(pl.multiple_of(2 * g0, 2 * CHUNK), CHUNK), :] = uw[:, LANE:2 * LANE].astype(BF)
                    wq_ref[d, hh, pl.ds(pl.multiple_of(2 * g0 + CHUNK, CHUNK), CHUNK), :] = (q * gam).astype(BF)
                    a_ref[d, hh, pl.ds(g0, CHUNK), :] = (qk * e_in).astype(BF)
                    ke_ref[d, hh, pl.ds(g0, CHUNK), :] = (k * jnp.exp(tot - g + lb)).astype(BF)
                    ge_ref[d, hh, pl.ds(pl.multiple_of((row0 // CHUNK + c) * 8, 8), 8), :] = jnp.broadcast_to(jnp.exp(tot), (8, LANE))
            return carry

        lax.fori_loop(0, nch, body, 0)

    phase1(sc_ref, 0, nc_ctx)
    phase1(sl_ref, CT, nc_lat)
    st_ref[...] = jnp.zeros_like(st_ref)

    def phase2(row0, nch, write):
        def body(n, carry):
            for d in range(2):
                c = n if d == 0 else nch - 1 - n
                g0 = pl.multiple_of(row0 + c * CHUNK, CHUNK)
                for hh in range(2):
                    st = st_ref[d, hh]
                    stb = st.astype(BF)
                    wq = wq_ref[d, hh, pl.ds(pl.multiple_of(2 * g0, 2 * CHUNK), 2 * CHUNK), :]
                    ws = _dot(wq, stb)
                    delta = u_ref[d, hh, pl.ds(g0, CHUNK), :] - ws[0:CHUNK, :]
                    db = delta.astype(BF)
                    if write:
                        o = ws[CHUNK:2 * CHUNK, :] + _dot(a_ref[d, hh, pl.ds(g0, CHUNK), :], db)
                        oref = of_ref if d == 0 else ob_ref
                        oref[pl.ds(g0, CHUNK), hh * LANE:(hh + 1) * LANE] = o
                    ge = ge_ref[d, hh, pl.ds(pl.multiple_of((row0 // CHUNK + c) * 8, 8), 1), :]
                    st_ref[d, hh] = st * ge[:, 0:1] + _dot_tn(ke_ref[d, hh, pl.ds(g0, CHUNK), :], db)
            return carry

        lax.fori_loop(0, nch, body, 0)

    phase2(0, nc_ctx, need_ctx)
    phase2(CT, nc_lat, True)

    nw = nw_ref[...]
    for hh in range(2):
        sl = slice(hh * LANE, (hh + 1) * LANE)
        o = of_ref[CT:CT + S, sl] + ob_ref[CT:CT + S, sl]
        ol_ref[:, sl] = _finish_rms(o, nw, zl_ref[:, sl]).astype(ol_ref.dtype)
        if need_ctx:
            o = of_ref[0:CT, sl] + ob_ref[0:CT, sl]
            oc_ref[:, sl] = _finish_rms(o, nw, zc_ref[:, sl]).astype(oc_ref.dtype)


def _gdn(p_l, ps_l, p_c, ps_c, conv_w, a_log, dt_bias, o_norm, need_ctx):
    B, S, _ = p_l.shape
    CT = p_c.shape[1]
    T = CT + S
    nch = T // CHUNK
    alog = jnp.zeros((1, N_SMALL), F32).at[0, SM_A:SM_A + 2 * N_HEADS].set(a_log.reshape(-1).astype(F32))
    dtb = jnp.zeros((1, N_SMALL), F32).at[0, SM_A:SM_A + 2 * N_HEADS].set(dt_bias.reshape(-1).astype(F32))

    def spec(n, width, off):
        return pl.BlockSpec((None, n, width), lambda b, p: (b, 0, off(p)))

    in_specs = []
    for n in (CT, S):
        in_specs += [spec(n, 2 * LANE, lambda p: GDN_Q // 2 + p), spec(n, 2 * LANE, lambda p: GDN_K // 2 + p),
                     spec(n, 2 * LANE, lambda p: GDN_V // 2 + p), spec(n, 2 * LANE, lambda p: GDN_Z // 2 + p),
                     spec(n, N_SMALL, lambda p: 0)]
    cw = lambda part: pl.BlockSpec((3, 2 * LANE), lambda b, p: (0, 2 * part + p))
    vec = pl.BlockSpec((1, LANE), lambda b, p: (0, 0))
    in_specs += [cw(0), cw(1), cw(2), vec, vec, vec]
    out_specs = [pl.BlockSpec((None, S, 2 * LANE), lambda b, p: (b, 0, p))]
    out_shape = [jax.ShapeDtypeStruct((B, S, GROUP_W), BF)]
    if need_ctx:
        out_specs.append(pl.BlockSpec((None, CT, 2 * LANE), lambda b, p: (b, 0, p)))
        out_shape.append(jax.ShapeDtypeStruct((B, CT, GROUP_W), BF))
    res = pl.pallas_call(
        functools.partial(_gdn_kernel, need_ctx=need_ctx, nc_ctx=CT // CHUNK, nc_lat=S // CHUNK),
        grid=(B, 2),
        in_specs=in_specs,
        out_specs=out_specs,
        out_shape=out_shape,
        scratch_shapes=[pltpu.VMEM((T, 2 * LANE), F32),
                        pltpu.VMEM((T, 2 * LANE), F32),
                        pltpu.VMEM((T, 2 * LANE), BF),
                        pltpu.VMEM((2, 2, T, LANE), F32),
                        pltpu.VMEM((2, 2, 2 * T, LANE), BF),
                        pltpu.VMEM((2, 2, T, CHUNK), BF),
                        pltpu.VMEM((2, 2, T, LANE), BF),
                        pltpu.VMEM((2, 2, nch * 8, LANE), F32),
                        pltpu.VMEM((2, 2, LANE, LANE), F32),
                        pltpu.VMEM((T, 2 * LANE), F32),
                        pltpu.VMEM((T, 2 * LANE), F32)],
        compiler_params=_cp(("parallel", "parallel")),
        name="gdn",
    )(p_c, p_c, p_c, p_c, ps_c, p_l, p_l, p_l, p_l, ps_l,
      conv_w, conv_w, conv_w, alog, dtb, o_norm.reshape(1, LANE))
    return (res[1] if need_ctx else None), res[0]


def _align_w_in(w):
    D = w.shape[0]
    big = jnp.concatenate([w[:, 0:3072], w[:, 3104:5152], w[:, 5168:6704]], axis=1)
    small = jnp.concatenate([w[:, 3072:3104], w[:, 5152:5168], jnp.zeros((D, N_SMALL - 48), w.dtype)], axis=1)
    return big.astype(BF), small.astype(BF)


def _pick(n, prefs):
    for p in prefs:
        if n % p == 0:
            return p
    return n


def kernel(x, c, ctx, c_ctx, ada_w, ada_b, norm1_w, norm2_w, w_in, w_out, na_q_norm, na_k_norm, na_rpb,
           gla_gate_up, gla_gate_b, gla_o_norm, gdn_conv_w, gdn_a_log, gdn_dt_bias, gdn_o_norm,
           ret_decay_logit, ret_gn_w, mlp_w1, mlp_w2):
    B, S, D = x.shape
    CT = ctx.shape[1]
    depth = ada_w.shape[0]
    R = ((B + 1 + 7) // 8) * 8
    cc = jnp.concatenate([c, c_ctx[None, :], jnp.zeros((R - B - 1, D), F32)], axis=0)
    mod_all = _ada(cc, ada_w, ada_b).reshape(depth, R, 6, D)

    tm_l = _pick(S, (1024, 512, 256))
    tm_c = _pick(B * CT, (1024, 512, 256))
    tn = _pick(N_BIG, (512, 256, 128))
    tm_o = _pick(S, (512, 256))
    tm_oc = _pick(B * CT, (512, 256))
    th = _pick(mlp_w1.shape[2], (512, 256))

    xl = x
    xc = ctx.reshape(1, B * CT, D)
    for layer in range(depth):
        need_ctx = layer < depth - 1
        mod = mod_all[layer]
        w_big, w_small = _align_w_in(w_in[layer])
        nw1 = norm1_w[layer].reshape(1, D)
        nw2 = norm2_w[layer].reshape(1, D)
        wo = w_out[layer].astype(BF)
        w1 = mlp_w1[layer].astype(BF)
        w2 = mlp_w2[layer].astype(BF)

        p_l, ps_l = _in_proj(xl, mod, None, nw1, w_big, w_small, tm_l, tn)
        p_c, ps_c = _in_proj(xc, mod, B, nw1, w_big, w_small, tm_c, tn)
        p_c = p_c.reshape(B, CT, N_BIG)
        ps_c = ps_c.reshape(B, CT, N_SMALL)

        na_c, na_l = _natten(p_l, p_c, na_q_norm[layer], na_k_norm[layer], na_rpb[layer], need_ctx)
        gl_c, gl_l = _gla(p_l, ps_l, p_c, ps_c, gla_gate_up[layer], gla_gate_b[layer], gla_o_norm[layer], need_ctx)
        gd_c, gd_l = _gdn(p_l, ps_l, p_c, ps_c, gdn_conv_w[layer], gdn_a_log[layer], gdn_dt_bias[layer],
                          gdn_o_norm[layer], need_ctx)
        rt_c, rt_l = _ret(p_l, p_c, ret_decay_logit[layer], ret_gn_w[layer], need_ctx)

        xl = _out_proj(xl, (na_l, gl_l, gd_l, rt_l), wo, mod, None, tm_o)
        xl = _mlp(xl, mod, None, nw2, w1, w2, tm_o, th)
        if need_ctx:
            ys = tuple(t.reshape(1, B * CT, GROUP_W) for t in (na_c, gl_c, gd_c, rt_c))
            xc = _out_proj(xc, ys, wo, mod, B, tm_oc)
            xc = _mlp(xc, mod, B, nw2, w1, w2, tm_oc, th)
    return xl
```

```python
import functools

import numpy as np
import jax
import jax.numpy as jnp
from jax import lax
from jax.experimental import pallas as pl
from jax.experimental.pallas import tpu as pltpu

BF = jnp.bfloat16
F32 = jnp.float32

N_HEADS = 4
HEAD_DIM = 128
GROUP_W = N_HEADS * HEAD_DIM
GRID_W = 64
NA_WIN_ROWS = 8
NA_WIN_COLS = 16
GLA_DK = 64
GLA_GATE_RANK = 16
GLA_GATE_TAU = 16.0
GDN_DK = 128
RET_DK = 64
ROPE_BASE = 10000.0
CHUNK = 64
GDN_BLK = 2 * CHUNK
EPS = 1e-6
NEG_INF = -1e30

LANE = 128
N_BIG = 52 * LANE
N_SMALL = LANE
NA_Q, NA_K, NA_V = 0, 4, 8
GLA_Q, GLA_K, GLA_V, GLA_G = 12, 14, 16, 20
GDN_Q, GDN_K, GDN_V, GDN_Z = 24, 28, 32, 36
RET_Q, RET_K, RET_V, RET_G = 40, 42, 44, 48
SM_RK, SM_A, SM_BT = 0, 32, 40

VMEM_LIMIT = 56 * 1024 * 1024


def _cp(sem, vmem=VMEM_LIMIT):
    return pltpu.CompilerParams(dimension_semantics=sem, vmem_limit_bytes=vmem)


def _dot(a, b):
    return jnp.dot(a, b, preferred_element_type=F32)


def _dot_nt(a, b):
    return lax.dot_general(a, b, (((1,), (1,)), ((), ())), preferred_element_type=F32)


def _dot_tn(a, b):
    return lax.dot_general(a, b, (((0,), (0,)), ((), ())), preferred_element_type=F32)


def _split3(x):
    hi = x.astype(BF)
    r1 = x - hi.astype(F32)
    mid = r1.astype(BF)
    lo = (r1 - mid.astype(F32)).astype(BF)
    return hi, mid, lo


def _exact_dot(m_bf, x):
    hi, mid, lo = _split3(x)
    n = x.shape[1]
    r = _dot(m_bf, jnp.concatenate([hi, mid, lo], axis=1))
    return r[:, 0:n] + r[:, n:2 * n] + r[:, 2 * n:3 * n]


def _sigmoid(x):
    return 1.0 / (1.0 + jnp.exp(-x))


def _silu(x):
    return x * _sigmoid(x)


def _log_sigmoid(x):
    return jnp.minimum(x, 0.0) - jnp.log(1.0 + jnp.exp(-jnp.abs(x)))


def _softplus(x):
    return jnp.maximum(x, 0.0) + jnp.log(1.0 + jnp.exp(-jnp.abs(x)))


def _ln_mod(x, nw, shift, scale):
    ms = jnp.mean(x * x, axis=-1, keepdims=True)
    return (x * lax.rsqrt(ms + EPS) * nw) * (1.0 + scale) + shift


def _tri_consts():
    ri = lax.broadcasted_iota(jnp.int32, (CHUNK, CHUNK), 0)
    ci = lax.broadcasted_iota(jnp.int32, (CHUNK, CHUNK), 1)
    incl = (ri >= ci, ri <= ci)
    strict = (ri > ci, ri < ci)
    tri_bf = tuple(jnp.where(m, 1.0, 0.0).astype(BF) for m in incl)
    return ri, ci, incl, strict, tri_bf


def _ada_kernel(c_ref, w_ref, b_ref, o_ref):
    sc = _silu(c_ref[...]).astype(BF)
    o_ref[...] = _dot(sc, w_ref[...].astype(BF)) + b_ref[...]


def _ada(cc, ada_w, ada_b):
    L, D, N6 = ada_w.shape
    R = cc.shape[0]
    tn = 1024 if N6 % 1024 == 0 else N6
    return pl.pallas_call(
        _ada_kernel,
        grid=(L, N6 // tn),
        in_specs=[pl.BlockSpec((R, D), lambda l, j: (0, 0)),
                  pl.BlockSpec((None, D, tn), lambda l, j: (l, 0, j)),
                  pl.BlockSpec((None, 1, tn), lambda l, j: (l, 0, j))],
        out_specs=pl.BlockSpec((None, R, tn), lambda l, j: (l, 0, j)),
        out_shape=jax.ShapeDtypeStruct((L, R, N6), F32),
        compiler_params=_cp(("parallel", "parallel")),
        name="ada_ln",
    )(cc, ada_w, ada_b.reshape(L, 1, N6))


def _inproj_kernel(x_ref, mod_ref, nw_ref, w_ref, ws_ref, o_ref, os_ref, h_ref):
    @pl.when(pl.program_id(1) == 0)
    def _():
        h = _ln_mod(x_ref[...], nw_ref[...], mod_ref[0:1, :], mod_ref[1:2, :]).astype(BF)
        h_ref[...] = h
        os_ref[...] = _dot(h, ws_ref[...])

    o_ref[...] = _dot(h_ref[...], w_ref[...]).astype(o_ref.dtype)


def _in_proj(x3, mod, const_row, nw, w_big, w_small, tm, tn):
    Bn, Tn, D = x3.shape
    nt = Tn // tm
    if const_row is None:
        mod_map = lambda i, j: (i // nt, 0, 0)
    else:
        mod_map = lambda i, j: (const_row, 0, 0)
    return pl.pallas_call(
        _inproj_kernel,
        grid=(Bn * nt, N_BIG // tn),
        in_specs=[pl.BlockSpec((None, tm, D), lambda i, j: (i // nt, i % nt, 0)),
                  pl.BlockSpec((None, 6, D), mod_map),
                  pl.BlockSpec((1, D), lambda i, j: (0, 0)),
                  pl.BlockSpec((D, tn), lambda i, j: (0, j)),
                  pl.BlockSpec((D, N_SMALL), lambda i, j: (0, 0))],
        out_specs=[pl.BlockSpec((None, tm, tn), lambda i, j: (i // nt, i % nt, j)),
                   pl.BlockSpec((None, tm, N_SMALL), lambda i, j: (i // nt, i % nt, 0))],
        out_shape=[jax.ShapeDtypeStruct((Bn, Tn, N_BIG), BF),
                   jax.ShapeDtypeStruct((Bn, Tn, N_SMALL), F32)],
        scratch_shapes=[pltpu.VMEM((tm, D), BF)],
        compiler_params=_cp(("parallel", "arbitrary")),
        name="in_proj",
    )(x3, mod, nw, w_big, w_small)


def _outproj_kernel(x_ref, y0, y1, y2, y3, w_ref, mod_ref, o_ref):
    acc = _dot(y0[...], w_ref[0 * GROUP_W:1 * GROUP_W, :])
    acc += _dot(y1[...], w_ref[1 * GROUP_W:2 * GROUP_W, :])
    acc += _dot(y2[...], w_ref[2 * GROUP_W:3 * GROUP_W, :])
    acc += _dot(y3[...], w_ref[3 * GROUP_W:4 * GROUP_W, :])
    o_ref[...] = x_ref[...] + mod_ref[2:3, :] * acc


def _out_proj(x3, ys, w_out, mod, const_row, tm):
    Bn, Tn, D = x3.shape
    nt = Tn // tm
    if const_row is None:
        mod_map = lambda i: (i // nt, 0, 0)
    else:
        mod_map = lambda i: (const_row, 0, 0)
    row_map = lambda i: (i // nt, i % nt, 0)
    return pl.pallas_call(
        _outproj_kernel,
        grid=(Bn * nt,),
        in_specs=[pl.BlockSpec((None, tm, D), row_map)]
                 + [pl.BlockSpec((None, tm, GROUP_W), row_map)] * 4
                 + [pl.BlockSpec((4 * GROUP_W, D), lambda i: (0, 0)),
                    pl.BlockSpec((None, 6, D), mod_map)],
        out_specs=pl.BlockSpec((None, tm, D), row_map),
        out_shape=jax.ShapeDtypeStruct((Bn, Tn, D), F32),
        compiler_params=_cp(("parallel",)),
        name="out_proj",
    )(x3, *ys, w_out, mod)


def _mlp_kernel(x_ref, mod_ref, nw_ref, w1_ref, w2_ref, o_ref, h_ref, *, nk):
    k = pl.program_id(1)

    @pl.when(k == 0)
    def _():
        h_ref[...] = _ln_mod(x_ref[...], nw_ref[...], mod_ref[3:4, :], mod_ref[4:5, :]).astype(BF)
        o_ref[...] = jnp.zeros_like(o_ref)

    hid = jnp.maximum(_dot(h_ref[...], w1_ref[...]), 0.0)
    o_ref[...] += _dot((hid * hid).astype(BF), w2_ref[...])

    @pl.when(k == nk - 1)
    def _():
        o_ref[...] = x_ref[...] + mod_ref[5:6, :] * o_ref[...]


def _mlp(x3, mod, const_row, nw, w1, w2, tm, th):
    Bn, Tn, D = x3.shape
    Hd = w1.shape[1]
    nt = Tn // tm
    nk = Hd // th
    if const_row is None:
        mod_map = lambda i, k: (i // nt, 0, 0)
    else:
        mod_map = lambda i, k: (const_row, 0, 0)
    row_map = lambda i, k: (i // nt, i % nt, 0)
    return pl.pallas_call(
        functools.partial(_mlp_kernel, nk=nk),
        grid=(Bn * nt, nk),
        in_specs=[pl.BlockSpec((None, tm, D), row_map),
                  pl.BlockSpec((None, 6, D), mod_map),
                  pl.BlockSpec((1, D), lambda i, k: (0, 0)),
                  pl.BlockSpec((D, th), lambda i, k: (0, k)),
                  pl.BlockSpec((th, D), lambda i, k: (k, 0))],
        out_specs=pl.BlockSpec((None, tm, D), row_map),
        out_shape=jax.ShapeDtypeStruct((Bn, Tn, D), F32),
        scratch_shapes=[pltpu.VMEM((tm, D), BF)],
        compiler_params=_cp(("parallel", "arbitrary")),
        name="mlp",
    )(x3, mod, nw, w1, w2)


def _rms_head(x, w):
    x = x.astype(F32)
    return x * lax.rsqrt(jnp.mean(x * x, axis=-1, keepdims=True) + EPS) * w


def _natten_kernel(ql_ref, kl_ref, vl_ref, qc_ref, kc_ref, vc_ref, qw_ref, kw_ref, bias_ref, *rest,
                   need_ctx, rows, kh):
    if need_ctx:
        ol_ref, oc_ref, qs, ks = rest
    else:
        ol_ref, qs, ks = rest
    scale = HEAD_DIM ** -0.5
    qw = qw_ref[...]
    kw = kw_ref[...]
    qs[...] = (_rms_head(ql_ref[...], qw) * scale).astype(BF)
    ks[...] = _rms_head(kl_ref[...], kw).astype(BF)
    kc = _rms_head(kc_ref[...], kw).astype(BF)
    vc = vc_ref[...]
    if need_ctx:
        qc = (_rms_head(qc_ref[...], qw) * scale).astype(BF)
        s = _dot_nt(qc, kc)
        p = jnp.exp(s - jnp.max(s, axis=-1, keepdims=True))
        l = jnp.sum(p, axis=-1, keepdims=True)
        oc_ref[...] = (_dot(p.astype(BF), vc) / l).astype(oc_ref.dtype)

    per_iter = 2 if rows % 2 == 0 else 1

    def body(it, carry):
        geo = []
        for j in range(per_iter):
            r = it * per_iter + j
            rs = jnp.clip(r - kh // 2, 0, rows - kh)
            geo.append((r - rs, pl.multiple_of(r * GRID_W, GRID_W), pl.multiple_of(rs * GRID_W, GRID_W)))
        qv = [qs[pl.ds(q0, GRID_W), :] for (d, q0, k0) in geo]
        sws = [_dot_nt(q, ks[pl.ds(k0, kh * GRID_W), :]) + bias_ref[d] for q, (d, q0, k0) in zip(qv, geo)]
        scs = [_dot_nt(q, kc) for q in qv]
        probs = []
        for sw, sc in zip(sws, scs):
            m = jnp.maximum(jnp.max(sw, axis=-1, keepdims=True), jnp.max(sc, axis=-1, keepdims=True))
            pw = jnp.exp(sw - m)
            pc = jnp.exp(sc - m)
            l = jnp.sum(pw, axis=-1, keepdims=True) + jnp.sum(pc, axis=-1, keepdims=True)
            probs.append((pw.astype(BF), pc.astype(BF), l))
        for (pw, pc, l), (d, q0, k0) in zip(probs, geo):
            o = (_dot(pw, vl_ref[pl.ds(k0, kh * GRID_W), :]) + _dot(pc, vc)) / l
            ol_ref[pl.ds(q0, GRID_W), :] = o.astype(ol_ref.dtype)
        return carry

    lax.fori_loop(0, rows // per_iter, body, 0)


def _natten_bias(rpb, kh):
    q = np.arange(GRID_W)[:, None]
    kc = np.arange(GRID_W)[None, :]
    col_off = np.clip(kc - q, -(NA_WIN_COLS - 1), NA_WIN_COLS - 1) + NA_WIN_COLS - 1
    onehot = (col_off[..., None] == np.arange(2 * NA_WIN_COLS - 1)).astype(np.float32)
    toe = jnp.einsum('hrc,qkc->hrqk', rpb.astype(F32), jnp.asarray(onehot), precision=lax.Precision.HIGHEST)
    cs = np.clip(q - NA_WIN_COLS // 2, 0, GRID_W - NA_WIN_COLS)
    valid = (kc >= cs) & (kc < cs + NA_WIN_COLS)
    toe = jnp.where(valid[None, None], toe, NEG_INF)
    per_d = [toe[:, NA_WIN_ROWS - 1 - d:NA_WIN_ROWS - 1 - d + kh] for d in range(kh)]
    bias = jnp.stack(per_d, axis=1).transpose(0, 1, 3, 2, 4)
    return bias.reshape(rpb.shape[0], kh, GRID_W, kh * GRID_W)


def _natten(p_l, p_c, qw, kw, rpb, need_ctx):
    B, S, _ = p_l.shape
    CT = p_c.shape[1]
    rows = S // GRID_W
    kh = min(NA_WIN_ROWS, rows)
    bias = _natten_bias(rpb, kh)
    lat = lambda off: pl.BlockSpec((None, S, LANE), lambda b, h: (b, 0, off + h))
    ctx = lambda off: pl.BlockSpec((None, CT, LANE), lambda b, h: (b, 0, off + h))
    vec = pl.BlockSpec((1, LANE), lambda b, h: (0, 0))
    out_specs = [pl.BlockSpec((None, S, LANE), lambda b, h: (b, 0, h))]
    out_shape = [jax.ShapeDtypeStruct((B, S, GROUP_W), BF)]
    if need_ctx:
        out_specs.append(pl.BlockSpec((None, CT, LANE), lambda b, h: (b, 0, h)))
        out_shape.append(jax.ShapeDtypeStruct((B, CT, GROUP_W), BF))
    res = pl.pallas_call(
        functools.partial(_natten_kernel, need_ctx=need_ctx, rows=rows, kh=kh),
        grid=(B, N_HEADS),
        in_specs=[lat(NA_Q), lat(NA_K), lat(NA_V), ctx(NA_Q), ctx(NA_K), ctx(NA_V), vec, vec,
                  pl.BlockSpec((None, kh, GRID_W, kh * GRID_W), lambda b, h: (h, 0, 0, 0))],
        out_specs=out_specs,
        out_shape=out_shape,
        scratch_shapes=[pltpu.VMEM((S, LANE), BF), pltpu.VMEM((S, LANE), BF)],
        compiler_params=_cp(("parallel", "parallel")),
        name="natten",
    )(p_l, p_l, p_l, p_c, p_c, p_c, qw.reshape(1, LANE), kw.reshape(1, LANE), bias)
    return (res[1] if need_ctx else None), res[0]


def _head_masks():
    lane = lax.broadcasted_iota(jnp.int32, (CHUNK, LANE), 1)
    return (lane < 64, lane >= 64)


def _finish_rms(o, nw, g):
    y = o * lax.rsqrt(jnp.mean(o * o, axis=-1, keepdims=True) + EPS) * nw
    return y * _silu(g.astype(F32))


def _gla_kernel(qc_ref, kc_ref, vc_ref, gc_ref, sc_ref, ql_ref, kl_ref, vl_ref, gl_ref, sl_ref,
                gup_ref, gb_ref, nw_ref, *rest, need_ctx, nc_ctx, nc_lat):
    if need_ctx:
        ol_ref, oc_ref, lg_ref, st_ref, of_ref, ob_ref = rest
    else:
        ol_ref, lg_ref, st_ref, of_ref, ob_ref = rest
        oc_ref = None
    CT = nc_ctx * CHUNK
    S = nc_lat * CHUNK
    _, _, incl, _, tri_bf = _tri_consts()
    hmask = _head_masks()

    for d in range(2):
        for (s_ref, r0, n) in ((sc_ref, 0, CT), (sl_ref, CT, S)):
            z = _dot(s_ref[...].astype(BF), gup_ref[d]) + gb_ref[d]
            lg_ref[d, r0:r0 + n, :] = _log_sigmoid(z) * (1.0 / GLA_GATE_TAU)
    st_ref[...] = jnp.zeros_like(st_ref)

    def segment(q_ref, k_ref, v_ref, row0, nch, write):
        per_iter = 2 if nch % 2 == 0 else 1

        def body(it, carry):
            pre = []
            for j in range(per_iter):
                n = it * per_iter + j
                for d in range(2):
                    c = n if d == 0 else nch - 1 - n
                    r0 = pl.multiple_of(c * CHUNK, CHUNK)
                    pre.append((d, r0, q_ref[pl.ds(r0, CHUNK), :].astype(F32), k_ref[pl.ds(r0, CHUNK), :].astype(F32)))
            cums = [_exact_dot(tri_bf[d], lg_ref[d, pl.ds(row0 + r0, CHUNK), :]) for (d, r0, _, _) in pre]
            work = []
            for (d, r0, q2, k2), cum in zip(pre, cums):
                tot = cum[CHUNK - 1:CHUNK, :] if d == 0 else cum[0:1, :]
                qd = q2 * (jnp.exp(cum) * (GLA_DK ** -0.5))
                kd = (k2 * jnp.exp(-cum)).astype(BF)
                ke = (k2 * jnp.exp(tot - cum)).astype(BF)
                ge = jnp.exp(tot)
                for hh in range(2):
                    vh = v_ref[pl.ds(r0, CHUNK), hh * LANE:(hh + 1) * LANE]
                    qdh = jnp.where(hmask[hh], qd, 0.0).astype(BF) if write else None
                    work.append((d, hh, r0, vh, qdh, kd, ke, ge))
            if write:
                atts = [jnp.where(incl[d], _dot_nt(qdh, kd), 0.0).astype(BF) for (d, hh, r0, vh, qdh, kd, ke, ge) in work]
            upds = [_dot_tn(vh, ke) for (d, hh, r0, vh, qdh, kd, ke, ge) in work]
            state = {(d, hh): st_ref[d, hh] for d in range(2) for hh in range(2)}
            for i, (d, hh, r0, vh, qdh, kd, ke, ge) in enumerate(work):
                st = state[d, hh]
                if write:
                    oref = of_ref if d == 0 else ob_ref
                    oref[pl.ds(row0 + r0, CHUNK), hh * LANE:(hh + 1) * LANE] = (
                        _dot(atts[i], vh) + _dot_nt(qdh, st.astype(BF)))
                state[d, hh] = st * ge + upds[i]
            for (d, hh), st in state.items():
                st_ref[d, hh] = st
            return carry

        lax.fori_loop(0, nch // per_iter, body, 0)

    segment(qc_ref, kc_ref, vc_ref, 0, nc_ctx, need_ctx)
    segment(ql_ref, kl_ref, vl_ref, CT, nc_lat, True)

    nw = nw_ref[...]
    for hh in range(2):
        sl = slice(hh * LANE, (hh + 1) * LANE)
        o = of_ref[CT:CT + S, sl] + ob_ref[CT:CT + S, sl]
        ol_ref[:, sl] = _finish_rms(o, nw, gl_ref[:, sl]).astype(ol_ref.dtype)
        if need_ctx:
            o = of_ref[0:CT, sl] + ob_ref[0:CT, sl]
            oc_ref[:, sl] = _finish_rms(o, nw, gc_ref[:, sl]).astype(oc_ref.dtype)


def _gla(p_l, ps_l, p_c, ps_c, gate_up, gate_b, o_norm, need_ctx):
    B, S, _ = p_l.shape
    CT = p_c.shape[1]
    T = CT + S
    gup = jnp.zeros((2, N_SMALL, N_HEADS * GLA_DK), F32)
    for d in range(2):
        gup = gup.at[d, SM_RK + d * GLA_GATE_RANK:SM_RK + (d + 1) * GLA_GATE_RANK].set(gate_up[d])
    gup = gup.astype(BF)
    gb = gate_b.reshape(2, 1, N_HEADS * GLA_DK)

    def spec(n, width, off):
        return pl.BlockSpec((None, n, width), lambda b, p: (b, 0, off(p)))

    in_specs = []
    for n in (CT, S):
        in_specs += [spec(n, LANE, lambda p: GLA_Q + p), spec(n, LANE, lambda p: GLA_K + p),
                     spec(n, 2 * LANE, lambda p: GLA_V // 2 + p), spec(n, 2 * LANE, lambda p: GLA_G // 2 + p),
                     spec(n, N_SMALL, lambda p: 0)]
    in_specs += [pl.BlockSpec((2, N_SMALL, LANE), lambda b, p: (0, 0, p)),
                 pl.BlockSpec((2, 1, LANE), lambda b, p: (0, 0, p)),
                 pl.BlockSpec((1, LANE), lambda b, p: (0, 0))]
    out_specs = [pl.BlockSpec((None, S, 2 * LANE), lambda b, p: (b, 0, p))]
    out_shape = [jax.ShapeDtypeStruct((B, S, GROUP_W), BF)]
    if need_ctx:
        out_specs.append(pl.BlockSpec((None, CT, 2 * LANE), lambda b, p: (b, 0, p)))
        out_shape.append(jax.ShapeDtypeStruct((B, CT, GROUP_W), BF))
    res = pl.pallas_call(
        functools.partial(_gla_kernel, need_ctx=need_ctx, nc_ctx=CT // CHUNK, nc_lat=S // CHUNK),
        grid=(B, 2),
        in_specs=in_specs,
        out_specs=out_specs,
        out_shape=out_shape,
        scratch_shapes=[pltpu.VMEM((2, T, LANE), F32),
                        pltpu.VMEM((2, 2, LANE, LANE), F32),
                        pltpu.VMEM((T, 2 * LANE), F32),
                        pltpu.VMEM((T, 2 * LANE), F32)],
        compiler_params=_cp(("parallel", "parallel")),
        name="gla",
    )(p_c, p_c, p_c, p_c, ps_c, p_l, p_l, p_l, p_l, ps_l, gup, gb, o_norm.reshape(1, LANE))
    return (res[1] if need_ctx else None), res[0]


def _rope_tables(S):
    pos = np.arange(S)
    half = RET_DK // 2
    quarter = half // 2
    freqs = ROPE_BASE ** (-np.arange(quarter, dtype=np.float64) / quarter)
    cos = np.zeros((S, LANE), np.float64)
    sin_dn = np.zeros((S, LANE), np.float64)
    sin_up = np.zeros((S, LANE), np.float64)
    for head in range(2):
        for part, p in enumerate((pos // GRID_W, pos % GRID_W)):
            ang = p[:, None].astype(np.float64) * freqs[None, :]
            base = head * RET_DK + part * half
            cos[:, base:base + quarter] = np.cos(ang)
            cos[:, base + quarter:base + half] = np.cos(ang)
            sin_dn[:, base:base + quarter] = -np.sin(ang)
            sin_up[:, base + quarter:base + half] = np.sin(ang)
    return tuple(jnp.asarray(t, F32) for t in (cos, sin_dn, sin_up))


def _ret_kernel(qc_ref, kc_ref, vc_ref, gc_ref, ql_ref, kl_ref, vl_ref, gl_ref,
                cos_ref, sdn_ref, sup_ref, dl_ref, gnw_ref, *rest, need_ctx, nc_ctx, nc_lat):
    if need_ctx:
        ol_ref, oc_ref, qr_ref, kr_ref, st_ref, of_ref, ob_ref = rest
    else:
        ol_ref, qr_ref, kr_ref, st_ref, of_ref, ob_ref = rest
        oc_ref = None
    CT = nc_ctx * CHUNK
    S = nc_lat * CHUNK
    ri, ci, incl, _, _ = _tri_consts()
    hmask = _head_masks()
    pp = pl.program_id(1)

    def rope(x):
        quarter = RET_DK // 4
        return (x * cos_ref[...] + pltpu.roll(x, LANE - quarter, 1) * sdn_ref[...]
                + pltpu.roll(x, quarter, 1) * sup_ref[...])

    qr_ref[0:CT, :] = qc_ref[...].astype(F32)
    kr_ref[0:CT, :] = kc_ref[...].astype(F32) * (RET_DK ** -0.5)
    qr_ref[CT:CT + S, :] = rope(ql_ref[...].astype(F32))
    kr_ref[CT:CT + S, :] = rope(kl_ref[...].astype(F32) * (RET_DK ** -0.5))
    st_ref[...] = jnp.zeros_like(st_ref)

    dmat, qfac, kfac, gend = {}, {}, {}, {}
    rif = ri.astype(F32)
    cif = ci.astype(F32)
    col = lax.broadcasted_iota(jnp.int32, (CHUNK, 1), 0).astype(F32)
    for d in range(2):
        for hh in range(2):
            lgam = _log_sigmoid(dl_ref[pl.ds(d * N_HEADS + 2 * pp + hh, 1), 0:1])
            dist = (rif - cif) if d == 0 else (cif - rif)
            dmat[d, hh] = jnp.exp(jnp.where(incl[d], dist * lgam, -jnp.inf))
            steps = (col + 1.0) if d == 0 else (CHUNK - col)
            qfac[d, hh] = jnp.exp(steps * lgam)
            kfac[d, hh] = jnp.exp((CHUNK - steps) * lgam)
            gend[d, hh] = jnp.exp(CHUNK * lgam)

    def segment(v_ref, row0, nch, write):
        per_iter = 2 if nch % 2 == 0 else 1

        def body(it, carry):
            work = []
            for j in range(per_iter):
                n = it * per_iter + j
                for d in range(2):
                    c = n if d == 0 else nch - 1 - n
                    r0 = pl.multiple_of(c * CHUNK, CHUNK)
                    q2 = qr_ref[pl.ds(row0 + r0, CHUNK), :]
                    k2 = kr_ref[pl.ds(row0 + r0, CHUNK), :]
                    k2b = k2.astype(BF)
                    for hh in range(2):
                        vh = v_ref[pl.ds(r0, CHUNK), hh * LANE:(hh + 1) * LANE]
                        qh = jnp.where(hmask[hh], q2, 0.0).astype(BF) if write else None
                        work.append((d, hh, r0, vh, qh, k2, k2b))
            if write:
                atts = [(_dot_nt(qh, k2b) * dmat[d, hh]).astype(BF) for (d, hh, r0, vh, qh, k2, k2b) in work]
            upds = [_dot_tn(vh, (k2 * kfac[d, hh]).astype(BF)) for (d, hh, r0, vh, qh, k2, k2b) in work]
            state = {(d, hh): st_ref[d, hh] for d in range(2) for hh in range(2)}
            for i, (d, hh, r0, vh, qh, k2, k2b) in enumerate(work):
                st = state[d, hh]
                if write:
                    oref = of_ref if d == 0 else ob_ref
                    oref[pl.ds(row0 + r0, CHUNK), hh * LANE:(hh + 1) * LANE] = (
                        _dot(atts[i], vh) + _dot_nt(qh, st.astype(BF)) * qfac[d, hh])
                state[d, hh] = st * gend[d, hh] + upds[i]
            for (d, hh), st in state.items():
                st_ref[d, hh] = st
            return carry

        lax.fori_loop(0, nch // per_iter, body, 0)

    segment(vc_ref, 0, nc_ctx, need_ctx)
    segment(vl_ref, CT, nc_lat, True)

    def finish(o, w, g):
        mu = jnp.mean(o, axis=-1, keepdims=True)
        oc = o - mu
        var = jnp.mean(oc * oc, axis=-1, keepdims=True)
        return oc * lax.rsqrt(var + EPS) * w * _silu(g.astype(F32))

    for hh in range(2):
        sl = slice(hh * LANE, (hh + 1) * LANE)
        w = gnw_ref[:, sl]
        o = of_ref[CT:CT + S, sl] + ob_ref[CT:CT + S, sl]
        ol_ref[:, sl] = finish(o, w, gl_ref[:, sl]).astype(ol_ref.dtype)
        if need_ctx:
            o = of_ref[0:CT, sl] + ob_ref[0:CT, sl]
            oc_ref[:, sl] = finish(o, w, gc_ref[:, sl]).astype(oc_ref.dtype)


def _ret(p_l, p_c, decay_logit, gn_w, need_ctx):
    B, S, _ = p_l.shape
    CT = p_c.shape[1]
    T = CT + S
    cos, sdn, sup = _rope_tables(S)
    dl = jnp.broadcast_to(decay_logit.reshape(2 * N_HEADS, 1).astype(F32), (2 * N_HEADS, LANE))

    def spec(n, width, off):
        return pl.BlockSpec((None, n, width), lambda b, p: (b, 0, off(p)))

    in_specs = []
    for n in (CT, S):
        in_specs += [spec(n, LANE, lambda p: RET_Q + p), spec(n, LANE, lambda p: RET_K + p),
                     spec(n, 2 * LANE, lambda p: RET_V // 2 + p), spec(n, 2 * LANE, lambda p: RET_G // 2 + p)]
    tab = pl.BlockSpec((S, LANE), lambda b, p: (0, 0))
    in_specs += [tab, tab, tab,
                 pl.BlockSpec((2 * N_HEADS, LANE), lambda b, p: (0, 0)),
                 pl.BlockSpec((1, 2 * LANE), lambda b, p: (0, p))]
    out_specs = [pl.BlockSpec((None, S, 2 * LANE), lambda b, p: (b, 0, p))]
    out_shape = [jax.ShapeDtypeStruct((B, S, GROUP_W), BF)]
    if need_ctx:
        out_specs.append(pl.BlockSpec((None, CT, 2 * LANE), lambda b, p: (b, 0, p)))
        out_shape.append(jax.ShapeDtypeStruct((B, CT, GROUP_W), BF))
    res = pl.pallas_call(
        functools.partial(_ret_kernel, need_ctx=need_ctx, nc_ctx=CT // CHUNK, nc_lat=S // CHUNK),
        grid=(B, 2),
        in_specs=in_specs,
        out_specs=out_specs,
        out_shape=out_shape,
        scratch_shapes=[pltpu.VMEM((T, LANE), F32),
                        pltpu.VMEM((T, LANE), F32),
                        pltpu.VMEM((2, 2, LANE, LANE), F32),
                        pltpu.VMEM((T, 2 * LANE), F32),
                        pltpu.VMEM((T, 2 * LANE), F32)],
        compiler_params=_cp(("parallel", "parallel")),
        name="retention",
    )(p_c, p_c, p_c, p_c, p_l, p_l, p_l, p_l, cos, sdn, sup, dl, gn_w.reshape(1, GROUP_W))
    return (res[1] if need_ctx else None), res[0]


def _gdn_kernel(qc_ref, kc_ref, vc_ref, zc_ref, sc_ref, ql_ref, kl_ref, vl_ref, zl_ref, sl_ref,
                cwq_ref, cwk_ref, cwv_ref, alog_ref, dtb_ref, nw_ref, *rest, need_ctx, nc_ctx, nc_lat):
    if need_ctx:
        ol_ref, oc_ref = rest[:2]
        rest = rest[2:]
    else:
        ol_ref = rest[0]
        oc_ref = None
        rest = rest[1:]
    qs, ks, vs, u_ref, wq_ref, a_ref, ke_ref, ge_ref, st_ref, of_ref, ob_ref = rest
    CT = nc_ctx * CHUNK
    S = nc_lat * CHUNK
    pp = pl.program_id(1)
    ri = lax.broadcasted_iota(jnp.int32, (GDN_BLK, GDN_BLK), 0)
    ci = lax.broadcasted_iota(jnp.int32, (GDN_BLK, GDN_BLK), 1)
    same = (ri // CHUNK) == (ci // CHUNK)
    incl = (same & (ri >= ci), same & (ri <= ci))
    strict = (same & (ri > ci), same & (ri < ci))
    tri2_bf = jnp.concatenate([jnp.where(m, 1.0, 0.0).astype(BF) for m in incl], axis=0)
    eye_f = jnp.where(ri == ci, 1.0, 0.0)
    lane = lax.broadcasted_iota(jnp.int32, (GDN_BLK, LANE), 1)
    rowi = lax.broadcasted_iota(jnp.int32, (GDN_BLK, 1), 0)
    lvl_masks = []
    s = 1
    while s < CHUNK:
        lvl_masks.append(jnp.where(((ri // (2 * s)) == (ci // (2 * s))) & ((ri // s) != (ci // s)), 1.0, 0.0))
        s *= 2

    def conv_silu(x_ref, w_ref, n):
        x = x_ref[...].astype(F32)
        row = lax.broadcasted_iota(jnp.int32, x.shape, 0)
        xp = jnp.where(row == 0, 0.0, pltpu.roll(x, 1, 0))
        xn = jnp.where(row == n - 1, 0.0, pltpu.roll(x, n - 1, 0))
        return _silu(xp * w_ref[0:1, :] + x * w_ref[1:2, :] + xn * w_ref[2:3, :])

    def l2n(x):
        return x * lax.rsqrt(jnp.sum(x * x, axis=-1, keepdims=True) + EPS)

    for (q_ref, k_ref, v_ref, r0, n) in ((qc_ref, kc_ref, vc_ref, 0, CT), (ql_ref, kl_ref, vl_ref, CT, S)):
        q = conv_silu(q_ref, cwq_ref, n)
        k = conv_silu(k_ref, cwk_ref, n)
        v = conv_silu(v_ref, cwv_ref, n)
        for hh in range(2):
            sl = slice(hh * LANE, (hh + 1) * LANE)
            qs[r0:r0 + n, sl] = l2n(q[:, sl]) * (GDN_DK ** -0.5)
            ks[r0:r0 + n, sl] = l2n(k[:, sl])
        vs[r0:r0 + n, :] = v.astype(BF)

    neg_a = -jnp.exp(alog_ref[...])
    dtb = dtb_ref[...]

    def phase1(small_ref, row0, nblk):
        per_iter = 2 if nblk % 2 == 0 else 1

        def body(it, carry):
            pipes = []
            a_list = []
            for j in range(per_iter):
                m = it * per_iter + j
                r0 = pl.multiple_of(m * GDN_BLK, GDN_BLK)
                g0 = pl.multiple_of(row0 + r0, GDN_BLK)
                sm = small_ref[pl.ds(r0, GDN_BLK), :]
                lg_all = neg_a * _softplus(sm + dtb)
                lb_all = _log_sigmoid(sm)
                cum2 = _exact_dot(tri2_bf, lg_all)
                for hh in range(2):
                    h = 2 * pp + hh
                    sl = slice(hh * LANE, (hh + 1) * LANE)
                    k = ks[pl.ds(g0, GDN_BLK), sl]
                    q = qs[pl.ds(g0, GDN_BLK), sl]
                    v = vs[pl.ds(g0, GDN_BLK), sl]
                    kb = k.astype(BF)
                    kkqk = _dot_nt(jnp.concatenate([kb, q.astype(BF)], axis=0), kb)
                    kk = kkqk[0:GDN_BLK, :]
                    qk = kkqk[GDN_BLK:2 * GDN_BLK, :]
                    for d in range(2):
                        cum_all = cum2[d * GDN_BLK:(d + 1) * GDN_BLK, :]
                        g = jnp.sum(jnp.where(lane == SM_A + d * N_HEADS + h, cum_all, 0.0), axis=-1, keepdims=True)
                        lb = jnp.sum(jnp.where(lane == SM_BT + d * N_HEADS + h, lb_all, 0.0), axis=-1, keepdims=True)
                        ends = (CHUNK - 1, GDN_BLK - 1) if d == 0 else (0, CHUNK)
                        tot_lo = g[ends[0]:ends[0] + 1, :]
                        tot_hi = g[ends[1]:ends[1] + 1, :]
                        tot = jnp.where(rowi < CHUNK, tot_lo, tot_hi)
                        hrow = jnp.broadcast_to(g - lb, (GDN_BLK, GDN_BLK)).T
                        e_in = jnp.exp(jnp.where(incl[d], g - hrow, -jnp.inf))
                        a_list.append(kk * jnp.where(strict[d], e_in, 0.0))
                        pipes.append((m, g0, hh, d, k, q, v, qk, e_in, g, lb, tot, tot_lo, tot_hi))
            a4 = jnp.stack(a_list, axis=0)
            x4 = eye_f[None] - a4 * lvl_masks[0][None]
            bdot = lambda p, r: jnp.einsum('pij,pjk->pik', p, r, preferred_element_type=F32)
            for msk in lvl_masks[1:]:
                t4 = (a4 * msk[None]).astype(BF)
                xb = x4.astype(BF)
                x4 = x4 - bdot(xb, bdot(t4, xb).astype(BF))
            x4b = x4.astype(BF)
            for p, (m, g0, hh, d, k, q, v, qk, e_in, g, lb, tot, tot_lo, tot_hi) in enumerate(pipes):
                gam = jnp.exp(g)
                rhs = jnp.concatenate([v, (k * gam).astype(BF)], axis=1)
                uw = _dot(x4b[p], rhs)
                u_ref[d, hh, pl.ds(g0, GDN_BLK), :] = uw[:, 0:LANE]
                w = uw[:, LANE:2 * LANE].astype(BF)
                qg = (q * gam).astype(BF)
                wq0 = pl.multiple_of(2 * g0, 2 * GDN_BLK)
                wq_ref[d, hh, pl.ds(wq0, 2 * GDN_BLK), :] = jnp.concatenate(
                    [w[0:CHUNK], qg[0:CHUNK], w[CHUNK:GDN_BLK], qg[CHUNK:GDN_BLK]], axis=0)
                a_ref[d, hh, pl.ds(g0, GDN_BLK), :] = (qk * e_in).astype(BF)
                ke_ref[d, hh, pl.ds(g0, GDN_BLK), :] = (k * jnp.exp(tot - g + lb)).astype(BF)
                ge0 = pl.multiple_of((row0 // CHUNK + 2 * m) * 8, 16)
                ge_ref[d, hh, pl.ds(ge0, 16), :] = jnp.concatenate(
                    [jnp.broadcast_to(jnp.exp(tot_lo), (8, LANE)), jnp.broadcast_to(jnp.exp(tot_hi), (8, LANE))], axis=0)
            return carry

        lax.fori_loop(0, nblk // per_iter, body, 0)

    phase1(sc_ref, 0, CT // GDN_BLK)
    phase1(sl_ref, CT, S // GDN_BLK)
    st_ref[...] = jnp.zeros_like(st_ref)

    def phase2(row0, nch, write):
        def body(n, carry):
            chains = []
            for d in range(2):
                c = n if d == 0 else nch - 1 - n
                g0 = pl.multiple_of(row0 + c * CHUNK, CHUNK)
                for hh in range(2):
                    chains.append((d, hh, c, g0))
            sts = [st_ref[d, hh] for (d, hh, c, g0) in chains]
            wss = [_dot(wq_ref[d, hh, pl.ds(pl.multiple_of(2 * g0, 2 * CHUNK), 2 * CHUNK), :], st.astype(BF))
                   for (d, hh, c, g0), st in zip(chains, sts)]
            dbs = [(u_ref[d, hh, pl.ds(g0, CHUNK), :] - ws[0:CHUNK, :]).astype(BF)
                   for (d, hh, c, g0), ws in zip(chains, wss)]
            upd = [_dot_tn(ke_ref[d, hh, pl.ds(g0, CHUNK), :], db) for (d, hh, c, g0), db in zip(chains, dbs)]
            for (d, hh, c, g0), st, up in zip(chains, sts, upd):
                ge = ge_ref[d, hh, pl.ds(pl.multiple_of((row0 // CHUNK + c) * 8, 8), 1), :]
                st_ref[d, hh] = st * ge[:, 0:1] + up
            if write:
                for (d, hh, c, g0), ws, db in zip(chains, wss, dbs):
                    o = ws[CHUNK:2 * CHUNK, :] + _dot(a_ref[d, hh, pl.ds(g0, CHUNK), :],
                                                      jnp.concatenate([db, db], axis=0))
                    oref = of_ref if d == 0 else ob_ref
                    oref[pl.ds(g0, CHUNK), hh * LANE:(hh + 1) * LANE] = o
            return carry

        lax.fori_loop(0, nch, body, 0)

    phase2(0, nc_ctx, need_ctx)
    phase2(CT, nc_lat, True)

    nw = nw_ref[...]
    for hh in range(2):
        sl = slice(hh * LANE, (hh + 1) * LANE)
        o = of_ref[CT:CT + S, sl] + ob_ref[CT:CT + S, sl]
        ol_ref[:, sl] = _finish_rms(o, nw, zl_ref[:, sl]).astype(ol_ref.dtype)
        if need_ctx:
            o = of_ref[0:CT, sl] + ob_ref[0:CT, sl]
            oc_ref[:, sl] = _finish_rms(o, nw, zc_ref[:, sl]).astype(oc_ref.dtype)


def _gdn(p_l, ps_l, p_c, ps_c, conv_w, a_log, dt_bias, o_norm, need_ctx):
    B, S, _ = p_l.shape
    CT = p_c.shape[1]
    T = CT + S
    nch = T // CHUNK
    alog = jnp.zeros((1, N_SMALL), F32).at[0, SM_A:SM_A + 2 * N_HEADS].set(a_log.reshape(-1).astype(F32))
    dtb = jnp.zeros((1, N_SMALL), F32).at[0, SM_A:SM_A + 2 * N_HEADS].set(dt_bias.reshape(-1).astype(F32))

    def spec(n, width, off):
        return pl.BlockSpec((None, n, width), lambda b, p: (b, 0, off(p)))

    in_specs = []
    for n in (CT, S):
        in_specs += [spec(n, 2 * LANE, lambda p: GDN_Q // 2 + p), spec(n, 2 * LANE, lambda p: GDN_K // 2 + p),
                     spec(n, 2 * LANE, lambda p: GDN_V // 2 + p), spec(n, 2 * LANE, lambda p: GDN_Z // 2 + p),
                     spec(n, N_SMALL, lambda p: 0)]
    cw = lambda part: pl.BlockSpec((3, 2 * LANE), lambda b, p: (0, 2 * part + p))
    vec = pl.BlockSpec((1, LANE), lambda b, p: (0, 0))
    in_specs += [cw(0), cw(1), cw(2), vec, vec, vec]
    out_specs = [pl.BlockSpec((None, S, 2 * LANE), lambda b, p: (b, 0, p))]
    out_shape = [jax.ShapeDtypeStruct((B, S, GROUP_W), BF)]
    if need_ctx:
        out_specs.append(pl.BlockSpec((None, CT, 2 * LANE), lambda b, p: (b, 0, p)))
        out_shape.append(jax.ShapeDtypeStruct((B, CT, GROUP_W), BF))
    res = pl.pallas_call(
        functools.partial(_gdn_kernel, need_ctx=need_ctx, nc_ctx=CT // CHUNK, nc_lat=S // CHUNK),
        grid=(B, 2),
        in_specs=in_specs,
        out_specs=out_specs,
        out_shape=out_shape,
        scratch_shapes=[pltpu.VMEM((T, 2 * LANE), F32),
                        pltpu.VMEM((T, 2 * LANE), F32),
                        pltpu.VMEM((T, 2 * LANE), BF),
                        pltpu.VMEM((2, 2, T, LANE), F32),
                        pltpu.VMEM((2, 2, 2 * T, LANE), BF),
                        pltpu.VMEM((2, 2, T, GDN_BLK), BF),
                        pltpu.VMEM((2, 2, T, LANE), BF),
                        pltpu.VMEM((2, 2, nch * 8, LANE), F32),
                        pltpu.VMEM((2, 2, LANE, LANE), F32),
                        pltpu.VMEM((T, 2 * LANE), F32),
                        pltpu.VMEM((T, 2 * LANE), F32)],
        compiler_params=_cp(("parallel", "parallel")),
        name="gdn",
    )(p_c, p_c, p_c, p_c, ps_c, p_l, p_l, p_l, p_l, ps_l,
      conv_w, conv_w, conv_w, alog, dtb, o_norm.reshape(1, LANE))
    return (res[1] if need_ctx else None), res[0]


def _align_w_in(w):
    D = w.shape[0]
    big = jnp.concatenate([w[:, 0:3072], w[:, 3104:5152], w[:, 5168:6704]], axis=1)
    small = jnp.concatenate([w[:, 3072:3104], w[:, 5152:5168], jnp.zeros((D, N_SMALL - 48), w.dtype)], axis=1)
    return big.astype(BF), small.astype(BF)


def _pick(n, prefs):
    for p in prefs:
        if n % p == 0:
            return p
    return n


def kernel(x, c, ctx, c_ctx, ada_w, ada_b, norm1_w, norm2_w, w_in, w_out, na_q_norm, na_k_norm, na_rpb,
           gla_gate_up, gla_gate_b, gla_o_norm, gdn_conv_w, gdn_a_log, gdn_dt_bias, gdn_o_norm,
           ret_decay_logit, ret_gn_w, mlp_w1, mlp_w2):
    B, S, D = x.shape
    CT = ctx.shape[1]
    depth = ada_w.shape[0]
    R = ((B + 1 + 7) // 8) * 8
    cc = jnp.concatenate([c, c_ctx[None, :], jnp.zeros((R - B - 1, D), F32)], axis=0)
    mod_all = _ada(cc, ada_w, ada_b).reshape(depth, R, 6, D)

    tm_l = _pick(S, (1024, 512, 256))
    tm_c = _pick(B * CT, (1024, 512, 256))
    tn = _pick(N_BIG, (512, 256, 128))
    tm_o = _pick(S, (512, 256))
    tm_oc = _pick(B * CT, (512, 256))
    th = _pick(mlp_w1.shape[2], (512, 256))

    xl = x
    xc = ctx.reshape(1, B * CT, D)
    for layer in range(depth):
        need_ctx = layer < depth - 1
        mod = mod_all[layer]
        w_big, w_small = _align_w_in(w_in[layer])
        nw1 = norm1_w[layer].reshape(1, D)
        nw2 = norm2_w[layer].reshape(1, D)
        wo = w_out[layer].astype(BF)
        w1 = mlp_w1[layer].astype(BF)
        w2 = mlp_w2[layer].astype(BF)

        p_l, ps_l = _in_proj(xl, mod, None, nw1, w_big, w_small, tm_l, tn)
        p_c, ps_c = _in_proj(xc, mod, B, nw1, w_big, w_small, tm_c, tn)
        p_c = p_c.reshape(B, CT, N_BIG)
        ps_c = ps_c.reshape(B, CT, N_SMALL)

        na_c, na_l = _natten(p_l, p_c, na_q_norm[layer], na_k_norm[layer], na_rpb[layer], need_ctx)
        gl_c, gl_l = _gla(p_l, ps_l, p_c, ps_c, gla_gate_up[layer], gla_gate_b[layer], gla_o_norm[layer], need_ctx)
        gd_c, gd_l = _gdn(p_l, ps_l, p_c, ps_c, gdn_conv_w[layer], gdn_a_log[layer], gdn_dt_bias[layer],
                          gdn_o_norm[layer], need_ctx)
        rt_c, rt_l = _ret(p_l, p_c, ret_decay_logit[layer], ret_gn_w[layer], need_ctx)

        xl = _out_proj(xl, (na_l, gl_l, gd_l, rt_l), wo, mod, None, tm_o)
        xl = _mlp(xl, mod, None, nw2, w1, w2, tm_o, th)
        if need_ctx:
            ys = tuple(t.reshape(1, B * CT, GROUP_W) for t in (na_c, gl_c, gd_c, rt_c))
            xc = _out_proj(xc, ys, wo, mod, B, tm_oc)
            xc = _mlp(xc, mod, B, nw2, w1, w2, tm_oc, th)
    return xl
```

```python
import functools

import numpy as np
import jax
import jax.numpy as jnp
from jax import lax
from jax.experimental import pallas as pl
from jax.experimental.pallas import tpu as pltpu

BF = jnp.bfloat16
F32 = jnp.float32

N_HEADS = 4
HEAD_DIM = 128
GROUP_W = N_HEADS * HEAD_DIM
GRID_W = 64
NA_WIN_ROWS = 8
NA_WIN_COLS = 16
NA_GROUP = 4
GLA_DK = 64
GLA_GATE_RANK = 16
GLA_GATE_TAU = 16.0
GDN_DK = 128
RET_DK = 64
ROPE_BASE = 10000.0
CHUNK = 64
GDN_BLK = 2 * CHUNK
EPS = 1e-6
NEG_INF = -1e30

LANE = 128
N_BIG = 52 * LANE
N_SMALL = LANE
NA_Q, NA_K, NA_V = 0, 4, 8
GLA_Q, GLA_K, GLA_V, GLA_G = 12, 14, 16, 20
GDN_Q, GDN_K, GDN_V, GDN_Z = 24, 28, 32, 36
RET_Q, RET_K, RET_V, RET_G = 40, 42, 44, 48
SM_RK, SM_A, SM_BT = 0, 32, 40

VMEM_LIMIT = 56 * 1024 * 1024


def _cp(sem, vmem=VMEM_LIMIT):
    return pltpu.CompilerParams(dimension_semantics=sem, vmem_limit_bytes=vmem)


def _dot(a, b):
    return jnp.dot(a, b, preferred_element_type=F32)


def _dot_nt(a, b):
    return lax.dot_general(a, b, (((1,), (1,)), ((), ())), preferred_element_type=F32)


def _dot_tn(a, b):
    return lax.dot_general(a, b, (((0,), (0,)), ((), ())), preferred_element_type=F32)


def _split3(x):
    hi = x.astype(BF)
    r1 = x - hi.astype(F32)
    mid = r1.astype(BF)
    lo = (r1 - mid.astype(F32)).astype(BF)
    return hi, mid, lo


def _exact_dot(m_bf, x):
    hi, mid, lo = _split3(x)
    n = x.shape[1]
    r = _dot(m_bf, jnp.concatenate([hi, mid, lo], axis=1))
    return r[:, 0:n] + r[:, n:2 * n] + r[:, 2 * n:3 * n]


def _sigmoid(x):
    return 1.0 / (1.0 + jnp.exp(-x))


def _silu(x):
    return x * _sigmoid(x)


def _log_sigmoid(x):
    return jnp.minimum(x, 0.0) - jnp.log(1.0 + jnp.exp(-jnp.abs(x)))


def _softplus(x):
    return jnp.maximum(x, 0.0) + jnp.log(1.0 + jnp.exp(-jnp.abs(x)))


def _ln_mod(x, nw, shift, scale):
    ms = jnp.mean(x * x, axis=-1, keepdims=True)
    return (x * lax.rsqrt(ms + EPS) * nw) * (1.0 + scale) + shift


def _tri_consts():
    ri = lax.broadcasted_iota(jnp.int32, (CHUNK, CHUNK), 0)
    ci = lax.broadcasted_iota(jnp.int32, (CHUNK, CHUNK), 1)
    incl = (ri >= ci, ri <= ci)
    strict = (ri > ci, ri < ci)
    tri_bf = tuple(jnp.where(m, 1.0, 0.0).astype(BF) for m in incl)
    return ri, ci, incl, strict, tri_bf


def _ada_kernel(c_ref, w_ref, b_ref, o_ref):
    sc = _silu(c_ref[...]).astype(BF)
    o_ref[...] = _dot(sc, w_ref[...].astype(BF)) + b_ref[...]


def _ada(cc, ada_w, ada_b):
    L, D, N6 = ada_w.shape
    R = cc.shape[0]
    tn = 1024 if N6 % 1024 == 0 else N6
    return pl.pallas_call(
        _ada_kernel,
        grid=(L, N6 // tn),
        in_specs=[pl.BlockSpec((R, D), lambda l, j: (0, 0)),
                  pl.BlockSpec((None, D, tn), lambda l, j: (l, 0, j)),
                  pl.BlockSpec((None, 1, tn), lambda l, j: (l, 0, j))],
        out_specs=pl.BlockSpec((None, R, tn), lambda l, j: (l, 0, j)),
        out_shape=jax.ShapeDtypeStruct((L, R, N6), F32),
        compiler_params=_cp(("parallel", "parallel")),
        name="ada_ln",
    )(cc, ada_w, ada_b.reshape(L, 1, N6))


def _inproj_kernel(x_ref, mod_ref, nw_ref, w_ref, ws_ref, o_ref, os_ref, h_ref):
    @pl.when(pl.program_id(1) == 0)
    def _():
        h = _ln_mod(x_ref[...], nw_ref[...], mod_ref[0:1, :], mod_ref[1:2, :]).astype(BF)
        h_ref[...] = h
        os_ref[...] = _dot(h, ws_ref[...])

    o_ref[...] = _dot(h_ref[...], w_ref[...]).astype(o_ref.dtype)


def _in_proj(x3, mod, const_row, nw, w_big, w_small, tm, tn):
    Bn, Tn, D = x3.shape
    nt = Tn // tm
    if const_row is None:
        mod_map = lambda i, j: (i // nt, 0, 0)
    else:
        mod_map = lambda i, j: (const_row, 0, 0)
    return pl.pallas_call(
        _inproj_kernel,
        grid=(Bn * nt, N_BIG // tn),
        in_specs=[pl.BlockSpec((None, tm, D), lambda i, j: (i // nt, i % nt, 0)),
                  pl.BlockSpec((None, 6, D), mod_map),
                  pl.BlockSpec((1, D), lambda i, j: (0, 0)),
                  pl.BlockSpec((D, tn), lambda i, j: (0, j)),
                  pl.BlockSpec((D, N_SMALL), lambda i, j: (0, 0))],
        out_specs=[pl.BlockSpec((None, tm, tn), lambda i, j: (i // nt, i % nt, j)),
                   pl.BlockSpec((None, tm, N_SMALL), lambda i, j: (i // nt, i % nt, 0))],
        out_shape=[jax.ShapeDtypeStruct((Bn, Tn, N_BIG), BF),
                   jax.ShapeDtypeStruct((Bn, Tn, N_SMALL), F32)],
        scratch_shapes=[pltpu.VMEM((tm, D), BF)],
        compiler_params=_cp(("parallel", "arbitrary")),
        name="in_proj",
    )(x3, mod, nw, w_big, w_small)


def _outproj_kernel(x_ref, y0, y1, y2, y3, w_ref, mod_ref, o_ref):
    acc = _dot(y0[...], w_ref[0 * GROUP_W:1 * GROUP_W, :])
    acc += _dot(y1[...], w_ref[1 * GROUP_W:2 * GROUP_W, :])
    acc += _dot(y2[...], w_ref[2 * GROUP_W:3 * GROUP_W, :])
    acc += _dot(y3[...], w_ref[3 * GROUP_W:4 * GROUP_W, :])
    o_ref[...] = x_ref[...] + mod_ref[2:3, :] * acc


def _out_proj(x3, ys, w_out, mod, const_row, tm):
    Bn, Tn, D = x3.shape
    nt = Tn // tm
    if const_row is None:
        mod_map = lambda i: (i // nt, 0, 0)
    else:
        mod_map = lambda i: (const_row, 0, 0)
    row_map = lambda i: (i // nt, i % nt, 0)
    return pl.pallas_call(
        _outproj_kernel,
        grid=(Bn * nt,),
        in_specs=[pl.BlockSpec((None, tm, D), row_map)]
                 + [pl.BlockSpec((None, tm, GROUP_W), row_map)] * 4
                 + [pl.BlockSpec((4 * GROUP_W, D), lambda i: (0, 0)),
                    pl.BlockSpec((None, 6, D), mod_map)],
        out_specs=pl.BlockSpec((None, tm, D), row_map),
        out_shape=jax.ShapeDtypeStruct((Bn, Tn, D), F32),
        compiler_params=_cp(("parallel",)),
        name="out_proj",
    )(x3, *ys, w_out, mod)


def _mlp_kernel(x_ref, mod_ref, nw_ref, w1_ref, w2_ref, o_ref, h_ref, *, nk):
    k = pl.program_id(1)

    @pl.when(k == 0)
    def _():
        h_ref[...] = _ln_mod(x_ref[...], nw_ref[...], mod_ref[3:4, :], mod_ref[4:5, :]).astype(BF)
        o_ref[...] = jnp.zeros_like(o_ref)

    hid = jnp.maximum(_dot(h_ref[...], w1_ref[...]), 0.0)
    o_ref[...] += _dot((hid * hid).astype(BF), w2_ref[...])

    @pl.when(k == nk - 1)
    def _():
        o_ref[...] = x_ref[...] + mod_ref[5:6, :] * o_ref[...]


def _mlp(x3, mod, const_row, nw, w1, w2, tm, th):
    Bn, Tn, D = x3.shape
    Hd = w1.shape[1]
    nt = Tn // tm
    nk = Hd // th
    if const_row is None:
        mod_map = lambda i, k: (i // nt, 0, 0)
    else:
        mod_map = lambda i, k: (const_row, 0, 0)
    row_map = lambda i, k: (i // nt, i % nt, 0)
    return pl.pallas_call(
        functools.partial(_mlp_kernel, nk=nk),
        grid=(Bn * nt, nk),
        in_specs=[pl.BlockSpec((None, tm, D), row_map),
                  pl.BlockSpec((None, 6, D), mod_map),
                  pl.BlockSpec((1, D), lambda i, k: (0, 0)),
                  pl.BlockSpec((D, th), lambda i, k: (0, k)),
                  pl.BlockSpec((th, D), lambda i, k: (k, 0))],
        out_specs=pl.BlockSpec((None, tm, D), row_map),
        out_shape=jax.ShapeDtypeStruct((Bn, Tn, D), F32),
        scratch_shapes=[pltpu.VMEM((tm, D), BF)],
        compiler_params=_cp(("parallel", "arbitrary")),
        name="mlp",
    )(x3, mod, nw, w1, w2)


def _rms_head(x, w):
    x = x.astype(F32)
    return x * lax.rsqrt(jnp.mean(x * x, axis=-1, keepdims=True) + EPS) * w


def _natten_kernel(geo_ref, ql_ref, kl_ref, vl_ref, qc_ref, kc_ref, vc_ref, qw_ref, kw_ref, bias_ref, *rest,
                   need_ctx, rows, grp, span):
    if need_ctx:
        ol_ref, oc_ref, qs, ks = rest
    else:
        ol_ref, qs, ks = rest
    scale = HEAD_DIM ** -0.5
    qw = qw_ref[...]
    kw = kw_ref[...]
    qs[...] = (_rms_head(ql_ref[...], qw) * scale).astype(BF)
    ks[...] = _rms_head(kl_ref[...], kw).astype(BF)
    kc = _rms_head(kc_ref[...], kw).astype(BF)
    vc = vc_ref[...]
    if need_ctx:
        qc = (_rms_head(qc_ref[...], qw) * scale).astype(BF)
        s = _dot_nt(qc, kc)
        p = jnp.exp(s - jnp.max(s, axis=-1, keepdims=True))
        l = jnp.sum(p, axis=-1, keepdims=True)
        oc_ref[...] = (_dot(p.astype(BF), vc) / l).astype(oc_ref.dtype)

    nq = grp * GRID_W
    nk = span * GRID_W

    ngroups = rows // grp
    per_iter = 2 if ngroups % 2 == 0 else 1

    def body(it, carry):
        geo = []
        for j in range(per_iter):
            g = it * per_iter + j
            geo.append((pl.multiple_of(g * nq, nq), pl.multiple_of(geo_ref[0, g] * GRID_W, GRID_W), geo_ref[1, g]))
        qv = [qs[pl.ds(q0, nq), :] for (q0, k0, t) in geo]
        sws = [_dot_nt(q, ks[pl.ds(k0, nk), :]) + bias_ref[t] for q, (q0, k0, t) in zip(qv, geo)]
        scs = [_dot_nt(q, kc) for q in qv]
        probs = []
        for sw, sc in zip(sws, scs):
            m = jnp.maximum(jnp.max(sw, axis=-1, keepdims=True), jnp.max(sc, axis=-1, keepdims=True))
            pw = jnp.exp(sw - m)
            pc = jnp.exp(sc - m)
            l = jnp.sum(pw, axis=-1, keepdims=True) + jnp.sum(pc, axis=-1, keepdims=True)
            probs.append((pw.astype(BF), pc.astype(BF), l))
        for (pw, pc, l), (q0, k0, t) in zip(probs, geo):
            o = (_dot(pw, vl_ref[pl.ds(k0, nk), :]) + _dot(pc, vc)) / l
            ol_ref[pl.ds(q0, nq), :] = o.astype(ol_ref.dtype)
        return carry

    lax.fori_loop(0, ngroups // per_iter, body, 0)


def _natten_geometry(rows, kh):
    grp = NA_GROUP if rows % NA_GROUP == 0 else 1
    span = min(kh + grp - 1, rows)
    starts, type_ids, types = [], [], []
    for g in range(rows // grp):
        rs = [int(np.clip(g * grp + j - kh // 2, 0, rows - kh)) for j in range(grp)]
        us = int(np.clip(rs[0], 0, rows - span))
        sig = tuple((rs[j] - us, g * grp + j - rs[j]) for j in range(grp))
        assert all(0 <= off <= span - kh for off, _ in sig)
        if sig not in types:
            types.append(sig)
        starts.append(us)
        type_ids.append(types.index(sig))
    return grp, span, np.asarray([starts, type_ids], np.int32), types


def _natten_bias(rpb, kh, span, types):
    q = np.arange(GRID_W)[:, None]
    kc = np.arange(GRID_W)[None, :]
    col_off = np.clip(kc - q, -(NA_WIN_COLS - 1), NA_WIN_COLS - 1) + NA_WIN_COLS - 1
    onehot = (col_off[..., None] == np.arange(2 * NA_WIN_COLS - 1)).astype(np.float32)
    toe = jnp.einsum('hrc,qkc->hrqk', rpb.astype(F32), jnp.asarray(onehot), precision=lax.Precision.HIGHEST)
    cs = np.clip(q - NA_WIN_COLS // 2, 0, GRID_W - NA_WIN_COLS)
    valid = (kc >= cs) & (kc < cs + NA_WIN_COLS)
    toe = jnp.where(valid[None, None], toe, NEG_INF)
    H = rpb.shape[0]
    masked = jnp.full((H, GRID_W, GRID_W), NEG_INF, F32)
    tables = []
    for sig in types:
        rows_ = []
        for off, d in sig:
            blocks = [toe[:, NA_WIN_ROWS - 1 - d + i - off] if 0 <= i - off < kh else masked for i in range(span)]
            rows_.append(jnp.stack(blocks, axis=2).reshape(H, GRID_W, span * GRID_W))
        tables.append(jnp.concatenate(rows_, axis=1))
    return jnp.stack(tables, axis=1)


def _natten(p_l, p_c, qw, kw, rpb, need_ctx):
    B, S, _ = p_l.shape
    CT = p_c.shape[1]
    rows = S // GRID_W
    kh = min(NA_WIN_ROWS, rows)
    grp, span, geo, types = _natten_geometry(rows, kh)
    bias = _natten_bias(rpb, kh, span, types)
    lat = lambda off: pl.BlockSpec((None, S, LANE), lambda b, h: (b, 0, off + h))
    ctx = lambda off: pl.BlockSpec((None, CT, LANE), lambda b, h: (b, 0, off + h))
    vec = pl.BlockSpec((1, LANE), lambda b, h: (0, 0))
    out_specs = [pl.BlockSpec((None, S, LANE), lambda b, h: (b, 0, h))]
    out_shape = [jax.ShapeDtypeStruct((B, S, GROUP_W), BF)]
    if need_ctx:
        out_specs.append(pl.BlockSpec((None, CT, LANE), lambda b, h: (b, 0, h)))
        out_shape.append(jax.ShapeDtypeStruct((B, CT, GROUP_W), BF))
    res = pl.pallas_call(
        functools.partial(_natten_kernel, need_ctx=need_ctx, rows=rows, grp=grp, span=span),
        grid=(B, N_HEADS),
        in_specs=[pl.BlockSpec(memory_space=pltpu.SMEM),
                  lat(NA_Q), lat(NA_K), lat(NA_V), ctx(NA_Q), ctx(NA_K), ctx(NA_V), vec, vec,
                  pl.BlockSpec((None, len(types), grp * GRID_W, span * GRID_W), lambda b, h: (h, 0, 0, 0))],
        out_specs=out_specs,
        out_shape=out_shape,
        scratch_shapes=[pltpu.VMEM((S, LANE), BF), pltpu.VMEM((S, LANE), BF)],
        compiler_params=_cp(("parallel", "parallel")),
        name="natten",
    )(jnp.asarray(geo), p_l, p_l, p_l, p_c, p_c, p_c, qw.reshape(1, LANE), kw.reshape(1, LANE), bias)
    return (res[1] if need_ctx else None), res[0]


def _head_masks():
    lane = lax.broadcasted_iota(jnp.int32, (CHUNK, LANE), 1)
    return (lane < 64, lane >= 64)


def _finish_rms(o, nw, g):
    y = o * lax.rsqrt(jnp.mean(o * o, axis=-1, keepdims=True) + EPS) * nw
    return y * _silu(g.astype(F32))


def _gla_kernel(qc_ref, kc_ref, vc_ref, gc_ref, sc_ref, ql_ref, kl_ref, vl_ref, gl_ref, sl_ref,
                gup_ref, gb_ref, nw_ref, *rest, need_ctx, nc_ctx, nc_lat):
    if need_ctx:
        ol_ref, oc_ref, lg_ref, st_ref, of_ref, ob_ref = rest
    else:
        ol_ref, lg_ref, st_ref, of_ref, ob_ref = rest
        oc_ref = None
    CT = nc_ctx * CHUNK
    S = nc_lat * CHUNK
    _, _, incl, _, tri_bf = _tri_consts()
    hmask = _head_masks()

    for d in range(2):
        for (s_ref, r0, n) in ((sc_ref, 0, CT), (sl_ref, CT, S)):
            z = _dot(s_ref[...].astype(BF), gup_ref[d]) + gb_ref[d]
            lg_ref[d, r0:r0 + n, :] = _log_sigmoid(z) * (1.0 / GLA_GATE_TAU)
    st_ref[...] = jnp.zeros_like(st_ref)

    def segment(q_ref, k_ref, v_ref, row0, nch, write):
        per_iter = 2 if nch % 2 == 0 else 1

        def body(it, carry):
            pre = []
            for j in range(per_iter):
                n = it * per_iter + j
                for d in range(2):
                    c = n if d == 0 else nch - 1 - n
                    r0 = pl.multiple_of(c * CHUNK, CHUNK)
                    pre.append((d, r0, q_ref[pl.ds(r0, CHUNK), :].astype(F32), k_ref[pl.ds(r0, CHUNK), :].astype(F32)))
            cums = [_exact_dot(tri_bf[d], lg_ref[d, pl.ds(row0 + r0, CHUNK), :]) for (d, r0, _, _) in pre]
            work = []
            for (d, r0, q2, k2), cum in zip(pre, cums):
                tot = cum[CHUNK - 1:CHUNK, :] if d == 0 else cum[0:1, :]
                qd = q2 * (jnp.exp(cum) * (GLA_DK ** -0.5))
                kd = (k2 * jnp.exp(-cum)).astype(BF)
                ke = (k2 * jnp.exp(tot - cum)).astype(BF)
                ge = jnp.exp(tot)
                for hh in range(2):
                    vh = v_ref[pl.ds(r0, CHUNK), hh * LANE:(hh + 1) * LANE]
                    qdh = jnp.where(hmask[hh], qd, 0.0).astype(BF) if write else None
                    work.append((d, hh, r0, vh, qdh, kd, ke, ge))
            if write:
                atts = [jnp.where(incl[d], _dot_nt(qdh, kd), 0.0).astype(BF) for (d, hh, r0, vh, qdh, kd, ke, ge) in work]
            upds = [_dot_tn(vh, ke) for (d, hh, r0, vh, qdh, kd, ke, ge) in work]
            state = {(d, hh): st_ref[d, hh] for d in range(2) for hh in range(2)}
            for i, (d, hh, r0, vh, qdh, kd, ke, ge) in enumerate(work):
                st = state[d, hh]
                if write:
                    oref = of_ref if d == 0 else ob_ref
                    oref[pl.ds(row0 + r0, CHUNK), hh * LANE:(hh + 1) * LANE] = (
                        _dot(atts[i], vh) + _dot_nt(qdh, st.astype(BF)))
                state[d, hh] = st * ge + upds[i]
            for (d, hh), st in state.items():
                st_ref[d, hh] = st
            return carry

        lax.fori_loop(0, nch // per_iter, body, 0)

    segment(qc_ref, kc_ref, vc_ref, 0, nc_ctx, need_ctx)
    segment(ql_ref, kl_ref, vl_ref, CT, nc_lat, True)

    nw = nw_ref[...]
    for hh in range(2):
        sl = slice(hh * LANE, (hh + 1) * LANE)
        o = of_ref[CT:CT + S, sl] + ob_ref[CT:CT + S, sl]
        ol_ref[:, sl] = _finish_rms(o, nw, gl_ref[:, sl]).astype(ol_ref.dtype)
        if need_ctx:
            o = of_ref[0:CT, sl] + ob_ref[0:CT, sl]
            oc_ref[:, sl] = _finish_rms(o, nw, gc_ref[:, sl]).astype(oc_ref.dtype)


def _gla(p_l, ps_l, p_c, ps_c, gate_up, gate_b, o_norm, need_ctx):
    B, S, _ = p_l.shape
    CT = p_c.shape[1]
    T = CT + S
    gup = jnp.zeros((2, N_SMALL, N_HEADS * GLA_DK), F32)
    for d in range(2):
        gup = gup.at[d, SM_RK + d * GLA_GATE_RANK:SM_RK + (d + 1) * GLA_GATE_RANK].set(gate_up[d])
    gup = gup.astype(BF)
    gb = gate_b.reshape(2, 1, N_HEADS * GLA_DK)

    def spec(n, width, off):
        return pl.BlockSpec((None, n, width), lambda b, p: (b, 0, off(p)))

    in_specs = []
    for n in (CT, S):
        in_specs += [spec(n, LANE, lambda p: GLA_Q + p), spec(n, LANE, lambda p: GLA_K + p),
                     spec(n, 2 * LANE, lambda p: GLA_V // 2 + p), spec(n, 2 * LANE, lambda p: GLA_G // 2 + p),
                     spec(n, N_SMALL, lambda p: 0)]
    in_specs += [pl.BlockSpec((2, N_SMALL, LANE), lambda b, p: (0, 0, p)),
                 pl.BlockSpec((2, 1, LANE), lambda b, p: (0, 0, p)),
                 pl.BlockSpec((1, LANE), lambda b, p: (0, 0))]
    out_specs = [pl.BlockSpec((None, S, 2 * LANE), lambda b, p: (b, 0, p))]
    out_shape = [jax.ShapeDtypeStruct((B, S, GROUP_W), BF)]
    if need_ctx:
        out_specs.append(pl.BlockSpec((None, CT, 2 * LANE), lambda b, p: (b, 0, p)))
        out_shape.append(jax.ShapeDtypeStruct((B, CT, GROUP_W), BF))
    res = pl.pallas_call(
        functools.partial(_gla_kernel, need_ctx=need_ctx, nc_ctx=CT // CHUNK, nc_lat=S // CHUNK),
        grid=(B, 2),
        in_specs=in_specs,
        out_specs=out_specs,
        out_shape=out_shape,
        scratch_shapes=[pltpu.VMEM((2, T, LANE), F32),
                        pltpu.VMEM((2, 2, LANE, LANE), F32),
                        pltpu.VMEM((T, 2 * LANE), F32),
                        pltpu.VMEM((T, 2 * LANE), F32)],
        compiler_params=_cp(("parallel", "parallel")),
        name="gla",
    )(p_c, p_c, p_c, p_c, ps_c, p_l, p_l, p_l, p_l, ps_l, gup, gb, o_norm.reshape(1, LANE))
    return (res[1] if need_ctx else None), res[0]


def _rope_tables(S):
    pos = np.arange(S)
    half = RET_DK // 2
    quarter = half // 2
    freqs = ROPE_BASE ** (-np.arange(quarter, dtype=np.float64) / quarter)
    cos = np.zeros((S, LANE), np.float64)
    sin_dn = np.zeros((S, LANE), np.float64)
    sin_up = np.zeros((S, LANE), np.float64)
    for head in range(2):
        for part, p in enumerate((pos // GRID_W, pos % GRID_W)):
            ang = p[:, None].astype(np.float64) * freqs[None, :]
            base = head * RET_DK + part * half
            cos[:, base:base + quarter] = np.cos(ang)
            cos[:, base + quarter:base + half] = np.cos(ang)
            sin_dn[:, base:base + quarter] = -np.sin(ang)
            sin_up[:, base + quarter:base + half] = np.sin(ang)
    return tuple(jnp.asarray(t, F32) for t in (cos, sin_dn, sin_up))


def _ret_kernel(qc_ref, kc_ref, vc_ref, gc_ref, ql_ref, kl_ref, vl_ref, gl_ref,
                cos_ref, sdn_ref, sup_ref, dl_ref, gnw_ref, *rest, need_ctx, nc_ctx, nc_lat):
    if need_ctx:
        ol_ref, oc_ref, qr_ref, kr_ref, st_ref, of_ref, ob_ref = rest
    else:
        ol_ref, qr_ref, kr_ref, st_ref, of_ref, ob_ref = rest
        oc_ref = None
    CT = nc_ctx * CHUNK
    S = nc_lat * CHUNK
    ri, ci, incl, _, _ = _tri_consts()
    hmask = _head_masks()
    pp = pl.program_id(1)

    def rope(x):
        quarter = RET_DK // 4
        return (x * cos_ref[...] + pltpu.roll(x, LANE - quarter, 1) * sdn_ref[...]
                + pltpu.roll(x, quarter, 1) * sup_ref[...])

    qr_ref[0:CT, :] = qc_ref[...].astype(F32)
    kr_ref[0:CT, :] = kc_ref[...].astype(F32) * (RET_DK ** -0.5)
    qr_ref[CT:CT + S, :] = rope(ql_ref[...].astype(F32))
    kr_ref[CT:CT + S, :] = rope(kl_ref[...].astype(F32) * (RET_DK ** -0.5))
    st_ref[...] = jnp.zeros_like(st_ref)

    dmat, qfac, kfac, gend = {}, {}, {}, {}
    rif = ri.astype(F32)
    cif = ci.astype(F32)
    col = lax.broadcasted_iota(jnp.int32, (CHUNK, 1), 0).astype(F32)
    for d in range(2):
        for hh in range(2):
            lgam = _log_sigmoid(dl_ref[pl.ds(d * N_HEADS + 2 * pp + hh, 1), 0:1])
            dist = (rif - cif) if d == 0 else (cif - rif)
            dmat[d, hh] = jnp.exp(jnp.where(incl[d], dist * lgam, -jnp.inf))
            steps = (col + 1.0) if d == 0 else (CHUNK - col)
            qfac[d, hh] = jnp.exp(steps * lgam)
            kfac[d, hh] = jnp.exp((CHUNK - steps) * lgam)
            gend[d, hh] = jnp.exp(CHUNK * lgam)

    def segment(v_ref, row0, nch, write):
        per_iter = 2 if nch % 2 == 0 else 1

        def body(it, carry):
            work = []
            for j in range(per_iter):
                n = it * per_iter + j
                for d in range(2):
                    c = n if d == 0 else nch - 1 - n
                    r0 = pl.multiple_of(c * CHUNK, CHUNK)
                    q2 = qr_ref[pl.ds(row0 + r0, CHUNK), :]
                    k2 = kr_ref[pl.ds(row0 + r0, CHUNK), :]
                    k2b = k2.astype(BF)
                    for hh in range(2):
                        vh = v_ref[pl.ds(r0, CHUNK), hh * LANE:(hh + 1) * LANE]
                        qh = jnp.where(hmask[hh], q2, 0.0).astype(BF) if write else None
                        work.append((d, hh, r0, vh, qh, k2, k2b))
            if write:
                atts = [(_dot_nt(qh, k2b) * dmat[d, hh]).astype(BF) for (d, hh, r0, vh, qh, k2, k2b) in work]
            upds = [_dot_tn(vh, (k2 * kfac[d, hh]).astype(BF)) for (d, hh, r0, vh, qh, k2, k2b) in work]
            state = {(d, hh): st_ref[d, hh] for d in range(2) for hh in range(2)}
            for i, (d, hh, r0, vh, qh, k2, k2b) in enumerate(work):
                st = state[d, hh]
                if write:
                    oref = of_ref if d == 0 else ob_ref
                    oref[pl.ds(row0 + r0, CHUNK), hh * LANE:(hh + 1) * LANE] = (
                        _dot(atts[i], vh) + _dot_nt(qh, st.astype(BF)) * qfac[d, hh])
                state[d, hh] = st * gend[d, hh] + upds[i]
            for (d, hh), st in state.items():
                st_ref[d, hh] = st
            return carry

        lax.fori_loop(0, nch // per_iter, body, 0)

    segment(vc_ref, 0, nc_ctx, need_ctx)
    segment(vl_ref, CT, nc_lat, True)

    def finish(o, w, g):
        mu = jnp.mean(o, axis=-1, keepdims=True)
        oc = o - mu
        var = jnp.mean(oc * oc, axis=-1, keepdims=True)
        return oc * lax.rsqrt(var + EPS) * w * _silu(g.astype(F32))

    for hh in range(2):
        sl = slice(hh * LANE, (hh + 1) * LANE)
        w = gnw_ref[:, sl]
        o = of_ref[CT:CT + S, sl] + ob_ref[CT:CT + S, sl]
        ol_ref[:, sl] = finish(o, w, gl_ref[:, sl]).astype(ol_ref.dtype)
        if need_ctx:
            o = of_ref[0:CT, sl] + ob_ref[0:CT, sl]
            oc_ref[:, sl] = finish(o, w, gc_ref[:, sl]).astype(oc_ref.dtype)


def _ret(p_l, p_c, decay_logit, gn_w, need_ctx):
    B, S, _ = p_l.shape
    CT = p_c.shape[1]
    T = CT + S
    cos, sdn, sup = _rope_tables(S)
    dl = jnp.broadcast_to(decay_logit.reshape(2 * N_HEADS, 1).astype(F32), (2 * N_HEADS, LANE))

    def spec(n, width, off):
        return pl.BlockSpec((None, n, width), lambda b, p: (b, 0, off(p)))

    in_specs = []
    for n in (CT, S):
        in_specs += [spec(n, LANE, lambda p: RET_Q + p), spec(n, LANE, lambda p: RET_K + p),
                     spec(n, 2 * LANE, lambda p: RET_V // 2 + p), spec(n, 2 * LANE, lambda p: RET_G // 2 + p)]
    tab = pl.BlockSpec((S, LANE), lambda b, p: (0, 0))
    in_specs += [tab, tab, tab,
                 pl.BlockSpec((2 * N_HEADS, LANE), lambda b, p: (0, 0)),
                 pl.BlockSpec((1, 2 * LANE), lambda b, p: (0, p))]
    out_specs = [pl.BlockSpec((None, S, 2 * LANE), lambda b, p: (b, 0, p))]
    out_shape = [jax.ShapeDtypeStruct((B, S, GROUP_W), BF)]
    if need_ctx:
        out_specs.append(pl.BlockSpec((None, CT, 2 * LANE), lambda b, p: (b, 0, p)))
        out_shape.append(jax.ShapeDtypeStruct((B, CT, GROUP_W), BF))
    res = pl.pallas_call(
        functools.partial(_ret_kernel, need_ctx=need_ctx, nc_ctx=CT // CHUNK, nc_lat=S // CHUNK),
        grid=(B, 2),
        in_specs=in_specs,
        out_specs=out_specs,
        out_shape=out_shape,
        scratch_shapes=[pltpu.VMEM((T, LANE), F32),
                        pltpu.VMEM((T, LANE), F32),
                        pltpu.VMEM((2, 2, LANE, LANE), F32),
                        pltpu.VMEM((T, 2 * LANE), F32),
                        pltpu.VMEM((T, 2 * LANE), F32)],
        compiler_params=_cp(("parallel", "parallel")),
        name="retention",
    )(p_c, p_c, p_c, p_c, p_l, p_l, p_l, p_l, cos, sdn, sup, dl, gn_w.reshape(1, GROUP_W))
    return (res[1] if need_ctx else None), res[0]


def _gdn_kernel(qc_ref, kc_ref, vc_ref, zc_ref, sc_ref, ql_ref, kl_ref, vl_ref, zl_ref, sl_ref,
                cwq_ref, cwk_ref, cwv_ref, alog_ref, dtb_ref, nw_ref, *rest, need_ctx, nc_ctx, nc_lat):
    if need_ctx:
        ol_ref, oc_ref = rest[:2]
        rest = rest[2:]
    else:
        ol_ref = rest[0]
        oc_ref = None
        rest = rest[1:]
    qs, ks, vs, u_ref, wq_ref, a_ref, ke_ref, ge_ref, st_ref, of_ref, ob_ref = rest
    CT = nc_ctx * CHUNK
    S = nc_lat * CHUNK
    pp = pl.program_id(1)
    ri = lax.broadcasted_iota(jnp.int32, (GDN_BLK, GDN_BLK), 0)
    ci = lax.broadcasted_iota(jnp.int32, (GDN_BLK, GDN_BLK), 1)
    same = (ri // CHUNK) == (ci // CHUNK)
    incl = (same & (ri >= ci), same & (ri <= ci))
    strict = (same & (ri > ci), same & (ri < ci))
    tri2_bf = jnp.concatenate([jnp.where(m, 1.0, 0.0).astype(BF) for m in incl], axis=0)
    eye_f = jnp.where(ri == ci, 1.0, 0.0)
    lane = lax.broadcasted_iota(jnp.int32, (GDN_BLK, LANE), 1)
    rowi = lax.broadcasted_iota(jnp.int32, (GDN_BLK, 1), 0)
    lvl_masks = []
    s = 1
    while s < CHUNK:
        lvl_masks.append(jnp.where(((ri // (2 * s)) == (ci // (2 * s))) & ((ri // s) != (ci // s)), 1.0, 0.0))
        s *= 2

    def conv_silu(x_ref, w_ref, n):
        x = x_ref[...].astype(F32)
        row = lax.broadcasted_iota(jnp.int32, x.shape, 0)
        xp = jnp.where(row == 0, 0.0, pltpu.roll(x, 1, 0))
        xn = jnp.where(row == n - 1, 0.0, pltpu.roll(x, n - 1, 0))
        return _silu(xp * w_ref[0:1, :] + x * w_ref[1:2, :] + xn * w_ref[2:3, :])

    def l2n(x):
        return x * lax.rsqrt(jnp.sum(x * x, axis=-1, keepdims=True) + EPS)

    for (q_ref, k_ref, v_ref, r0, n) in ((qc_ref, kc_ref, vc_ref, 0, CT), (ql_ref, kl_ref, vl_ref, CT, S)):
        q = conv_silu(q_ref, cwq_ref, n)
        k = conv_silu(k_ref, cwk_ref, n)
        v = conv_silu(v_ref, cwv_ref, n)
        for hh in range(2):
            sl = slice(hh * LANE, (hh + 1) * LANE)
            qs[r0:r0 + n, sl] = l2n(q[:, sl]) * (GDN_DK ** -0.5)
            ks[r0:r0 + n, sl] = l2n(k[:, sl])
        vs[r0:r0 + n, :] = v.astype(BF)

    neg_a = -jnp.exp(alog_ref[...])
    dtb = dtb_ref[...]

    def phase1(small_ref, row0, nblk):
        per_iter = 4 if nblk % 4 == 0 else (2 if nblk % 2 == 0 else 1)

        def body(it, carry):
            pipes = []
            a_list = []
            for j in range(per_iter):
                m = it * per_iter + j
                r0 = pl.multiple_of(m * GDN_BLK, GDN_BLK)
                g0 = pl.multiple_of(row0 + r0, GDN_BLK)
                sm = small_ref[pl.ds(r0, GDN_BLK), :]
                lg_all = neg_a * _softplus(sm + dtb)
                lb_all = _log_sigmoid(sm)
                cum2 = _exact_dot(tri2_bf, lg_all)
                for hh in range(2):
                    h = 2 * pp + hh
                    sl = slice(hh * LANE, (hh + 1) * LANE)
                    k = ks[pl.ds(g0, GDN_BLK), sl]
                    q = qs[pl.ds(g0, GDN_BLK), sl]
                    v = vs[pl.ds(g0, GDN_BLK), sl]
                    kb = k.astype(BF)
                    kkqk = _dot_nt(jnp.concatenate([kb, q.astype(BF)], axis=0), kb)
                    kk = kkqk[0:GDN_BLK, :]
                    qk = kkqk[GDN_BLK:2 * GDN_BLK, :]
                    for d in range(2):
                        cum_all = cum2[d * GDN_BLK:(d + 1) * GDN_BLK, :]
                        g = jnp.sum(jnp.where(lane == SM_A + d * N_HEADS + h, cum_all, 0.0), axis=-1, keepdims=True)
                        lb = jnp.sum(jnp.where(lane == SM_BT + d * N_HEADS + h, lb_all, 0.0), axis=-1, keepdims=True)
                        ends = (CHUNK - 1, GDN_BLK - 1) if d == 0 else (0, CHUNK)
                        tot_lo = g[ends[0]:ends[0] + 1, :]
                        tot_hi = g[ends[1]:ends[1] + 1, :]
                        tot = jnp.where(rowi < CHUNK, tot_lo, tot_hi)
                        hrow = jnp.broadcast_to(g - lb, (GDN_BLK, GDN_BLK)).T
                        e_in = jnp.exp(jnp.where(incl[d], g - hrow, -jnp.inf))
                        a_list.append(kk * jnp.where(strict[d], e_in, 0.0))
                        pipes.append((m, g0, hh, d, k, q, v, qk, e_in, g, lb, tot, tot_lo, tot_hi))
            a4 = jnp.stack(a_list, axis=0)
            x4 = eye_f[None] - a4 * lvl_masks[0][None]
            bdot = lambda p, r: jnp.einsum('pij,pjk->pik', p, r, preferred_element_type=F32)
            for msk in lvl_masks[1:]:
                t4 = (a4 * msk[None]).astype(BF)
                xb = x4.astype(BF)
                x4 = x4 - bdot(xb, bdot(t4, xb).astype(BF))
            x4b = x4.astype(BF)
            for p, (m, g0, hh, d, k, q, v, qk, e_in, g, lb, tot, tot_lo, tot_hi) in enumerate(pipes):
                gam = jnp.exp(g)
                rhs = jnp.concatenate([v, (k * gam).astype(BF)], axis=1)
                uw = _dot(x4b[p], rhs)
                u_ref[d, hh, pl.ds(g0, GDN_BLK), :] = uw[:, 0:LANE]
                w = uw[:, LANE:2 * LANE].astype(BF)
                qg = (q * gam).astype(BF)
                wq0 = pl.multiple_of(2 * g0, 2 * GDN_BLK)
                wq_ref[d, hh, pl.ds(wq0, 2 * GDN_BLK), :] = jnp.concatenate(
                    [w[0:CHUNK], qg[0:CHUNK], w[CHUNK:GDN_BLK], qg[CHUNK:GDN_BLK]], axis=0)
                a_ref[d, hh, pl.ds(g0, GDN_BLK), :] = (qk * e_in).astype(BF)
                ke_ref[d, hh, pl.ds(g0, GDN_BLK), :] = (k * jnp.exp(tot - g + lb)).astype(BF)
                ge0 = pl.multiple_of((row0 // CHUNK + 2 * m) * 8, 16)
                ge_ref[d, hh, pl.ds(ge0, 16), :] = jnp.concatenate(
                    [jnp.broadcast_to(jnp.exp(tot_lo), (8, LANE)), jnp.broadcast_to(jnp.exp(tot_hi), (8, LANE))], axis=0)
            return carry

        lax.fori_loop(0, nblk // per_iter, body, 0)

    phase1(sc_ref, 0, CT // GDN_BLK)
    phase1(sl_ref, CT, S // GDN_BLK)
    st_ref[...] = jnp.zeros_like(st_ref)

    def phase2(row0, nch, write):
        per_iter = 2 if nch % 2 == 0 else 1

        def body(it, carry):
            state = {(d, hh): st_ref[d, hh] for d in range(2) for hh in range(2)}
            outs = []
            for j in range(per_iter):
                n = it * per_iter + j
                chains = []
                for d in range(2):
                    c = n if d == 0 else nch - 1 - n
                    g0 = pl.multiple_of(row0 + c * CHUNK, CHUNK)
                    for hh in range(2):
                        chains.append((d, hh, c, g0))
                wss = [_dot(wq_ref[d, hh, pl.ds(pl.multiple_of(2 * g0, 2 * CHUNK), 2 * CHUNK), :],
                            state[d, hh].astype(BF)) for (d, hh, c, g0) in chains]
                dbs = [(u_ref[d, hh, pl.ds(g0, CHUNK), :] - ws[0:CHUNK, :]).astype(BF)
                       for (d, hh, c, g0), ws in zip(chains, wss)]
                upd = [_dot_tn(ke_ref[d, hh, pl.ds(g0, CHUNK), :], db) for (d, hh, c, g0), db in zip(chains, dbs)]
                for (d, hh, c, g0), up in zip(chains, upd):
                    ge = ge_ref[d, hh, pl.ds(pl.multiple_of((row0 // CHUNK + c) * 8, 8), 1), :]
                    state[d, hh] = state[d, hh] * ge[:, 0:1] + up
                outs.append((chains, wss, dbs))
            for (d, hh), st in state.items():
                st_ref[d, hh] = st
            if write:
                for chains, wss, dbs in outs:
                    for (d, hh, c, g0), ws, db in zip(chains, wss, dbs):
                        o = ws[CHUNK:2 * CHUNK, :] + _dot(a_ref[d, hh, pl.ds(g0, CHUNK), :],
                                                          jnp.concatenate([db, db], axis=0))
                        oref = of_ref if d == 0 else ob_ref
                        oref[pl.ds(g0, CHUNK), hh * LANE:(hh + 1) * LANE] = o
            return carry

        lax.fori_loop(0, nch // per_iter, body, 0)

    phase2(0, nc_ctx, need_ctx)
    phase2(CT, nc_lat, True)

    nw = nw_ref[...]
    for hh in range(2):
        sl = slice(hh * LANE, (hh + 1) * LANE)
        o = of_ref[CT:CT + S, sl] + ob_ref[CT:CT + S, sl]
        ol_ref[:, sl] = _finish_rms(o, nw, zl_ref[:, sl]).astype(ol_ref.dtype)
        if need_ctx:
            o = of_ref[0:CT, sl] + ob_ref[0:CT, sl]
            oc_ref[:, sl] = _finish_rms(o, nw, zc_ref[:, sl]).astype(oc_ref.dtype)


def _gdn(p_l, ps_l, p_c, ps_c, conv_w, a_log, dt_bias, o_norm, need_ctx):
    B, S, _ = p_l.shape
    CT = p_c.shape[1]
    T = CT + S
    nch = T // CHUNK
    alog = jnp.zeros((1, N_SMALL), F32).at[0, SM_A:SM_A + 2 * N_HEADS].set(a_log.reshape(-1).astype(F32))
    dtb = jnp.zeros((1, N_SMALL), F32).at[0, SM_A:SM_A + 2 * N_HEADS].set(dt_bias.reshape(-1).astype(F32))

    def spec(n, width, off):
        return pl.BlockSpec((None, n, width), lambda b, p: (b, 0, off(p)))

    in_specs = []
    for n in (CT, S):
        in_specs += [spec(n, 2 * LANE, lambda p: GDN_Q // 2 + p), spec(n, 2 * LANE, lambda p: GDN_K // 2 + p),
                     spec(n, 2 * LANE, lambda p: GDN_V // 2 + p), spec(n, 2 * LANE, lambda p: GDN_Z // 2 + p),
                     spec(n, N_SMALL, lambda p: 0)]
    cw = lambda part: pl.BlockSpec((3, 2 * LANE), lambda b, p: (0, 2 * part + p))
    vec = pl.BlockSpec((1, LANE), lambda b, p: (0, 0))
    in_specs += [cw(0), cw(1), cw(2), vec, vec, vec]
    out_specs = [pl.BlockSpec((None, S, 2 * LANE), lambda b, p: (b, 0, p))]
    out_shape = [jax.ShapeDtypeStruct((B, S, GROUP_W), BF)]
    if need_ctx:
        out_specs.append(pl.BlockSpec((None, CT, 2 * LANE), lambda b, p: (b, 0, p)))
        out_shape.append(jax.ShapeDtypeStruct((B, CT, GROUP_W), BF))
    res = pl.pallas_call(
        functools.partial(_gdn_kernel, need_ctx=need_ctx, nc_ctx=CT // CHUNK, nc_lat=S // CHUNK),
        grid=(B, 2),
        in_specs=in_specs,
        out_specs=out_specs,
        out_shape=out_shape,
        scratch_shapes=[pltpu.VMEM((T, 2 * LANE), F32),
                        pltpu.VMEM((T, 2 * LANE), F32),
                        pltpu.VMEM((T, 2 * LANE), BF),
                        pltpu.VMEM((2, 2, T, LANE), F32),
                        pltpu.VMEM((2, 2, 2 * T, LANE), BF),
                        pltpu.VMEM((2, 2, T, GDN_BLK), BF),
                        pltpu.VMEM((2, 2, T, LANE), BF),
                        pltpu.VMEM((2, 2, nch * 8, LANE), F32),
                        pltpu.VMEM((2, 2, LANE, LANE), F32),
                        pltpu.VMEM((T, 2 * LANE), F32),
                        pltpu.VMEM((T, 2 * LANE), F32)],
        compiler_params=_cp(("parallel", "parallel")),
        name="gdn",
    )(p_c, p_c, p_c, p_c, ps_c, p_l, p_l, p_l, p_l, ps_l,
      conv_w, conv_w, conv_w, alog, dtb, o_norm.reshape(1, LANE))
    return (res[1] if need_ctx else None), res[0]


def _align_w_in(w):
    D = w.shape[0]
    big = jnp.concatenate([w[:, 0:3072], w[:, 3104:5152], w[:, 5168:6704]], axis=1)
    small = jnp.concatenate([w[:, 3072:3104], w[:, 5152:5168], jnp.zeros((D, N_SMALL - 48), w.dtype)], axis=1)
    return big.astype(BF), small.astype(BF)


def _pick(n, prefs):
    for p in prefs:
        if n % p == 0:
            return p
    return n


def kernel(x, c, ctx, c_ctx, ada_w, ada_b, norm1_w, norm2_w, w_in, w_out, na_q_norm, na_k_norm, na_rpb,
           gla_gate_up, gla_gate_b, gla_o_norm, gdn_conv_w, gdn_a_log, gdn_dt_bias, gdn_o_norm,
           ret_decay_logit, ret_gn_w, mlp_w1, mlp_w2):
    B, S, D = x.shape
    CT = ctx.shape[1]
    depth = ada_w.shape[0]
    R = ((B + 1 + 7) // 8) * 8
    cc = jnp.concatenate([c, c_ctx[None, :], jnp.zeros((R - B - 1, D), F32)], axis=0)
    mod_all = _ada(cc, ada_w, ada_b).reshape(depth, R, 6, D)

    tm_l = _pick(S, (1024, 512, 256))
    tm_c = _pick(B * CT, (1024, 512, 256))
    tn = _pick(N_BIG, (1664, 512, 256, 128))
    tm_o = _pick(S, (512, 256))
    tm_oc = _pick(B * CT, (512, 256))
    th = _pick(mlp_w1.shape[2], (1024, 512, 256))

    xl = x
    xc = ctx.reshape(1, B * CT, D)
    for layer in range(depth):
        need_ctx = layer < depth - 1
        mod = mod_all[layer]
        w_big, w_small = _align_w_in(w_in[layer])
        nw1 = norm1_w[layer].reshape(1, D)
        nw2 = norm2_w[layer].reshape(1, D)
        wo = w_out[layer].astype(BF)
        w1 = mlp_w1[layer].astype(BF)
        w2 = mlp_w2[layer].astype(BF)

        p_l, ps_l = _in_proj(xl, mod, None, nw1, w_big, w_small, tm_l, tn)
        p_c, ps_c = _in_proj(xc, mod, B, nw1, w_big, w_small, tm_c, tn)
        p_c = p_c.reshape(B, CT, N_BIG)
        ps_c = ps_c.reshape(B, CT, N_SMALL)

        na_c, na_l = _natten(p_l, p_c, na_q_norm[layer], na_k_norm[layer], na_rpb[layer], need_ctx)
        gl_c, gl_l = _gla(p_l, ps_l, p_c, ps_c, gla_gate_up[layer], gla_gate_b[layer], gla_o_norm[layer], need_ctx)
        gd_c, gd_l = _gdn(p_l, ps_l, p_c, ps_c, gdn_conv_w[layer], gdn_a_log[layer], gdn_dt_bias[layer],
                          gdn_o_norm[layer], need_ctx)
        rt_c, rt_l = _ret(p_l, p_c, ret_decay_logit[layer], ret_gn_w[layer], need_ctx)

        xl = _out_proj(xl, (na_l, gl_l, gd_l, rt_l), wo, mod, None, tm_o)
        xl = _mlp(xl, mod, None, nw2, w1, w2, tm_o, th)
        if need_ctx:
            ys = tuple(t.reshape(1, B * CT, GROUP_W) for t in (na_c, gl_c, gd_c, rt_c))
            xc = _out_proj(xc, ys, wo, mod, B, tm_oc)
            xc = _mlp(xc, mod, B, nw2, w1, w2, tm_oc, th)
    return xl
```

```python
import functools

import numpy as np
import jax
import jax.numpy as jnp
from jax import lax
from jax.experimental import pallas as pl
from jax.experimental.pallas import tpu as pltpu

BF = jnp.bfloat16
F32 = jnp.float32

N_HEADS = 4
HEAD_DIM = 128
GROUP_W = N_HEADS * HEAD_DIM
GRID_W = 64
NA_WIN_ROWS = 8
NA_WIN_COLS = 16
NA_GROUP = 4
GLA_DK = 64
GLA_GATE_RANK = 16
GLA_GATE_TAU = 16.0
GDN_DK = 128
RET_DK = 64
ROPE_BASE = 10000.0
CHUNK = 64
GDN_BLK = 2 * CHUNK
EPS = 1e-6
NEG_INF = -1e30

LANE = 128
N_BIG = 52 * LANE
N_SMALL = LANE
NA_Q, NA_K, NA_V = 0, 4, 8
GLA_Q, GLA_K, GLA_V, GLA_G = 12, 14, 16, 20
GDN_Q, GDN_K, GDN_V, GDN_Z = 24, 28, 32, 36
RET_Q, RET_K, RET_V, RET_G = 40, 42, 44, 48
SM_RK, SM_A, SM_BT = 0, 32, 40

VMEM_LIMIT = 56 * 1024 * 1024


def _cp(sem, vmem=VMEM_LIMIT):
    return pltpu.CompilerParams(dimension_semantics=sem, vmem_limit_bytes=vmem)


def _dot(a, b):
    return jnp.dot(a, b, preferred_element_type=F32)


def _dot_nt(a, b):
    return lax.dot_general(a, b, (((1,), (1,)), ((), ())), preferred_element_type=F32)


def _dot_tn(a, b):
    return lax.dot_general(a, b, (((0,), (0,)), ((), ())), preferred_element_type=F32)


def _split3(x):
    hi = x.astype(BF)
    r1 = x - hi.astype(F32)
    mid = r1.astype(BF)
    lo = (r1 - mid.astype(F32)).astype(BF)
    return hi, mid, lo


def _exact_dot(m_bf, x):
    hi, mid, lo = _split3(x)
    n = x.shape[1]
    r = _dot(m_bf, jnp.concatenate([hi, mid, lo], axis=1))
    return r[:, 0:n] + r[:, n:2 * n] + r[:, 2 * n:3 * n]


def _sigmoid(x):
    return 1.0 / (1.0 + jnp.exp(-x))


def _silu(x):
    return x * _sigmoid(x)


def _log_sigmoid(x):
    return jnp.minimum(x, 0.0) - jnp.log(1.0 + jnp.exp(-jnp.abs(x)))


def _softplus(x):
    return jnp.maximum(x, 0.0) + jnp.log(1.0 + jnp.exp(-jnp.abs(x)))


def _ln_mod(x, nw, shift, scale):
    ms = jnp.mean(x * x, axis=-1, keepdims=True)
    return (x * lax.rsqrt(ms + EPS) * nw) * (1.0 + scale) + shift


def _tri_consts():
    ri = lax.broadcasted_iota(jnp.int32, (CHUNK, CHUNK), 0)
    ci = lax.broadcasted_iota(jnp.int32, (CHUNK, CHUNK), 1)
    incl = (ri >= ci, ri <= ci)
    strict = (ri > ci, ri < ci)
    tri_bf = tuple(jnp.where(m, 1.0, 0.0).astype(BF) for m in incl)
    return ri, ci, incl, strict, tri_bf


def _ada_kernel(c_ref, w_ref, b_ref, o_ref):
    sc = _silu(c_ref[...]).astype(BF)
    o_ref[...] = _dot(sc, w_ref[...].astype(BF)) + b_ref[...]


def _ada(cc, ada_w, ada_b):
    L, D, N6 = ada_w.shape
    R = cc.shape[0]
    tn = 1024 if N6 % 1024 == 0 else N6
    return pl.pallas_call(
        _ada_kernel,
        grid=(L, N6 // tn),
        in_specs=[pl.BlockSpec((R, D), lambda l, j: (0, 0)),
                  pl.BlockSpec((None, D, tn), lambda l, j: (l, 0, j)),
                  pl.BlockSpec((None, 1, tn), lambda l, j: (l, 0, j))],
        out_specs=pl.BlockSpec((None, R, tn), lambda l, j: (l, 0, j)),
        out_shape=jax.ShapeDtypeStruct((L, R, N6), F32),
        compiler_params=_cp(("parallel", "parallel")),
        name="ada_ln",
    )(cc, ada_w, ada_b.reshape(L, 1, N6))


def _inproj_kernel(x_ref, mod_ref, nw_ref, w_ref, ws_ref, o_ref, os_ref, h_ref):
    @pl.when(pl.program_id(1) == 0)
    def _():
        h = _ln_mod(x_ref[...], nw_ref[...], mod_ref[0:1, :], mod_ref[1:2, :]).astype(BF)
        h_ref[...] = h
        os_ref[...] = _dot(h, ws_ref[...])

    o_ref[...] = _dot(h_ref[...], w_ref[...]).astype(o_ref.dtype)


def _in_proj(x3, mod, const_row, nw, w_big, w_small, tm, tn):
    Bn, Tn, D = x3.shape
    nt = Tn // tm
    if const_row is None:
        mod_map = lambda i, j: (i // nt, 0, 0)
    else:
        mod_map = lambda i, j: (const_row, 0, 0)
    return pl.pallas_call(
        _inproj_kernel,
        grid=(Bn * nt, N_BIG // tn),
        in_specs=[pl.BlockSpec((None, tm, D), lambda i, j: (i // nt, i % nt, 0)),
                  pl.BlockSpec((None, 6, D), mod_map),
                  pl.BlockSpec((1, D), lambda i, j: (0, 0)),
                  pl.BlockSpec((D, tn), lambda i, j: (0, j)),
                  pl.BlockSpec((D, N_SMALL), lambda i, j: (0, 0))],
        out_specs=[pl.BlockSpec((None, tm, tn), lambda i, j: (i // nt, i % nt, j)),
                   pl.BlockSpec((None, tm, N_SMALL), lambda i, j: (i // nt, i % nt, 0))],
        out_shape=[jax.ShapeDtypeStruct((Bn, Tn, N_BIG), BF),
                   jax.ShapeDtypeStruct((Bn, Tn, N_SMALL), F32)],
        scratch_shapes=[pltpu.VMEM((tm, D), BF)],
        compiler_params=_cp(("parallel", "arbitrary")),
        name="in_proj",
    )(x3, mod, nw, w_big, w_small)


def _outproj_kernel(x_ref, y0, y1, y2, y3, w_ref, mod_ref, o_ref):
    acc = _dot(y0[...], w_ref[0 * GROUP_W:1 * GROUP_W, :])
    acc += _dot(y1[...], w_ref[1 * GROUP_W:2 * GROUP_W, :])
    acc += _dot(y2[...], w_ref[2 * GROUP_W:3 * GROUP_W, :])
    acc += _dot(y3[...], w_ref[3 * GROUP_W:4 * GROUP_W, :])
    o_ref[...] = x_ref[...] + mod_ref[2:3, :] * acc


def _out_proj(x3, ys, w_out, mod, const_row, tm):
    Bn, Tn, D = x3.shape
    nt = Tn // tm
    if const_row is None:
        mod_map = lambda i: (i // nt, 0, 0)
    else:
        mod_map = lambda i: (const_row, 0, 0)
    row_map = lambda i: (i // nt, i % nt, 0)
    return pl.pallas_call(
        _outproj_kernel,
        grid=(Bn * nt,),
        in_specs=[pl.BlockSpec((None, tm, D), row_map)]
                 + [pl.BlockSpec((None, tm, GROUP_W), row_map)] * 4
                 + [pl.BlockSpec((4 * GROUP_W, D), lambda i: (0, 0)),
                    pl.BlockSpec((None, 6, D), mod_map)],
        out_specs=pl.BlockSpec((None, tm, D), row_map),
        out_shape=jax.ShapeDtypeStruct((Bn, Tn, D), F32),
        compiler_params=_cp(("parallel",)),
        name="out_proj",
    )(x3, *ys, w_out, mod)


def _mlp_kernel(x_ref, mod_ref, nw_ref, w1_ref, w2_ref, o_ref, h_ref, *, nk):
    k = pl.program_id(1)

    @pl.when(k == 0)
    def _():
        h_ref[...] = _ln_mod(x_ref[...], nw_ref[...], mod_ref[3:4, :], mod_ref[4:5, :]).astype(BF)
        o_ref[...] = jnp.zeros_like(o_ref)

    hid = jnp.maximum(_dot(h_ref[...], w1_ref[...]), 0.0)
    o_ref[...] += _dot((hid * hid).astype(BF), w2_ref[...])

    @pl.when(k == nk - 1)
    def _():
        o_ref[...] = x_ref[...] + mod_ref[5:6, :] * o_ref[...]


def _mlp(x3, mod, const_row, nw, w1, w2, tm, th):
    Bn, Tn, D = x3.shape
    Hd = w1.shape[1]
    nt = Tn // tm
    nk = Hd // th
    if const_row is None:
        mod_map = lambda i, k: (i // nt, 0, 0)
    else:
        mod_map = lambda i, k: (const_row, 0, 0)
    row_map = lambda i, k: (i // nt, i % nt, 0)
    return pl.pallas_call(
        functools.partial(_mlp_kernel, nk=nk),
        grid=(Bn * nt, nk),
        in_specs=[pl.BlockSpec((None, tm, D), row_map),
                  pl.BlockSpec((None, 6, D), mod_map),
                  pl.BlockSpec((1, D), lambda i, k: (0, 0)),
                  pl.BlockSpec((D, th), lambda i, k: (0, k)),
                  pl.BlockSpec((th, D), lambda i, k: (k, 0))],
        out_specs=pl.BlockSpec((None, tm, D), row_map),
        out_shape=jax.ShapeDtypeStruct((Bn, Tn, D), F32),
        scratch_shapes=[pltpu.VMEM((tm, D), BF)],
        compiler_params=_cp(("parallel", "arbitrary")),
        name="mlp",
    )(x3, mod, nw, w1, w2)


def _rms_head(x, w):
    x = x.astype(F32)
    return x * lax.rsqrt(jnp.mean(x * x, axis=-1, keepdims=True) + EPS) * w


def _natten_kernel(geo_ref, ql_ref, kl_ref, vl_ref, qc_ref, kc_ref, vc_ref, qw_ref, kw_ref, bias_ref, *rest,
                   need_ctx, rows, grp, span):
    if need_ctx:
        ol_ref, oc_ref, qs, ks = rest
    else:
        ol_ref, qs, ks = rest
    scale = HEAD_DIM ** -0.5
    qw = qw_ref[...]
    kw = kw_ref[...]
    qs[...] = (_rms_head(ql_ref[...], qw) * scale).astype(BF)
    ks[...] = _rms_head(kl_ref[...], kw).astype(BF)
    kc = _rms_head(kc_ref[...], kw).astype(BF)
    vc = vc_ref[...]
    if need_ctx:
        qc = (_rms_head(qc_ref[...], qw) * scale).astype(BF)
        s = _dot_nt(qc, kc)
        p = jnp.exp(s - jnp.max(s, axis=-1, keepdims=True))
        l = jnp.sum(p, axis=-1, keepdims=True)
        oc_ref[...] = (_dot(p.astype(BF), vc) / l).astype(oc_ref.dtype)

    nq = grp * GRID_W
    nk = span * GRID_W

    ngroups = rows // grp
    per_iter = 2 if ngroups % 2 == 0 else 1

    def body(it, carry):
        geo = []
        for j in range(per_iter):
            g = it * per_iter + j
            geo.append((pl.multiple_of(g * nq, nq), pl.multiple_of(geo_ref[0, g] * GRID_W, GRID_W), geo_ref[1, g]))
        qv = [qs[pl.ds(q0, nq), :] for (q0, k0, t) in geo]
        sws = [_dot_nt(q, ks[pl.ds(k0, nk), :]) + bias_ref[t] for q, (q0, k0, t) in zip(qv, geo)]
        scs = [_dot_nt(q, kc) for q in qv]
        probs = []
        for sw, sc in zip(sws, scs):
            m = jnp.maximum(jnp.max(sw, axis=-1, keepdims=True), jnp.max(sc, axis=-1, keepdims=True))
            pw = jnp.exp(sw - m)
            pc = jnp.exp(sc - m)
            l = jnp.sum(pw, axis=-1, keepdims=True) + jnp.sum(pc, axis=-1, keepdims=True)
            probs.append((pw.astype(BF), pc.astype(BF), l))
        for (pw, pc, l), (q0, k0, t) in zip(probs, geo):
            o = (_dot(pw, vl_ref[pl.ds(k0, nk), :]) + _dot(pc, vc)) / l
            ol_ref[pl.ds(q0, nq), :] = o.astype(ol_ref.dtype)
        return carry

    lax.fori_loop(0, ngroups // per_iter, body, 0)


def _natten_geometry(rows, kh):
    grp = NA_GROUP if rows % NA_GROUP == 0 else 1
    span = min(kh + grp - 1, rows)
    starts, type_ids, types = [], [], []
    for g in range(rows // grp):
        rs = [int(np.clip(g * grp + j - kh // 2, 0, rows - kh)) for j in range(grp)]
        us = int(np.clip(rs[0], 0, rows - span))
        sig = tuple((rs[j] - us, g * grp + j - rs[j]) for j in range(grp))
        assert all(0 <= off <= span - kh for off, _ in sig)
        if sig not in types:
            types.append(sig)
        starts.append(us)
        type_ids.append(types.index(sig))
    return grp, span, np.asarray([starts, type_ids], np.int32), types


def _natten_bias(rpb, kh, span, types):
    q = np.arange(GRID_W)[:, None]
    kc = np.arange(GRID_W)[None, :]
    col_off = np.clip(kc - q, -(NA_WIN_COLS - 1), NA_WIN_COLS - 1) + NA_WIN_COLS - 1
    onehot = (col_off[..., None] == np.arange(2 * NA_WIN_COLS - 1)).astype(np.float32)
    toe = jnp.einsum('hrc,qkc->hrqk', rpb.astype(F32), jnp.asarray(onehot), precision=lax.Precision.HIGHEST)
    cs = np.clip(q - NA_WIN_COLS // 2, 0, GRID_W - NA_WIN_COLS)
    valid = (kc >= cs) & (kc < cs + NA_WIN_COLS)
    toe = jnp.where(valid[None, None], toe, NEG_INF)
    H = rpb.shape[0]
    masked = jnp.full((H, GRID_W, GRID_W), NEG_INF, F32)
    tables = []
    for sig in types:
        rows_ = []
        for off, d in sig:
            blocks = [toe[:, NA_WIN_ROWS - 1 - d + i - off] if 0 <= i - off < kh else masked for i in range(span)]
            rows_.append(jnp.stack(blocks, axis=2).reshape(H, GRID_W, span * GRID_W))
        tables.append(jnp.concatenate(rows_, axis=1))
    return jnp.stack(tables, axis=1)


def _natten(p_l, p_c, qw, kw, rpb, need_ctx):
    B, S, _ = p_l.shape
    CT = p_c.shape[1]
    rows = S // GRID_W
    kh = min(NA_WIN_ROWS, rows)
    grp, span, geo, types = _natten_geometry(rows, kh)
    bias = _natten_bias(rpb, kh, span, types)
    lat = lambda off: pl.BlockSpec((None, S, LANE), lambda b, h: (b, 0, off + h))
    ctx = lambda off: pl.BlockSpec((None, CT, LANE), lambda b, h: (b, 0, off + h))
    vec = pl.BlockSpec((1, LANE), lambda b, h: (0, 0))
    out_specs = [pl.BlockSpec((None, S, LANE), lambda b, h: (b, 0, h))]
    out_shape = [jax.ShapeDtypeStruct((B, S, GROUP_W), BF)]
    if need_ctx:
        out_specs.append(pl.BlockSpec((None, CT, LANE), lambda b, h: (b, 0, h)))
        out_shape.append(jax.ShapeDtypeStruct((B, CT, GROUP_W), BF))
    res = pl.pallas_call(
        functools.partial(_natten_kernel, need_ctx=need_ctx, rows=rows, grp=grp, span=span),
        grid=(B, N_HEADS),
        in_specs=[pl.BlockSpec(memory_space=pltpu.SMEM),
                  lat(NA_Q), lat(NA_K), lat(NA_V), ctx(NA_Q), ctx(NA_K), ctx(NA_V), vec, vec,
                  pl.BlockSpec((None, len(types), grp * GRID_W, span * GRID_W), lambda b, h: (h, 0, 0, 0))],
        out_specs=out_specs,
        out_shape=out_shape,
        scratch_shapes=[pltpu.VMEM((S, LANE), BF), pltpu.VMEM((S, LANE), BF)],
        compiler_params=_cp(("parallel", "parallel")),
        name="natten",
    )(jnp.asarray(geo), p_l, p_l, p_l, p_c, p_c, p_c, qw.reshape(1, LANE), kw.reshape(1, LANE), bias)
    return (res[1] if need_ctx else None), res[0]


def _pair_consts():
    lane = lax.broadcasted_iota(jnp.int32, (CHUNK, LANE), 1)
    row = lax.broadcasted_iota(jnp.int32, (CHUNK, LANE), 0)
    col = lane & (CHUNK - 1)
    hmask = (lane < CHUNK, lane >= CHUNK)
    incl2 = (row >= col, row <= col)
    r2 = lax.broadcasted_iota(jnp.int32, (LANE, 2 * LANE), 0)
    c2 = lax.broadcasted_iota(jnp.int32, (LANE, 2 * LANE), 1)
    bmask = (r2 < CHUNK) == (c2 < LANE)
    eye = lax.broadcasted_iota(jnp.int32, (LANE, LANE), 0) == lax.broadcasted_iota(jnp.int32, (LANE, LANE), 1)
    return hmask, incl2, bmask, eye


def _finish_rms(o, nw, g):
    y = o * lax.rsqrt(jnp.mean(o * o, axis=-1, keepdims=True) + EPS) * nw
    return y * _silu(g.astype(F32))


def _gla_kernel(qc_ref, kc_ref, vc_ref, gc_ref, sc_ref, ql_ref, kl_ref, vl_ref, gl_ref, sl_ref,
                gup_ref, gb_ref, nw_ref, *rest, need_ctx, nc_ctx, nc_lat):
    if need_ctx:
        ol_ref, oc_ref, lg_ref, st_ref, of_ref, ob_ref = rest
    else:
        ol_ref, lg_ref, st_ref, of_ref, ob_ref = rest
        oc_ref = None
    CT = nc_ctx * CHUNK
    S = nc_lat * CHUNK
    _, _, _, _, tri_bf = _tri_consts()
    hmask, incl2, bmask, eye = _pair_consts()
    zero_v = jnp.zeros((CHUNK, LANE), BF)

    for d in range(2):
        for (s_ref, r0, n) in ((sc_ref, 0, CT), (sl_ref, CT, S)):
            z = _dot(s_ref[...].astype(BF), gup_ref[d]) + gb_ref[d]
            lg_ref[d, r0:r0 + n, :] = _log_sigmoid(z) * (1.0 / GLA_GATE_TAU)
    st_ref[...] = jnp.zeros_like(st_ref)

    def segment(q_ref, k_ref, v_ref, row0, nch, write):
        per_iter = 4 if nch % 4 == 0 else (2 if nch % 2 == 0 else 1)

        def body(it, carry):
            pre = []
            for j in range(per_iter):
                n = it * per_iter + j
                for d in range(2):
                    c = n if d == 0 else nch - 1 - n
                    r0 = pl.multiple_of(c * CHUNK, CHUNK)
                    pre.append((d, r0, q_ref[pl.ds(r0, CHUNK), :].astype(F32), k_ref[pl.ds(r0, CHUNK), :].astype(F32)))
            cums = [_exact_dot(tri_bf[d], lg_ref[d, pl.ds(row0 + r0, CHUNK), :]) for (d, r0, _, _) in pre]
            work = []
            for (d, r0, q2, k2), cum in zip(pre, cums):
                tot = cum[CHUNK - 1:CHUNK, :] if d == 0 else cum[0:1, :]
                ge_col = jnp.sum(jnp.where(eye, jnp.exp(tot), 0.0), axis=1, keepdims=True)
                ke = (k2 * jnp.exp(tot - cum)).astype(BF)
                vp = v_ref[pl.ds(r0, CHUNK), :]
                qd = kdm = None
                if write:
                    qd = (q2 * (jnp.exp(cum) * (GLA_DK ** -0.5))).astype(BF)
                    kd = k2 * jnp.exp(-cum)
                    kdm = jnp.concatenate([jnp.where(hmask[0], kd, 0.0), jnp.where(hmask[1], kd, 0.0)],
                                          axis=0).astype(BF)
                work.append((d, r0, vp, qd, kdm, ke, ge_col))
            if write:
                atts = [jnp.where(incl2[d], _dot_nt(qd, kdm), 0.0).astype(BF)
                        for (d, r0, vp, qd, kdm, ke, ge_col) in work]
                intras = [_dot(att, jnp.concatenate(
                              [jnp.concatenate([vp[:, 0:LANE], zero_v], axis=1),
                               jnp.concatenate([zero_v, vp[:, LANE:2 * LANE]], axis=1)], axis=0))
                          for att, (d, r0, vp, qd, kdm, ke, ge_col) in zip(atts, work)]
            upds = [jnp.where(bmask, _dot_tn(ke, vp), 0.0) for (d, r0, vp, qd, kdm, ke, ge_col) in work]
            state = {d: st_ref[d] for d in range(2)}
            for i, (d, r0, vp, qd, kdm, ke, ge_col) in enumerate(work):
                st = state[d]
                if write:
                    oref = of_ref if d == 0 else ob_ref
                    oref[pl.ds(row0 + r0, CHUNK), :] = intras[i] + _dot(qd, st.astype(BF))
                state[d] = st * ge_col + upds[i]
            for d, st in state.items():
                st_ref[d] = st
            return carry

        lax.fori_loop(0, nch // per_iter, body, 0)

    segment(qc_ref, kc_ref, vc_ref, 0, nc_ctx, need_ctx)
    segment(ql_ref, kl_ref, vl_ref, CT, nc_lat, True)

    nw = nw_ref[...]
    for hh in range(2):
        sl = slice(hh * LANE, (hh + 1) * LANE)
        o = of_ref[CT:CT + S, sl] + ob_ref[CT:CT + S, sl]
        ol_ref[:, sl] = _finish_rms(o, nw, gl_ref[:, sl]).astype(ol_ref.dtype)
        if need_ctx:
            o = of_ref[0:CT, sl] + ob_ref[0:CT, sl]
            oc_ref[:, sl] = _finish_rms(o, nw, gc_ref[:, sl]).astype(oc_ref.dtype)


def _gla(p_l, ps_l, p_c, ps_c, gate_up, gate_b, o_norm, need_ctx):
    B, S, _ = p_l.shape
    CT = p_c.shape[1]
    T = CT + S
    gup = jnp.zeros((2, N_SMALL, N_HEADS * GLA_DK), F32)
    for d in range(2):
        gup = gup.at[d, SM_RK + d * GLA_GATE_RANK:SM_RK + (d + 1) * GLA_GATE_RANK].set(gate_up[d])
    gup = gup.astype(BF)
    gb = gate_b.reshape(2, 1, N_HEADS * GLA_DK)

    def spec(n, width, off):
        return pl.BlockSpec((None, n, width), lambda b, p: (b, 0, off(p)))

    in_specs = []
    for n in (CT, S):
        in_specs += [spec(n, LANE, lambda p: GLA_Q + p), spec(n, LANE, lambda p: GLA_K + p),
                     spec(n, 2 * LANE, lambda p: GLA_V // 2 + p), spec(n, 2 * LANE, lambda p: GLA_G // 2 + p),
                     spec(n, N_SMALL, lambda p: 0)]
    in_specs += [pl.BlockSpec((2, N_SMALL, LANE), lambda b, p: (0, 0, p)),
                 pl.BlockSpec((2, 1, LANE), lambda b, p: (0, 0, p)),
                 pl.BlockSpec((1, LANE), lambda b, p: (0, 0))]
    out_specs = [pl.BlockSpec((None, S, 2 * LANE), lambda b, p: (b, 0, p))]
    out_shape = [jax.ShapeDtypeStruct((B, S, GROUP_W), BF)]
    if need_ctx:
        out_specs.append(pl.BlockSpec((None, CT, 2 * LANE), lambda b, p: (b, 0, p)))
        out_shape.append(jax.ShapeDtypeStruct((B, CT, GROUP_W), BF))
    res = pl.pallas_call(
        functools.partial(_gla_kernel, need_ctx=need_ctx, nc_ctx=CT // CHUNK, nc_lat=S // CHUNK),
        grid=(B, 2),
        in_specs=in_specs,
        out_specs=out_specs,
        out_shape=out_shape,
        scratch_shapes=[pltpu.VMEM((2, T, LANE), F32),
                        pltpu.VMEM((2, LANE, 2 * LANE), F32),
                        pltpu.VMEM((T, 2 * LANE), F32),
                        pltpu.VMEM((T, 2 * LANE), F32)],
        compiler_params=_cp(("parallel", "parallel")),
        name="gla",
    )(p_c, p_c, p_c, p_c, ps_c, p_l, p_l, p_l, p_l, ps_l, gup, gb, o_norm.reshape(1, LANE))
    return (res[1] if need_ctx else None), res[0]


def _rope_tables(S):
    pos = np.arange(S)
    half = RET_DK // 2
    quarter = half // 2
    freqs = ROPE_BASE ** (-np.arange(quarter, dtype=np.float64) / quarter)
    cos = np.zeros((S, LANE), np.float64)
    sin_dn = np.zeros((S, LANE), np.float64)
    sin_up = np.zeros((S, LANE), np.float64)
    for head in range(2):
        for part, p in enumerate((pos // GRID_W, pos % GRID_W)):
            ang = p[:, None].astype(np.float64) * freqs[None, :]
            base = head * RET_DK + part * half
            cos[:, base:base + quarter] = np.cos(ang)
            cos[:, base + quarter:base + half] = np.cos(ang)
            sin_dn[:, base:base + quarter] = -np.sin(ang)
            sin_up[:, base + quarter:base + half] = np.sin(ang)
    return tuple(jnp.asarray(t, F32) for t in (cos, sin_dn, sin_up))


def _ret_kernel(qc_ref, kc_ref, vc_ref, gc_ref, ql_ref, kl_ref, vl_ref, gl_ref,
                cos_ref, sdn_ref, sup_ref, dl_ref, gnw_ref, *rest, need_ctx, nc_ctx, nc_lat):
    if need_ctx:
        ol_ref, oc_ref, qr_ref, kr_ref, st_ref, of_ref, ob_ref = rest
    else:
        ol_ref, qr_ref, kr_ref, st_ref, of_ref, ob_ref = rest
        oc_ref = None
    CT = nc_ctx * CHUNK
    S = nc_lat * CHUNK
    hmask, incl2, bmask, _ = _pair_consts()
    zero_v = jnp.zeros((CHUNK, LANE), BF)
    pp = pl.program_id(1)

    def rope(x):
        quarter = RET_DK // 4
        return (x * cos_ref[...] + pltpu.roll(x, LANE - quarter, 1) * sdn_ref[...]
                + pltpu.roll(x, quarter, 1) * sup_ref[...])

    qr_ref[0:CT, :] = qc_ref[...].astype(F32)
    kr_ref[0:CT, :] = kc_ref[...].astype(F32) * (RET_DK ** -0.5)
    qr_ref[CT:CT + S, :] = rope(ql_ref[...].astype(F32))
    kr_ref[CT:CT + S, :] = rope(kl_ref[...].astype(F32) * (RET_DK ** -0.5))
    st_ref[...] = jnp.zeros_like(st_ref)

    dmat, qfac, kfac, gend = {}, {}, {}, {}
    rowf = lax.broadcasted_iota(jnp.int32, (CHUNK, LANE), 0).astype(F32)
    colf = (lax.broadcasted_iota(jnp.int32, (CHUNK, LANE), 1) & (CHUNK - 1)).astype(F32)
    lane_v = lax.broadcasted_iota(jnp.int32, (CHUNK, 2 * LANE), 1)
    row_v = lax.broadcasted_iota(jnp.int32, (CHUNK, 2 * LANE), 0).astype(F32)
    row_k = lax.broadcasted_iota(jnp.int32, (LANE, 1), 0)
    for d in range(2):
        lg_a = _log_sigmoid(dl_ref[pl.ds(d * N_HEADS + 2 * pp, 1), 0:1])
        lg_b = _log_sigmoid(dl_ref[pl.ds(d * N_HEADS + 2 * pp + 1, 1), 0:1])
        lg2 = jnp.where(hmask[0], lg_a, lg_b)
        dist = (rowf - colf) if d == 0 else (colf - rowf)
        dmat[d] = jnp.exp(jnp.where(incl2[d], dist * lg2, -jnp.inf))
        steps_v = (row_v + 1.0) if d == 0 else (CHUNK - row_v)
        qfac[d] = jnp.exp(steps_v * jnp.where(lane_v < LANE, lg_a, lg_b))
        steps_k = (rowf + 1.0) if d == 0 else (CHUNK - rowf)
        kfac[d] = jnp.exp((CHUNK - steps_k) * lg2)
        gend[d] = jnp.exp(CHUNK * jnp.where(row_k < CHUNK, lg_a, lg_b))

    def segment(v_ref, row0, nch, write):
        per_iter = 4 if nch % 4 == 0 else (2 if nch % 2 == 0 else 1)

        def body(it, carry):
            work = []
            for j in range(per_iter):
                n = it * per_iter + j
                for d in range(2):
                    c = n if d == 0 else nch - 1 - n
                    r0 = pl.multiple_of(c * CHUNK, CHUNK)
                    k2 = kr_ref[pl.ds(row0 + r0, CHUNK), :]
                    vp = v_ref[pl.ds(r0, CHUNK), :]
                    qb = km = None
                    if write:
                        qb = qr_ref[pl.ds(row0 + r0, CHUNK), :].astype(BF)
                        km = jnp.concatenate([jnp.where(hmask[0], k2, 0.0), jnp.where(hmask[1], k2, 0.0)],
                                             axis=0).astype(BF)
                    work.append((d, r0, vp, qb, km, (k2 * kfac[d]).astype(BF)))
            if write:
                atts = [(_dot_nt(qb, km) * dmat[d]).astype(BF) for (d, r0, vp, qb, km, ke) in work]
                intras = [_dot(att, jnp.concatenate(
                              [jnp.concatenate([vp[:, 0:LANE], zero_v], axis=1),
                               jnp.concatenate([zero_v, vp[:, LANE:2 * LANE]], axis=1)], axis=0))
                          for att, (d, r0, vp, qb, km, ke) in zip(atts, work)]
            upds = [jnp.where(bmask, _dot_tn(ke, vp), 0.0) for (d, r0, vp, qb, km, ke) in work]
            state = {d: st_ref[d] for d in range(2)}
            for i, (d, r0, vp, qb, km, ke) in enumerate(work):
                st = state[d]
                if write:
                    oref = of_ref if d == 0 else ob_ref
                    oref[pl.ds(row0 + r0, CHUNK), :] = intras[i] + _dot(qb, st.astype(BF)) * qfac[d]
                state[d] = st * gend[d] + upds[i]
            for d, st in state.items():
                st_ref[d] = st
            return carry

        lax.fori_loop(0, nch // per_iter, body, 0)

    segment(vc_ref, 0, nc_ctx, need_ctx)
    segment(vl_ref, CT, nc_lat, True)

    def finish(o, w, g):
        mu = jnp.mean(o, axis=-1, keepdims=True)
        oc = o - mu
        var = jnp.mean(oc * oc, axis=-1, keepdims=True)
        return oc * lax.rsqrt(var + EPS) * w * _silu(g.astype(F32))

    for hh in range(2):
        sl = slice(hh * LANE, (hh + 1) * LANE)
        w = gnw_ref[:, sl]
        o = of_ref[CT:CT + S, sl] + ob_ref[CT:CT + S, sl]
        ol_ref[:, sl] = finish(o, w, gl_ref[:, sl]).astype(ol_ref.dtype)
        if need_ctx:
            o = of_ref[0:CT, sl] + ob_ref[0:CT, sl]
            oc_ref[:, sl] = finish(o, w, gc_ref[:, sl]).astype(oc_ref.dtype)


def _ret(p_l, p_c, decay_logit, gn_w, need_ctx):
    B, S, _ = p_l.shape
    CT = p_c.shape[1]
    T = CT + S
    cos, sdn, sup = _rope_tables(S)
    dl = jnp.broadcast_to(decay_logit.reshape(2 * N_HEADS, 1).astype(F32), (2 * N_HEADS, LANE))

    def spec(n, width, off):
        return pl.BlockSpec((None, n, width), lambda b, p: (b, 0, off(p)))

    in_specs = []
    for n in (CT, S):
        in_specs += [spec(n, LANE, lambda p: RET_Q + p), spec(n, LANE, lambda p: RET_K + p),
                     spec(n, 2 * LANE, lambda p: RET_V // 2 + p), spec(n, 2 * LANE, lambda p: RET_G // 2 + p)]
    tab = pl.BlockSpec((S, LANE), lambda b, p: (0, 0))
    in_specs += [tab, tab, tab,
                 pl.BlockSpec((2 * N_HEADS, LANE), lambda b, p: (0, 0)),
                 pl.BlockSpec((1, 2 * LANE), lambda b, p: (0, p))]
    out_specs = [pl.BlockSpec((None, S, 2 * LANE), lambda b, p: (b, 0, p))]
    out_shape = [jax.ShapeDtypeStruct((B, S, GROUP_W), BF)]
    if need_ctx:
        out_specs.append(pl.BlockSpec((None, CT, 2 * LANE), lambda b, p: (b, 0, p)))
        out_shape.append(jax.ShapeDtypeStruct((B, CT, GROUP_W), BF))
    res = pl.pallas_call(
        functools.partial(_ret_kernel, need_ctx=need_ctx, nc_ctx=CT // CHUNK, nc_lat=S // CHUNK),
        grid=(B, 2),
        in_specs=in_specs,
        out_specs=out_specs,
        out_shape=out_shape,
        scratch_shapes=[pltpu.VMEM((T, LANE), F32),
                        pltpu.VMEM((T, LANE), F32),
                        pltpu.VMEM((2, LANE, 2 * LANE), F32),
                        pltpu.VMEM((T, 2 * LANE), F32),
                        pltpu.VMEM((T, 2 * LANE), F32)],
        compiler_params=_cp(("parallel", "parallel")),
        name="retention",
    )(p_c, p_c, p_c, p_c, p_l, p_l, p_l, p_l, cos, sdn, sup, dl, gn_w.reshape(1, GROUP_W))
    return (res[1] if need_ctx else None), res[0]


def _gdn_kernel(qc_ref, kc_ref, vc_ref, zc_ref, sc_ref, ql_ref, kl_ref, vl_ref, zl_ref, sl_ref,
                cwq_ref, cwk_ref, cwv_ref, alog_ref, dtb_ref, nw_ref, *rest, need_ctx, nc_ctx, nc_lat):
    if need_ctx:
        ol_ref, oc_ref = rest[:2]
        rest = rest[2:]
    else:
        ol_ref = rest[0]
        oc_ref = None
        rest = rest[1:]
    qs, ks, vs, u_ref, wq_ref, a_ref, ke_ref, ge_ref, st_ref, of_ref, ob_ref = rest
    CT = nc_ctx * CHUNK
    S = nc_lat * CHUNK
    pp = pl.program_id(1)
    ri = lax.broadcasted_iota(jnp.int32, (GDN_BLK, GDN_BLK), 0)
    ci = lax.broadcasted_iota(jnp.int32, (GDN_BLK, GDN_BLK), 1)
    same = (ri // CHUNK) == (ci // CHUNK)
    incl = (same & (ri >= ci), same & (ri <= ci))
    strict = (same & (ri > ci), same & (ri < ci))
    tri2_bf = jnp.concatenate([jnp.where(m, 1.0, 0.0).astype(BF) for m in incl], axis=0)
    eye_f = jnp.where(ri == ci, 1.0, 0.0)
    lane = lax.broadcasted_iota(jnp.int32, (GDN_BLK, LANE), 1)
    rowi = lax.broadcasted_iota(jnp.int32, (GDN_BLK, 1), 0)
    lvl_masks = []
    s = 1
    while s < CHUNK:
        lvl_masks.append(jnp.where(((ri // (2 * s)) == (ci // (2 * s))) & ((ri // s) != (ci // s)), 1.0, 0.0))
        s *= 2

    def conv_silu(x_ref, w_ref, n):
        x = x_ref[...].astype(F32)
        row = lax.broadcasted_iota(jnp.int32, x.shape, 0)
        xp = jnp.where(row == 0, 0.0, pltpu.roll(x, 1, 0))
        xn = jnp.where(row == n - 1, 0.0, pltpu.roll(x, n - 1, 0))
        return _silu(xp * w_ref[0:1, :] + x * w_ref[1:2, :] + xn * w_ref[2:3, :])

    def l2n(x):
        return x * lax.rsqrt(jnp.sum(x * x, axis=-1, keepdims=True) + EPS)

    for (q_ref, k_ref, v_ref, r0, n) in ((qc_ref, kc_ref, vc_ref, 0, CT), (ql_ref, kl_ref, vl_ref, CT, S)):
        q = conv_silu(q_ref, cwq_ref, n)
        k = conv_silu(k_ref, cwk_ref, n)
        v = conv_silu(v_ref, cwv_ref, n)
        for hh in range(2):
            sl = slice(hh * LANE, (hh + 1) * LANE)
            qs[r0:r0 + n, sl] = l2n(q[:, sl]) * (GDN_DK ** -0.5)
            ks[r0:r0 + n, sl] = l2n(k[:, sl])
        vs[r0:r0 + n, :] = v.astype(BF)

    neg_a = -jnp.exp(alog_ref[...])
    dtb = dtb_ref[...]

    def phase1(small_ref, row0, nblk):
        per_iter = 4 if nblk % 4 == 0 else (2 if nblk % 2 == 0 else 1)

        def body(it, carry):
            pipes = []
            a_list = []
            for j in range(per_iter):
                m = it * per_iter + j
                r0 = pl.multiple_of(m * GDN_BLK, GDN_BLK)
                g0 = pl.multiple_of(row0 + r0, GDN_BLK)
                sm = small_ref[pl.ds(r0, GDN_BLK), :]
                lg_all = neg_a * _softplus(sm + dtb)
                lb_all = _log_sigmoid(sm)
                cum2 = _exact_dot(tri2_bf, lg_all)
                for hh in range(2):
                    h = 2 * pp + hh
                    sl = slice(hh * LANE, (hh + 1) * LANE)
                    k = ks[pl.ds(g0, GDN_BLK), sl]
                    q = qs[pl.ds(g0, GDN_BLK), sl]
                    v = vs[pl.ds(g0, GDN_BLK), sl]
                    kb = k.astype(BF)
                    kkqk = _dot_nt(jnp.concatenate([kb, q.astype(BF)], axis=0), kb)
                    kk = kkqk[0:GDN_BLK, :]
                    qk = kkqk[GDN_BLK:2 * GDN_BLK, :]
                    for d in range(2):
                        cum_all = cum2[d * GDN_BLK:(d + 1) * GDN_BLK, :]
                        g = jnp.sum(jnp.where(lane == SM_A + d * N_HEADS + h, cum_all, 0.0), axis=-1, keepdims=True)
                        lb = jnp.sum(jnp.where(lane == SM_BT + d * N_HEADS + h, lb_all, 0.0), axis=-1, keepdims=True)
                        ends = (CHUNK - 1, GDN_BLK - 1) if d == 0 else (0, CHUNK)
                        tot_lo = g[ends[0]:ends[0] + 1, :]
                        tot_hi = g[ends[1]:ends[1] + 1, :]
                        tot = jnp.where(rowi < CHUNK, tot_lo, tot_hi)
                        hrow = jnp.broadcast_to(g - lb, (GDN_BLK, GDN_BLK)).T
                        e_in = jnp.exp(jnp.where(incl[d], g - hrow, -jnp.inf))
                        a_list.append(kk * jnp.where(strict[d], e_in, 0.0))
                        pipes.append((m, g0, hh, d, k, q, v, qk, e_in, g, lb, tot, tot_lo, tot_hi))
            a4 = jnp.stack(a_list, axis=0)
            x4 = eye_f[None] - a4 * lvl_masks[0][None]
            bdot = lambda p, r: jnp.einsum('pij,pjk->pik', p, r, preferred_element_type=F32)
            for msk in lvl_masks[1:]:
                t4 = (a4 * msk[None]).astype(BF)
                xb = x4.astype(BF)
                x4 = x4 - bdot(xb, bdot(t4, xb).astype(BF))
            x4b = x4.astype(BF)
            for p, (m, g0, hh, d, k, q, v, qk, e_in, g, lb, tot, tot_lo, tot_hi) in enumerate(pipes):
                gam = jnp.exp(g)
                rhs = jnp.concatenate([v, (k * gam).astype(BF)], axis=1)
                uw = _dot(x4b[p], rhs)
                u_ref[d, hh, pl.ds(g0, GDN_BLK), :] = uw[:, 0:LANE]
                w = uw[:, LANE:2 * LANE].astype(BF)
                qg = (q * gam).astype(BF)
                wq0 = pl.multiple_of(2 * g0, 2 * GDN_BLK)
                wq_ref[d, hh, pl.ds(wq0, 2 * GDN_BLK), :] = jnp.concatenate(
                    [w[0:CHUNK], qg[0:CHUNK], w[CHUNK:GDN_BLK], qg[CHUNK:GDN_BLK]], axis=0)
                a_ref[d, hh, pl.ds(g0, GDN_BLK), :] = (qk * e_in).astype(BF)
                ke_ref[d, hh, pl.ds(g0, GDN_BLK), :] = (k * jnp.exp(tot - g + lb)).astype(BF)
                ge0 = pl.multiple_of((row0 // CHUNK + 2 * m) * 8, 16)
                ge_ref[d, hh, pl.ds(ge0, 16), :] = jnp.concatenate(
                    [jnp.broadcast_to(jnp.exp(tot_lo), (8, LANE)), jnp.broadcast_to(jnp.exp(tot_hi), (8, LANE))], axis=0)
            return carry

        lax.fori_loop(0, nblk // per_iter, body, 0)

    phase1(sc_ref, 0, CT // GDN_BLK)
    phase1(sl_ref, CT, S // GDN_BLK)
    st_ref[...] = jnp.zeros_like(st_ref)

    def phase2(row0, nch, write):
        per_iter = 2 if nch % 2 == 0 else 1

        def body(it, carry):
            state = {(d, hh): st_ref[d, hh] for d in range(2) for hh in range(2)}
            outs = []
            for j in range(per_iter):
                n = it * per_iter + j
                chains = []
                for d in range(2):
                    c = n if d == 0 else nch - 1 - n
                    g0 = pl.multiple_of(row0 + c * CHUNK, CHUNK)
                    for hh in range(2):
                        chains.append((d, hh, c, g0))
                wss = [_dot(wq_ref[d, hh, pl.ds(pl.multiple_of(2 * g0, 2 * CHUNK), 2 * CHUNK), :],
                            state[d, hh].astype(BF)) for (d, hh, c, g0) in chains]
                dbs = [(u_ref[d, hh, pl.ds(g0, CHUNK), :] - ws[0:CHUNK, :]).astype(BF)
                       for (d, hh, c, g0), ws in zip(chains, wss)]
                upd = [_dot_tn(ke_ref[d, hh, pl.ds(g0, CHUNK), :], db) for (d, hh, c, g0), db in zip(chains, dbs)]
                for (d, hh, c, g0), up in zip(chains, upd):
                    ge = ge_ref[d, hh, pl.ds(pl.multiple_of((row0 // CHUNK + c) * 8, 8), 1), :]
                    state[d, hh] = state[d, hh] * ge[:, 0:1] + up
                outs.append((chains, wss, dbs))
            for (d, hh), st in state.items():
                st_ref[d, hh] = st
            if write:
                for chains, wss, dbs in outs:
                    for (d, hh, c, g0), ws, db in zip(chains, wss, dbs):
                        o = ws[CHUNK:2 * CHUNK, :] + _dot(a_ref[d, hh, pl.ds(g0, CHUNK), :],
                                                          jnp.concatenate([db, db], axis=0))
                        oref = of_ref if d == 0 else ob_ref
                        oref[pl.ds(g0, CHUNK), hh * LANE:(hh + 1) * LANE] = o
            return carry

        lax.fori_loop(0, nch // per_iter, body, 0)

    phase2(0, nc_ctx, need_ctx)
    phase2(CT, nc_lat, True)

    nw = nw_ref[...]
    for hh in range(2):
        sl = slice(hh * LANE, (hh + 1) * LANE)
        o = of_ref[CT:CT + S, sl] + ob_ref[CT:CT + S, sl]
        ol_ref[:, sl] = _finish_rms(o, nw, zl_ref[:, sl]).astype(ol_ref.dtype)
        if need_ctx:
            o = of_ref[0:CT, sl] + ob_ref[0:CT, sl]
            oc_ref[:, sl] = _finish_rms(o, nw, zc_ref[:, sl]).astype(oc_ref.dtype)


def _gdn(p_l, ps_l, p_c, ps_c, conv_w, a_log, dt_bias, o_norm, need_ctx):
    B, S, _ = p_l.shape
    CT = p_c.shape[1]
    T = CT + S
    nch = T // CHUNK
    alog = jnp.zeros((1, N_SMALL), F32).at[0, SM_A:SM_A + 2 * N_HEADS].set(a_log.reshape(-1).astype(F32))
    dtb = jnp.zeros((1, N_SMALL), F32).at[0, SM_A:SM_A + 2 * N_HEADS].set(dt_bias.reshape(-1).astype(F32))

    def spec(n, width, off):
        return pl.BlockSpec((None, n, width), lambda b, p: (b, 0, off(p)))

    in_specs = []
    for n in (CT, S):
        in_specs += [spec(n, 2 * LANE, lambda p: GDN_Q // 2 + p), spec(n, 2 * LANE, lambda p: GDN_K // 2 + p),
                     spec(n, 2 * LANE, lambda p: GDN_V // 2 + p), spec(n, 2 * LANE, lambda p: GDN_Z // 2 + p),
                     spec(n, N_SMALL, lambda p: 0)]
    cw = lambda part: pl.BlockSpec((3, 2 * LANE), lambda b, p: (0, 2 * part + p))
    vec = pl.BlockSpec((1, LANE), lambda b, p: (0, 0))
    in_specs += [cw(0), cw(1), cw(2), vec, vec, vec]
    out_specs = [pl.BlockSpec((None, S, 2 * LANE), lambda b, p: (b, 0, p))]
    out_shape = [jax.ShapeDtypeStruct((B, S, GROUP_W), BF)]
    if need_ctx:
        out_specs.append(pl.BlockSpec((None, CT, 2 * LANE), lambda b, p: (b, 0, p)))
        out_shape.append(jax.ShapeDtypeStruct((B, CT, GROUP_W), BF))
    res = pl.pallas_call(
        functools.partial(_gdn_kernel, need_ctx=need_ctx, nc_ctx=CT // CHUNK, nc_lat=S // CHUNK),
        grid=(B, 2),
        in_specs=in_specs,
        out_specs=out_specs,
        out_shape=out_shape,
        scratch_shapes=[pltpu.VMEM((T, 2 * LANE), F32),
                        pltpu.VMEM((T, 2 * LANE), F32),
                        pltpu.VMEM((T, 2 * LANE), BF),
                        pltpu.VMEM((2, 2, T, LANE), F32),
                        pltpu.VMEM((2, 2, 2 * T, LANE), BF),
                        pltpu.VMEM((2, 2, T, GDN_BLK), BF),
                        pltpu.VMEM((2, 2, T, LANE), BF),
                        pltpu.VMEM((2, 2, nch * 8, LANE), F32),
                        pltpu.VMEM((2, 2, LANE, LANE), F32),
                        pltpu.VMEM((T, 2 * LANE), F32),
                        pltpu.VMEM((T, 2 * LANE), F32)],
        compiler_params=_cp(("parallel", "parallel")),
        name="gdn",
    )(p_c, p_c, p_c, p_c, ps_c, p_l, p_l, p_l, p_l, ps_l,
      conv_w, conv_w, conv_w, alog, dtb, o_norm.reshape(1, LANE))
    return (res[1] if need_ctx else None), res[0]


def _align_w_in(w):
    D = w.shape[0]
    big = jnp.concatenate([w[:, 0:3072], w[:, 3104:5152], w[:, 5168:6704]], axis=1)
    small = jnp.concatenate([w[:, 3072:3104], w[:, 5152:5168], jnp.zeros((D, N_SMALL - 48), w.dtype)], axis=1)
    return big.astype(BF), small.astype(BF)


def _pick(n, prefs):
    for p in prefs:
        if n % p == 0:
            return p
    return n


def kernel(x, c, ctx, c_ctx, ada_w, ada_b, norm1_w, norm2_w, w_in, w_out, na_q_norm, na_k_norm, na_rpb,
           gla_gate_up, gla_gate_b, gla_o_norm, gdn_conv_w, gdn_a_log, gdn_dt_bias, gdn_o_norm,
           ret_decay_logit, ret_gn_w, mlp_w1, mlp_w2):
    B, S, D = x.shape
    CT = ctx.shape[1]
    depth = ada_w.shape[0]
    R = ((B + 1 + 7) // 8) * 8
    cc = jnp.concatenate([c, c_ctx[None, :], jnp.zeros((R - B - 1, D), F32)], axis=0)
    mod_all = _ada(cc, ada_w, ada_b).reshape(depth, R, 6, D)

    tm_l = _pick(S, (1024, 512, 256))
    tm_c = _pick(B * CT, (1024, 512, 256))
    tn = _pick(N_BIG, (1664, 512, 256, 128))
    tm_o = _pick(S, (512, 256))
    tm_oc = _pick(B * CT, (512, 256))
    th = _pick(mlp_w1.shape[2], (1024, 512, 256))

    xl = x
    xc = ctx.reshape(1, B * CT, D)
    for layer in range(depth):
        need_ctx = layer < depth - 1
        mod = mod_all[layer]
        w_big, w_small = _align_w_in(w_in[layer])
        nw1 = norm1_w[layer].reshape(1, D)
        nw2 = norm2_w[layer].reshape(1, D)
        wo = w_out[layer].astype(BF)
        w1 = mlp_w1[layer].astype(BF)
        w2 = mlp_w2[layer].astype(BF)

        p_l, ps_l = _in_proj(xl, mod, None, nw1, w_big, w_small, tm_l, tn)
        p_c, ps_c = _in_proj(xc, mod, B, nw1, w_big, w_small, tm_c, tn)
        p_c = p_c.reshape(B, CT, N_BIG)
        ps_c = ps_c.reshape(B, CT, N_SMALL)

        na_c, na_l = _natten(p_l, p_c, na_q_norm[layer], na_k_norm[layer], na_rpb[layer], need_ctx)
        gl_c, gl_l = _gla(p_l, ps_l, p_c, ps_c, gla_gate_up[layer], gla_gate_b[layer], gla_o_norm[layer], need_ctx)
        gd_c, gd_l = _gdn(p_l, ps_l, p_c, ps_c, gdn_conv_w[layer], gdn_a_log[layer], gdn_dt_bias[layer],
                          gdn_o_norm[layer], need_ctx)
        rt_c, rt_l = _ret(p_l, p_c, ret_decay_logit[layer], ret_gn_w[layer], need_ctx)

        xl = _out_proj(xl, (na_l, gl_l, gd_l, rt_l), wo, mod, None, tm_o)
        xl = _mlp(xl, mod, None, nw2, w1, w2, tm_o, th)
        if need_ctx:
            ys = tuple(t.reshape(1, B * CT, GROUP_W) for t in (na_c, gl_c, gd_c, rt_c))
            xc = _out_proj(xc, ys, wo, mod, B, tm_oc)
            xc = _mlp(xc, mod, B, nw2, w1, w2, tm_oc, th)
    return xl
```

```python
import functools

import numpy as np
import jax
import jax.numpy as jnp
from jax import lax
from jax.experimental import pallas as pl
from jax.experimental.pallas import tpu as pltpu

BF = jnp.bfloat16
F32 = jnp.float32

N_HEADS = 4
HEAD_DIM = 128
GROUP_W = N_HEADS * HEAD_DIM
GRID_W = 64
NA_WIN_ROWS = 8
NA_WIN_COLS = 16
NA_GROUP = 4
GLA_DK = 64
GLA_GATE_RANK = 16
GLA_GATE_TAU = 16.0
GDN_DK = 128
RET_DK = 64
ROPE_BASE = 10000.0
CHUNK = 64
GDN_BLK = 2 * CHUNK
EPS = 1e-6
NEG_INF = -1e30

LANE = 128
N_BIG = 52 * LANE
N_SMALL = LANE
NA_Q, NA_K, NA_V = 0, 4, 8
GLA_Q, GLA_K, GLA_V, GLA_G = 12, 14, 16, 20
GDN_Q, GDN_K, GDN_V, GDN_Z = 24, 28, 32, 36
RET_Q, RET_K, RET_V, RET_G = 40, 42, 44, 48
SM_RK, SM_A, SM_BT = 0, 32, 40

VMEM_LIMIT = 56 * 1024 * 1024


def _cp(sem, vmem=VMEM_LIMIT):
    return pltpu.CompilerParams(dimension_semantics=sem, vmem_limit_bytes=vmem)


def _dot(a, b):
    return jnp.dot(a, b, preferred_element_type=F32)


def _dot_nt(a, b):
    return lax.dot_general(a, b, (((1,), (1,)), ((), ())), preferred_element_type=F32)


def _dot_tn(a, b):
    return lax.dot_general(a, b, (((0,), (0,)), ((), ())), preferred_element_type=F32)


def _split3(x):
    hi = x.astype(BF)
    r1 = x - hi.astype(F32)
    mid = r1.astype(BF)
    lo = (r1 - mid.astype(F32)).astype(BF)
    return hi, mid, lo


def _exact_dot(m_bf, x):
    hi, mid, lo = _split3(x)
    n = x.shape[1]
    r = _dot(m_bf, jnp.concatenate([hi, mid, lo], axis=1))
    return r[:, 0:n] + r[:, n:2 * n] + r[:, 2 * n:3 * n]


def _sigmoid(x):
    return 1.0 / (1.0 + jnp.exp(-x))


def _silu(x):
    return x * _sigmoid(x)


def _log_sigmoid(x):
    return jnp.minimum(x, 0.0) - jnp.log(1.0 + jnp.exp(-jnp.abs(x)))


def _softplus(x):
    return jnp.maximum(x, 0.0) + jnp.log(1.0 + jnp.exp(-jnp.abs(x)))


def _ln_mod(x, nw, shift, scale):
    ms = jnp.mean(x * x, axis=-1, keepdims=True)
    return (x * lax.rsqrt(ms + EPS) * nw) * (1.0 + scale) + shift


def _tri_consts():
    ri = lax.broadcasted_iota(jnp.int32, (CHUNK, CHUNK), 0)
    ci = lax.broadcasted_iota(jnp.int32, (CHUNK, CHUNK), 1)
    incl = (ri >= ci, ri <= ci)
    strict = (ri > ci, ri < ci)
    tri_bf = tuple(jnp.where(m, 1.0, 0.0).astype(BF) for m in incl)
    return ri, ci, incl, strict, tri_bf


def _ada_kernel(c_ref, w_ref, b_ref, o_ref):
    sc = _silu(c_ref[...]).astype(BF)
    o_ref[...] = _dot(sc, w_ref[...].astype(BF)) + b_ref[...]


def _ada(cc, ada_w, ada_b):
    L, D, N6 = ada_w.shape
    R = cc.shape[0]
    tn = 1024 if N6 % 1024 == 0 else N6
    return pl.pallas_call(
        _ada_kernel,
        grid=(L, N6 // tn),
        in_specs=[pl.BlockSpec((R, D), lambda l, j: (0, 0)),
                  pl.BlockSpec((None, D, tn), lambda l, j: (l, 0, j)),
                  pl.BlockSpec((None, 1, tn), lambda l, j: (l, 0, j))],
        out_specs=pl.BlockSpec((None, R, tn), lambda l, j: (l, 0, j)),
        out_shape=jax.ShapeDtypeStruct((L, R, N6), F32),
        compiler_params=_cp(("parallel", "parallel")),
        name="ada_ln",
    )(cc, ada_w, ada_b.reshape(L, 1, N6))


def _inproj_kernel(x_ref, mod_ref, nw_ref, w_ref, ws_ref, o_ref, os_ref, h_ref):
    @pl.when(pl.program_id(1) == 0)
    def _():
        h = _ln_mod(x_ref[...], nw_ref[...], mod_ref[0:1, :], mod_ref[1:2, :]).astype(BF)
        h_ref[...] = h
        os_ref[...] = _dot(h, ws_ref[...])

    o_ref[...] = _dot(h_ref[...], w_ref[...]).astype(o_ref.dtype)


def _in_proj(x3, mod, const_row, nw, w_big, w_small, tm, tn):
    Bn, Tn, D = x3.shape
    nt = Tn // tm
    if const_row is None:
        mod_map = lambda i, j: (i // nt, 0, 0)
    else:
        mod_map = lambda i, j: (const_row, 0, 0)
    return pl.pallas_call(
        _inproj_kernel,
        grid=(Bn * nt, N_BIG // tn),
        in_specs=[pl.BlockSpec((None, tm, D), lambda i, j: (i // nt, i % nt, 0)),
                  pl.BlockSpec((None, 6, D), mod_map),
                  pl.BlockSpec((1, D), lambda i, j: (0, 0)),
                  pl.BlockSpec((D, tn), lambda i, j: (0, j)),
                  pl.BlockSpec((D, N_SMALL), lambda i, j: (0, 0))],
        out_specs=[pl.BlockSpec((None, tm, tn), lambda i, j: (i // nt, i % nt, j)),
                   pl.BlockSpec((None, tm, N_SMALL), lambda i, j: (i // nt, i % nt, 0))],
        out_shape=[jax.ShapeDtypeStruct((Bn, Tn, N_BIG), BF),
                   jax.ShapeDtypeStruct((Bn, Tn, N_SMALL), F32)],
        scratch_shapes=[pltpu.VMEM((tm, D), BF)],
        compiler_params=_cp(("parallel", "arbitrary")),
        name="in_proj",
    )(x3, mod, nw, w_big, w_small)


def _outproj_kernel(x_ref, y0, y1, y2, y3, w_ref, mod_ref, nw_ref, o_ref, h_ref):
    acc = _dot(y0[...], w_ref[0 * GROUP_W:1 * GROUP_W, :])
    acc += _dot(y1[...], w_ref[1 * GROUP_W:2 * GROUP_W, :])
    acc += _dot(y2[...], w_ref[2 * GROUP_W:3 * GROUP_W, :])
    acc += _dot(y3[...], w_ref[3 * GROUP_W:4 * GROUP_W, :])
    xn = x_ref[...] + mod_ref[2:3, :] * acc
    o_ref[...] = xn
    h_ref[...] = _ln_mod(xn, nw_ref[...], mod_ref[3:4, :], mod_ref[4:5, :]).astype(BF)


def _out_proj(x3, ys, w_out, mod, const_row, nw2, tm):
    Bn, Tn, D = x3.shape
    nt = Tn // tm
    if const_row is None:
        mod_map = lambda i: (i // nt, 0, 0)
    else:
        mod_map = lambda i: (const_row, 0, 0)
    row_map = lambda i: (i // nt, i % nt, 0)
    return pl.pallas_call(
        _outproj_kernel,
        grid=(Bn * nt,),
        in_specs=[pl.BlockSpec((None, tm, D), row_map)]
                 + [pl.BlockSpec((None, tm, GROUP_W), row_map)] * 4
                 + [pl.BlockSpec((4 * GROUP_W, D), lambda i: (0, 0)),
                    pl.BlockSpec((None, 6, D), mod_map),
                    pl.BlockSpec((1, D), lambda i: (0, 0))],
        out_specs=[pl.BlockSpec((None, tm, D), row_map), pl.BlockSpec((None, tm, D), row_map)],
        out_shape=[jax.ShapeDtypeStruct((Bn, Tn, D), F32), jax.ShapeDtypeStruct((Bn, Tn, D), BF)],
        compiler_params=_cp(("parallel",)),
        name="out_proj",
    )(x3, *ys, w_out, mod, nw2)


def _mlp_kernel(x_ref, h_ref, mod_ref, w1_ref, w2_ref, o_ref, *, nk):
    k = pl.program_id(1)

    @pl.when(k == 0)
    def _():
        o_ref[...] = jnp.zeros_like(o_ref)

    hid = jnp.maximum(_dot(h_ref[...], w1_ref[...]), 0.0)
    o_ref[...] += _dot((hid * hid).astype(BF), w2_ref[...])

    @pl.when(k == nk - 1)
    def _():
        o_ref[...] = x_ref[...] + mod_ref[5:6, :] * o_ref[...]


def _mlp(x3, h3, mod, const_row, w1, w2, tm, th):
    Bn, Tn, D = x3.shape
    Hd = w1.shape[1]
    nt = Tn // tm
    nk = Hd // th
    if const_row is None:
        mod_map = lambda i, k: (i // nt, 0, 0)
    else:
        mod_map = lambda i, k: (const_row, 0, 0)
    row_map = lambda i, k: (i // nt, i % nt, 0)
    return pl.pallas_call(
        functools.partial(_mlp_kernel, nk=nk),
        grid=(Bn * nt, nk),
        in_specs=[pl.BlockSpec((None, tm, D), row_map),
                  pl.BlockSpec((None, tm, D), row_map),
                  pl.BlockSpec((None, 6, D), mod_map),
                  pl.BlockSpec((D, th), lambda i, k: (0, k)),
                  pl.BlockSpec((th, D), lambda i, k: (k, 0))],
        out_specs=pl.BlockSpec((None, tm, D), row_map),
        out_shape=jax.ShapeDtypeStruct((Bn, Tn, D), F32),
        compiler_params=_cp(("parallel", "arbitrary")),
        name="mlp",
    )(x3, h3, mod, w1, w2)


def _rms_head(x, w):
    x = x.astype(F32)
    return x * lax.rsqrt(jnp.mean(x * x, axis=-1, keepdims=True) + EPS) * w


def _natten_kernel(geo_ref, ql_ref, kl_ref, vl_ref, qc_ref, kc_ref, vc_ref, qw_ref, kw_ref, bias_ref, *rest,
                   need_ctx, rows, grp, span):
    if need_ctx:
        ol_ref, oc_ref, qs, ks = rest
    else:
        ol_ref, qs, ks = rest
    scale = HEAD_DIM ** -0.5
    qw = qw_ref[...]
    kw = kw_ref[...]
    qs[...] = (_rms_head(ql_ref[...], qw) * scale).astype(BF)
    ks[...] = _rms_head(kl_ref[...], kw).astype(BF)
    kc = _rms_head(kc_ref[...], kw).astype(BF)
    vc = vc_ref[...]
    if need_ctx:
        qc = (_rms_head(qc_ref[...], qw) * scale).astype(BF)
        s = _dot_nt(qc, kc)
        p = jnp.exp(s - jnp.max(s, axis=-1, keepdims=True))
        l = jnp.sum(p, axis=-1, keepdims=True)
        oc_ref[...] = (_dot(p.astype(BF), vc) / l).astype(oc_ref.dtype)

    nq = grp * GRID_W
    nk = span * GRID_W

    ngroups = rows // grp
    per_iter = 2 if ngroups % 2 == 0 else 1

    def body(it, carry):
        geo = []
        for j in range(per_iter):
            g = it * per_iter + j
            geo.append((pl.multiple_of(g * nq, nq), pl.multiple_of(geo_ref[0, g] * GRID_W, GRID_W), geo_ref[1, g]))
        qv = [qs[pl.ds(q0, nq), :] for (q0, k0, t) in geo]
        sws = [_dot_nt(q, ks[pl.ds(k0, nk), :]) + bias_ref[t] for q, (q0, k0, t) in zip(qv, geo)]
        scs = [_dot_nt(q, kc) for q in qv]
        probs = []
        for sw, sc in zip(sws, scs):
            m = jnp.maximum(jnp.max(sw, axis=-1, keepdims=True), jnp.max(sc, axis=-1, keepdims=True))
            pw = jnp.exp(sw - m)
            pc = jnp.exp(sc - m)
            l = jnp.sum(pw, axis=-1, keepdims=True) + jnp.sum(pc, axis=-1, keepdims=True)
            probs.append((pw.astype(BF), pc.astype(BF), l))
        for (pw, pc, l), (q0, k0, t) in zip(probs, geo):
            o = (_dot(pw, vl_ref[pl.ds(k0, nk), :]) + _dot(pc, vc)) / l
            ol_ref[pl.ds(q0, nq), :] = o.astype(ol_ref.dtype)
        return carry

    lax.fori_loop(0, ngroups // per_iter, body, 0)


def _natten_geometry(rows, kh):
    grp = NA_GROUP if rows % NA_GROUP == 0 else 1
    span = min(kh + grp - 1, rows)
    starts, type_ids, types = [], [], []
    for g in range(rows // grp):
        rs = [int(np.clip(g * grp + j - kh // 2, 0, rows - kh)) for j in range(grp)]
        us = int(np.clip(rs[0], 0, rows - span))
        sig = tuple((rs[j] - us, g * grp + j - rs[j]) for j in range(grp))
        assert all(0 <= off <= span - kh for off, _ in sig)
        if sig not in types:
            types.append(sig)
        starts.append(us)
        type_ids.append(types.index(sig))
    return grp, span, np.asarray([starts, type_ids], np.int32), types


def _natten_bias(rpb, kh, span, types):
    q = np.arange(GRID_W)[:, None]
    kc = np.arange(GRID_W)[None, :]
    col_off = np.clip(kc - q, -(NA_WIN_COLS - 1), NA_WIN_COLS - 1) + NA_WIN_COLS - 1
    onehot = (col_off[..., None] == np.arange(2 * NA_WIN_COLS - 1)).astype(np.float32)
    toe = jnp.einsum('lhrc,qkc->lhrqk', rpb.astype(F32), jnp.asarray(onehot), precision=lax.Precision.HIGHEST)
    cs = np.clip(q - NA_WIN_COLS // 2, 0, GRID_W - NA_WIN_COLS)
    valid = (kc >= cs) & (kc < cs + NA_WIN_COLS)
    toe = jnp.where(valid[None, None, None], toe, NEG_INF)
    L, H = rpb.shape[:2]
    pad = jnp.full((L, H, span, GRID_W, GRID_W), NEG_INF, F32)
    toe = jnp.concatenate([pad, toe, pad], axis=2)
    slabs, keep = [], []
    for sig in types:
        for off, d in sig:
            start = span + NA_WIN_ROWS - 1 - d - off
            slabs.append(toe[:, :, start:start + span])
            keep.append([0 <= i - off < kh for i in range(span)])
    slab = jnp.stack(slabs, axis=2)
    slab = jnp.where(np.asarray(keep)[None, None, :, :, None, None], slab, NEG_INF)
    slab = slab.transpose(0, 1, 2, 4, 3, 5)
    grp = len(types[0])
    return slab.reshape(L, H, len(types), grp * GRID_W, span * GRID_W)


def _natten_tables(rpb_all, S):
    rows = S // GRID_W
    kh = min(NA_WIN_ROWS, rows)
    grp, span, geo, types = _natten_geometry(rows, kh)
    return grp, span, geo, _natten_bias(rpb_all, kh, span, types)


def _natten(p_l, p_c, qw, kw, tables, layer, need_ctx):
    B, S, _ = p_l.shape
    CT = p_c.shape[1]
    rows = S // GRID_W
    grp, span, geo, bias_all = tables
    n_types = bias_all.shape[2]
    lat = lambda off: pl.BlockSpec((None, S, LANE), lambda b, h: (b, 0, off + h))
    ctx = lambda off: pl.BlockSpec((None, CT, LANE), lambda b, h: (b, 0, off + h))
    vec = pl.BlockSpec((1, LANE), lambda b, h: (0, 0))
    out_specs = [pl.BlockSpec((None, S, LANE), lambda b, h: (b, 0, h))]
    out_shape = [jax.ShapeDtypeStruct((B, S, GROUP_W), BF)]
    if need_ctx:
        out_specs.append(pl.BlockSpec((None, CT, LANE), lambda b, h: (b, 0, h)))
        out_shape.append(jax.ShapeDtypeStruct((B, CT, GROUP_W), BF))
    res = pl.pallas_call(
        functools.partial(_natten_kernel, need_ctx=need_ctx, rows=rows, grp=grp, span=span),
        grid=(B, N_HEADS),
        in_specs=[pl.BlockSpec(memory_space=pltpu.SMEM),
                  lat(NA_Q), lat(NA_K), lat(NA_V), ctx(NA_Q), ctx(NA_K), ctx(NA_V), vec, vec,
                  pl.BlockSpec((None, None, n_types, grp * GRID_W, span * GRID_W),
                               lambda b, h: (layer, h, 0, 0, 0))],
        out_specs=out_specs,
        out_shape=out_shape,
        scratch_shapes=[pltpu.VMEM((S, LANE), BF), pltpu.VMEM((S, LANE), BF)],
        compiler_params=_cp(("parallel", "parallel")),
        name="natten",
    )(jnp.asarray(geo), p_l, p_l, p_l, p_c, p_c, p_c, qw.reshape(1, LANE), kw.reshape(1, LANE), bias_all)
    return (res[1] if need_ctx else None), res[0]


def _pair_consts():
    lane = lax.broadcasted_iota(jnp.int32, (CHUNK, LANE), 1)
    row = lax.broadcasted_iota(jnp.int32, (CHUNK, LANE), 0)
    col = lane & (CHUNK - 1)
    hmask = (lane < CHUNK, lane >= CHUNK)
    incl2 = (row >= col, row <= col)
    r2 = lax.broadcasted_iota(jnp.int32, (LANE, 2 * LANE), 0)
    c2 = lax.broadcasted_iota(jnp.int32, (LANE, 2 * LANE), 1)
    bmask = (r2 < CHUNK) == (c2 < LANE)
    eye = lax.broadcasted_iota(jnp.int32, (LANE, LANE), 0) == lax.broadcasted_iota(jnp.int32, (LANE, LANE), 1)
    return hmask, incl2, bmask, eye


def _finish_rms(o, nw, g):
    y = o * lax.rsqrt(jnp.mean(o * o, axis=-1, keepdims=True) + EPS) * nw
    return y * _silu(g.astype(F32))


def _gla_kernel(qc_ref, kc_ref, vc_ref, gc_ref, sc_ref, ql_ref, kl_ref, vl_ref, gl_ref, sl_ref,
                gup_ref, gb_ref, nw_ref, *rest, need_ctx, nc_ctx, nc_lat):
    if need_ctx:
        ol_ref, oc_ref, lg_ref, st_ref, of_ref, ob_ref = rest
    else:
        ol_ref, lg_ref, st_ref, of_ref, ob_ref = rest
        oc_ref = None
    CT = nc_ctx * CHUNK
    S = nc_lat * CHUNK
    _, _, _, _, tri_bf = _tri_consts()
    hmask, incl2, bmask, eye = _pair_consts()
    zero_v = jnp.zeros((CHUNK, LANE), BF)

    for d in range(2):
        for (s_ref, r0, n) in ((sc_ref, 0, CT), (sl_ref, CT, S)):
            z = _dot(s_ref[...].astype(BF), gup_ref[d]) + gb_ref[d]
            lg_ref[d, r0:r0 + n, :] = _log_sigmoid(z) * (1.0 / GLA_GATE_TAU)
    st_ref[...] = jnp.zeros_like(st_ref)

    def segment(q_ref, k_ref, v_ref, row0, nch, write):
        per_iter = 4 if nch % 4 == 0 else (2 if nch % 2 == 0 else 1)

        def body(it, carry):
            pre = []
            for j in range(per_iter):
                n = it * per_iter + j
                for d in range(2):
                    c = n if d == 0 else nch - 1 - n
                    r0 = pl.multiple_of(c * CHUNK, CHUNK)
                    pre.append((d, r0, q_ref[pl.ds(r0, CHUNK), :].astype(F32), k_ref[pl.ds(r0, CHUNK), :].astype(F32)))
            cums = [_exact_dot(tri_bf[d], lg_ref[d, pl.ds(row0 + r0, CHUNK), :]) for (d, r0, _, _) in pre]
            work = []
            for (d, r0, q2, k2), cum in zip(pre, cums):
                tot = cum[CHUNK - 1:CHUNK, :] if d == 0 else cum[0:1, :]
                ge_col = jnp.sum(jnp.where(eye, jnp.exp(tot), 0.0), axis=1, keepdims=True)
                ke = (k2 * jnp.exp(tot - cum)).astype(BF)
                vp = v_ref[pl.ds(r0, CHUNK), :]
                qd = kdm = None
                if write:
                    qd = (q2 * (jnp.exp(cum) * (GLA_DK ** -0.5))).astype(BF)
                    kd = k2 * jnp.exp(-cum)
                    kdm = jnp.concatenate([jnp.where(hmask[0], kd, 0.0), jnp.where(hmask[1], kd, 0.0)],
                                          axis=0).astype(BF)
                work.append((d, r0, vp, qd, kdm, ke, ge_col))
            if write:
                atts = [jnp.where(incl2[d], _dot_nt(qd, kdm), 0.0).astype(BF)
                        for (d, r0, vp, qd, kdm, ke, ge_col) in work]
                intras = [_dot(att, jnp.concatenate(
                              [jnp.concatenate([vp[:, 0:LANE], zero_v], axis=1),
                               jnp.concatenate([zero_v, vp[:, LANE:2 * LANE]], axis=1)], axis=0))
                          for att, (d, r0, vp, qd, kdm, ke, ge_col) in zip(atts, work)]
            upds = [jnp.where(bmask, _dot_tn(ke, vp), 0.0) for (d, r0, vp, qd, kdm, ke, ge_col) in work]
            state = {d: st_ref[d] for d in range(2)}
            for i, (d, r0, vp, qd, kdm, ke, ge_col) in enumerate(work):
                st = state[d]
                if write:
                    oref = of_ref if d == 0 else ob_ref
                    oref[pl.ds(row0 + r0, CHUNK), :] = intras[i] + _dot(qd, st.astype(BF))
                state[d] = st * ge_col + upds[i]
            for d, st in state.items():
                st_ref[d] = st
            return carry

        lax.fori_loop(0, nch // per_iter, body, 0)

    segment(qc_ref, kc_ref, vc_ref, 0, nc_ctx, need_ctx)
    segment(ql_ref, kl_ref, vl_ref, CT, nc_lat, True)

    nw = nw_ref[...]
    for hh in range(2):
        sl = slice(hh * LANE, (hh + 1) * LANE)
        o = of_ref[CT:CT + S, sl] + ob_ref[CT:CT + S, sl]
        ol_ref[:, sl] = _finish_rms(o, nw, gl_ref[:, sl]).astype(ol_ref.dtype)
        if need_ctx:
            o = of_ref[0:CT, sl] + ob_ref[0:CT, sl]
            oc_ref[:, sl] = _finish_rms(o, nw, gc_ref[:, sl]).astype(oc_ref.dtype)


def _gla(p_l, ps_l, p_c, ps_c, gate_up, gate_b, o_norm, need_ctx):
    B, S, _ = p_l.shape
    CT = p_c.shape[1]
    T = CT + S
    gup = jnp.zeros((2, N_SMALL, N_HEADS * GLA_DK), F32)
    for d in range(2):
        gup = gup.at[d, SM_RK + d * GLA_GATE_RANK:SM_RK + (d + 1) * GLA_GATE_RANK].set(gate_up[d])
    gup = gup.astype(BF)
    gb = gate_b.reshape(2, 1, N_HEADS * GLA_DK)

    def spec(n, width, off):
        return pl.BlockSpec((None, n, width), lambda b, p: (b, 0, off(p)))

    in_specs = []
    for n in (CT, S):
        in_specs += [spec(n, LANE, lambda p: GLA_Q + p), spec(n, LANE, lambda p: GLA_K + p),
                     spec(n, 2 * LANE, lambda p: GLA_V // 2 + p), spec(n, 2 * LANE, lambda p: GLA_G // 2 + p),
                     spec(n, N_SMALL, lambda p: 0)]
    in_specs += [pl.BlockSpec((2, N_SMALL, LANE), lambda b, p: (0, 0, p)),
                 pl.BlockSpec((2, 1, LANE), lambda b, p: (0, 0, p)),
                 pl.BlockSpec((1, LANE), lambda b, p: (0, 0))]
    out_specs = [pl.BlockSpec((None, S, 2 * LANE), lambda b, p: (b, 0, p))]
    out_shape = [jax.ShapeDtypeStruct((B, S, GROUP_W), BF)]
    if need_ctx:
        out_specs.append(pl.BlockSpec((None, CT, 2 * LANE), lambda b, p: (b, 0, p)))
        out_shape.append(jax.ShapeDtypeStruct((B, CT, GROUP_W), BF))
    res = pl.pallas_call(
        functools.partial(_gla_kernel, need_ctx=need_ctx, nc_ctx=CT // CHUNK, nc_lat=S // CHUNK),
        grid=(B, 2),
        in_specs=in_specs,
        out_specs=out_specs,
        out_shape=out_shape,
        scratch_shapes=[pltpu.VMEM((2, T, LANE), F32),
                        pltpu.VMEM((2, LANE, 2 * LANE), F32),
                        pltpu.VMEM((T, 2 * LANE), F32),
                        pltpu.VMEM((T, 2 * LANE), F32)],
        compiler_params=_cp(("parallel", "parallel")),
        name="gla",
    )(p_c, p_c, p_c, p_c, ps_c, p_l, p_l, p_l, p_l, ps_l, gup, gb, o_norm.reshape(1, LANE))
    return (res[1] if need_ctx else None), res[0]


def _rope_tables(S):
    pos = np.arange(S)
    half = RET_DK // 2
    quarter = half // 2
    freqs = ROPE_BASE ** (-np.arange(quarter, dtype=np.float64) / quarter)
    cos = np.zeros((S, LANE), np.float64)
    sin_dn = np.zeros((S, LANE), np.float64)
    sin_up = np.zeros((S, LANE), np.float64)
    for head in range(2):
        for part, p in enumerate((pos // GRID_W, pos % GRID_W)):
            ang = p[:, None].astype(np.float64) * freqs[None, :]
            base = head * RET_DK + part * half
            cos[:, base:base + quarter] = np.cos(ang)
            cos[:, base + quarter:base + half] = np.cos(ang)
            sin_dn[:, base:base + quarter] = -np.sin(ang)
            sin_up[:, base + quarter:base + half] = np.sin(ang)
    return tuple(jnp.asarray(t, F32) for t in (cos, sin_dn, sin_up))


def _ret_kernel(qc_ref, kc_ref, vc_ref, gc_ref, ql_ref, kl_ref, vl_ref, gl_ref,
                cos_ref, sdn_ref, sup_ref, dl_ref, gnw_ref, *rest, need_ctx, nc_ctx, nc_lat):
    if need_ctx:
        ol_ref, oc_ref, qr_ref, kr_ref, st_ref, of_ref, ob_ref = rest
    else:
        ol_ref, qr_ref, kr_ref, st_ref, of_ref, ob_ref = rest
        oc_ref = None
    CT = nc_ctx * CHUNK
    S = nc_lat * CHUNK
    hmask, incl2, bmask, _ = _pair_consts()
    zero_v = jnp.zeros((CHUNK, LANE), BF)
    pp = pl.program_id(1)

    def rope(x):
        quarter = RET_DK // 4
        return (x * cos_ref[...] + pltpu.roll(x, LANE - quarter, 1) * sdn_ref[...]
                + pltpu.roll(x, quarter, 1) * sup_ref[...])

    qr_ref[0:CT, :] = qc_ref[...].astype(F32)
    kr_ref[0:CT, :] = kc_ref[...].astype(F32) * (RET_DK ** -0.5)
    qr_ref[CT:CT + S, :] = rope(ql_ref[...].astype(F32))
    kr_ref[CT:CT + S, :] = rope(kl_ref[...].astype(F32) * (RET_DK ** -0.5))
    st_ref[...] = jnp.zeros_like(st_ref)

    dmat, qfac, kfac, gend = {}, {}, {}, {}
    rowf = lax.broadcasted_iota(jnp.int32, (CHUNK, LANE), 0).astype(F32)
    colf = (lax.broadcasted_iota(jnp.int32, (CHUNK, LANE), 1) & (CHUNK - 1)).astype(F32)
    lane_v = lax.broadcasted_iota(jnp.int32, (CHUNK, 2 * LANE), 1)
    row_v = lax.broadcasted_iota(jnp.int32, (CHUNK, 2 * LANE), 0).astype(F32)
    row_k = lax.broadcasted_iota(jnp.int32, (LANE, 1), 0)
    for d in range(2):
        lg_a = _log_sigmoid(dl_ref[pl.ds(d * N_HEADS + 2 * pp, 1), 0:1])
        lg_b = _log_sigmoid(dl_ref[pl.ds(d * N_HEADS + 2 * pp + 1, 1), 0:1])
        lg2 = jnp.where(hmask[0], lg_a, lg_b)
        dist = (rowf - colf) if d == 0 else (colf - rowf)
        dmat[d] = jnp.exp(jnp.where(incl2[d], dist * lg2, -jnp.inf))
        steps_v = (row_v + 1.0) if d == 0 else (CHUNK - row_v)
        qfac[d] = jnp.exp(steps_v * jnp.where(lane_v < LANE, lg_a, lg_b))
        steps_k = (rowf + 1.0) if d == 0 else (CHUNK - rowf)
        kfac[d] = jnp.exp((CHUNK - steps_k) * lg2)
        gend[d] = jnp.exp(CHUNK * jnp.where(row_k < CHUNK, lg_a, lg_b))

    def segment(v_ref, row0, nch, write):
        per_iter = 4 if nch % 4 == 0 else (2 if nch % 2 == 0 else 1)

        def body(it, carry):
            work = []
            for j in range(per_iter):
                n = it * per_iter + j
                for d in range(2):
                    c = n if d == 0 else nch - 1 - n
                    r0 = pl.multiple_of(c * CHUNK, CHUNK)
                    k2 = kr_ref[pl.ds(row0 + r0, CHUNK), :]
                    vp = v_ref[pl.ds(r0, CHUNK), :]
                    qb = km = None
                    if write:
                        qb = qr_ref[pl.ds(row0 + r0, CHUNK), :].astype(BF)
                        km = jnp.concatenate([jnp.where(hmask[0], k2, 0.0), jnp.where(hmask[1], k2, 0.0)],
                                             axis=0).astype(BF)
                    work.append((d, r0, vp, qb, km, (k2 * kfac[d]).astype(BF)))
            if write:
                atts = [(_dot_nt(qb, km) * dmat[d]).astype(BF) for (d, r0, vp, qb, km, ke) in work]
                intras = [_dot(att, jnp.concatenate(
                              [jnp.concatenate([vp[:, 0:LANE], zero_v], axis=1),
                               jnp.concatenate([zero_v, vp[:, LANE:2 * LANE]], axis=1)], axis=0))
                          for att, (d, r0, vp, qb, km, ke) in zip(atts, work)]
            upds = [jnp.where(bmask, _dot_tn(ke, vp), 0.0) for (d, r0, vp, qb, km, ke) in work]
            state = {d: st_ref[d] for d in range(2)}
            for i, (d, r0, vp, qb, km, ke) in enumerate(work):
                st = state[d]
                if write:
                    oref = of_ref if d == 0 else ob_ref
                    oref[pl.ds(row0 + r0, CHUNK), :] = intras[i] + _dot(qb, st.astype(BF)) * qfac[d]
                state[d] = st * gend[d] + upds[i]
            for d, st in state.items():
                st_ref[d] = st
            return carry

        lax.fori_loop(0, nch // per_iter, body, 0)

    segment(vc_ref, 0, nc_ctx, need_ctx)
    segment(vl_ref, CT, nc_lat, True)

    def finish(o, w, g):
        mu = jnp.mean(o, axis=-1, keepdims=True)
        oc = o - mu
        var = jnp.mean(oc * oc, axis=-1, keepdims=True)
        return oc * lax.rsqrt(var + EPS) * w * _silu(g.astype(F32))

    for hh in range(2):
        sl = slice(hh * LANE, (hh + 1) * LANE)
        w = gnw_ref[:, sl]
        o = of_ref[CT:CT + S, sl] + ob_ref[CT:CT + S, sl]
        ol_ref[:, sl] = finish(o, w, gl_ref[:, sl]).astype(ol_ref.dtype)
        if need_ctx:
            o = of_ref[0:CT, sl] + ob_ref[0:CT, sl]
            oc_ref[:, sl] = finish(o, w, gc_ref[:, sl]).astype(oc_ref.dtype)


def _ret(p_l, p_c, decay_logit, gn_w, need_ctx):
    B, S, _ = p_l.shape
    CT = p_c.shape[1]
    T = CT + S
    cos, sdn, sup = _rope_tables(S)
    dl = jnp.broadcast_to(decay_logit.reshape(2 * N_HEADS, 1).astype(F32), (2 * N_HEADS, LANE))

    def spec(n, width, off):
        return pl.BlockSpec((None, n, width), lambda b, p: (b, 0, off(p)))

    in_specs = []
    for n in (CT, S):
        in_specs += [spec(n, LANE, lambda p: RET_Q + p), spec(n, LANE, lambda p: RET_K + p),
                     spec(n, 2 * LANE, lambda p: RET_V // 2 + p), spec(n, 2 * LANE, lambda p: RET_G // 2 + p)]
    tab = pl.BlockSpec((S, LANE), lambda b, p: (0, 0))
    in_specs += [tab, tab, tab,
                 pl.BlockSpec((2 * N_HEADS, LANE), lambda b, p: (0, 0)),
                 pl.BlockSpec((1, 2 * LANE), lambda b, p: (0, p))]
    out_specs = [pl.BlockSpec((None, S, 2 * LANE), lambda b, p: (b, 0, p))]
    out_shape = [jax.ShapeDtypeStruct((B, S, GROUP_W), BF)]
    if need_ctx:
        out_specs.append(pl.BlockSpec((None, CT, 2 * LANE), lambda b, p: (b, 0, p)))
        out_shape.append(jax.ShapeDtypeStruct((B, CT, GROUP_W), BF))
    res = pl.pallas_call(
        functools.partial(_ret_kernel, need_ctx=need_ctx, nc_ctx=CT // CHUNK, nc_lat=S // CHUNK),
        grid=(B, 2),
        in_specs=in_specs,
        out_specs=out_specs,
        out_shape=out_shape,
        scratch_shapes=[pltpu.VMEM((T, LANE), F32),
                        pltpu.VMEM((T, LANE), F32),
                        pltpu.VMEM((2, LANE, 2 * LANE), F32),
                        pltpu.VMEM((T, 2 * LANE), F32),
                        pltpu.VMEM((T, 2 * LANE), F32)],
        compiler_params=_cp(("parallel", "parallel")),
        name="retention",
    )(p_c, p_c, p_c, p_c, p_l, p_l, p_l, p_l, cos, sdn, sup, dl, gn_w.reshape(1, GROUP_W))
    return (res[1] if need_ctx else None), res[0]


def _gdn_kernel(qc_ref, kc_ref, vc_ref, zc_ref, sc_ref, ql_ref, kl_ref, vl_ref, zl_ref, sl_ref,
                cwq_ref, cwk_ref, cwv_ref, alog_ref, dtb_ref, nw_ref, *rest, need_ctx, nc_ctx, nc_lat):
    if need_ctx:
        ol_ref, oc_ref = rest[:2]
        rest = rest[2:]
    else:
        ol_ref = rest[0]
        oc_ref = None
        rest = rest[1:]
    qs, ks, vs, u_ref, wq_ref, a_ref, ke_ref, ge_ref, st_ref, of_ref, ob_ref = rest
    CT = nc_ctx * CHUNK
    S = nc_lat * CHUNK
    pp = pl.program_id(1)
    ri = lax.broadcasted_iota(jnp.int32, (GDN_BLK, GDN_BLK), 0)
    ci = lax.broadcasted_iota(jnp.int32, (GDN_BLK, GDN_BLK), 1)
    same = (ri // CHUNK) == (ci // CHUNK)
    incl = (same & (ri >= ci), same & (ri <= ci))
    strict = (same & (ri > ci), same & (ri < ci))
    tri2_bf = jnp.concatenate([jnp.where(m, 1.0, 0.0).astype(BF) for m in incl], axis=0)
    eye_f = jnp.where(ri == ci, 1.0, 0.0)
    lane = lax.broadcasted_iota(jnp.int32, (GDN_BLK, LANE), 1)
    rowi = lax.broadcasted_iota(jnp.int32, (GDN_BLK, 1), 0)
    lvl_masks = []
    s = 1
    while s < CHUNK:
        lvl_masks.append(jnp.where(((ri // (2 * s)) == (ci // (2 * s))) & ((ri // s) != (ci // s)), 1.0, 0.0))
        s *= 2

    def conv_silu(x_ref, w_ref, n):
        x = x_ref[...].astype(F32)
        row = lax.broadcasted_iota(jnp.int32, x.shape, 0)
        xp = jnp.where(row == 0, 0.0, pltpu.roll(x, 1, 0))
        xn = jnp.where(row == n - 1, 0.0, pltpu.roll(x, n - 1, 0))
        return _silu(xp * w_ref[0:1, :] + x * w_ref[1:2, :] + xn * w_ref[2:3, :])

    def l2n(x):
        return x * lax.rsqrt(jnp.sum(x * x, axis=-1, keepdims=True) + EPS)

    for (q_ref, k_ref, v_ref, r0, n) in ((qc_ref, kc_ref, vc_ref, 0, CT), (ql_ref, kl_ref, vl_ref, CT, S)):
        q = conv_silu(q_ref, cwq_ref, n)
        k = conv_silu(k_ref, cwk_ref, n)
        v = conv_silu(v_ref, cwv_ref, n)
        for hh in range(2):
            sl = slice(hh * LANE, (hh + 1) * LANE)
            qs[r0:r0 + n, sl] = l2n(q[:, sl]) * (GDN_DK ** -0.5)
            ks[r0:r0 + n, sl] = l2n(k[:, sl])
        vs[r0:r0 + n, :] = v.astype(BF)

    neg_a = -jnp.exp(alog_ref[...])
    dtb = dtb_ref[...]

    def phase1(small_ref, row0, nblk):
        per_iter = 4 if nblk % 4 == 0 else (2 if nblk % 2 == 0 else 1)

        def body(it, carry):
            pipes = []
            a_list = []
            for j in range(per_iter):
                m = it * per_iter + j
                r0 = pl.multiple_of(m * GDN_BLK, GDN_BLK)
                g0 = pl.multiple_of(row0 + r0, GDN_BLK)
                sm = small_ref[pl.ds(r0, GDN_BLK), :]
                lg_all = neg_a * _softplus(sm + dtb)
                lb_all = _log_sigmoid(sm)
                cum2 = _exact_dot(tri2_bf, lg_all)
                for hh in range(2):
                    h = 2 * pp + hh
                    sl = slice(hh * LANE, (hh + 1) * LANE)
                    k = ks[pl.ds(g0, GDN_BLK), sl]
                    q = qs[pl.ds(g0, GDN_BLK), sl]
                    v = vs[pl.ds(g0, GDN_BLK), sl]
                    kb = k.astype(BF)
                    kkqk = _dot_nt(jnp.concatenate([kb, q.astype(BF)], axis=0), kb)
                    kk = kkqk[0:GDN_BLK, :]
                    qk = kkqk[GDN_BLK:2 * GDN_BLK, :]
                    for d in range(2):
                        cum_all = cum2[d * GDN_BLK:(d + 1) * GDN_BLK, :]
                        g = jnp.sum(jnp.where(lane == SM_A + d * N_HEADS + h, cum_all, 0.0), axis=-1, keepdims=True)
                        lb = jnp.sum(jnp.where(lane == SM_BT + d * N_HEADS + h, lb_all, 0.0), axis=-1, keepdims=True)
                        ends = (CHUNK - 1, GDN_BLK - 1) if d == 0 else (0, CHUNK)
                        tot_lo = g[ends[0]:ends[0] + 1, :]
                        tot_hi = g[ends[1]:ends[1] + 1, :]
                        tot = jnp.where(rowi < CHUNK, tot_lo, tot_hi)
                        hrow = jnp.broadcast_to(g - lb, (GDN_BLK, GDN_BLK)).T
                        e_in = jnp.exp(jnp.where(incl[d], g - hrow, -jnp.inf))
                        a_list.append(kk * jnp.where(strict[d], e_in, 0.0))
                        pipes.append((m, g0, hh, d, k, q, v, qk, e_in, g, lb, tot, tot_lo, tot_hi))
            a4 = jnp.stack(a_list, axis=0)
            x4 = eye_f[None] - a4 * lvl_masks[0][None]
            bdot = lambda p, r: jnp.einsum('pij,pjk->pik', p, r, preferred_element_type=F32)
            for msk in lvl_masks[1:]:
                t4 = (a4 * msk[None]).astype(BF)
                xb = x4.astype(BF)
                x4 = x4 - bdot(xb, bdot(t4, xb).astype(BF))
            x4b = x4.astype(BF)
            for p, (m, g0, hh, d, k, q, v, qk, e_in, g, lb, tot, tot_lo, tot_hi) in enumerate(pipes):
                gam = jnp.exp(g)
                rhs = jnp.concatenate([v, (k * gam).astype(BF)], axis=1)
                uw = _dot(x4b[p], rhs)
                u_ref[d, hh, pl.ds(g0, GDN_BLK), :] = uw[:, 0:LANE]
                w = uw[:, LANE:2 * LANE].astype(BF)
                qg = (q * gam).astype(BF)
                wq0 = pl.multiple_of(2 * g0, 2 * GDN_BLK)
                wq_ref[d, hh, pl.ds(wq0, 2 * GDN_BLK), :] = jnp.concatenate(
                    [w[0:CHUNK], qg[0:CHUNK], w[CHUNK:GDN_BLK], qg[CHUNK:GDN_BLK]], axis=0)
                a_ref[d, hh, pl.ds(g0, GDN_BLK), :] = (qk * e_in).astype(BF)
                ke_ref[d, hh, pl.ds(g0, GDN_BLK), :] = (k * jnp.exp(tot - g + lb)).astype(BF)
                ge0 = pl.multiple_of((row0 // CHUNK + 2 * m) * 8, 16)
                ge_ref[d, hh, pl.ds(ge0, 16), :] = jnp.concatenate(
                    [jnp.broadcast_to(jnp.exp(tot_lo), (8, LANE)), jnp.broadcast_to(jnp.exp(tot_hi), (8, LANE))], axis=0)
            return carry

        lax.fori_loop(0, nblk // per_iter, body, 0)

    phase1(sc_ref, 0, CT // GDN_BLK)
    phase1(sl_ref, CT, S // GDN_BLK)
    st_ref[...] = jnp.zeros_like(st_ref)

    def phase2(row0, nch, write):
        per_iter = 2 if nch % 2 == 0 else 1

        def body(it, carry):
            state = {(d, hh): st_ref[d, hh] for d in range(2) for hh in range(2)}
            outs = []
            for j in range(per_iter):
                n = it * per_iter + j
                chains = []
                for d in range(2):
                    c = n if d == 0 else nch - 1 - n
                    g0 = pl.multiple_of(row0 + c * CHUNK, CHUNK)
                    for hh in range(2):
                        chains.append((d, hh, c, g0))
                wss = [_dot(wq_ref[d, hh, pl.ds(pl.multiple_of(2 * g0, 2 * CHUNK), 2 * CHUNK), :],
                            state[d, hh].astype(BF)) for (d, hh, c, g0) in chains]
                dbs = [(u_ref[d, hh, pl.ds(g0, CHUNK), :] - ws[0:CHUNK, :]).astype(BF)
                       for (d, hh, c, g0), ws in zip(chains, wss)]
                upd = [_dot_tn(ke_ref[d, hh, pl.ds(g0, CHUNK), :], db) for (d, hh, c, g0), db in zip(chains, dbs)]
                for (d, hh, c, g0), up in zip(chains, upd):
                    ge = ge_ref[d, hh, pl.ds(pl.multiple_of((row0 // CHUNK + c) * 8, 8), 1), :]
                    state[d, hh] = state[d, hh] * ge[:, 0:1] + up
                outs.append((chains, wss, dbs))
            for (d, hh), st in state.items():
                st_ref[d, hh] = st
            if write:
                for chains, wss, dbs in outs:
                    for (d, hh, c, g0), ws, db in zip(chains, wss, dbs):
                        o = ws[CHUNK:2 * CHUNK, :] + _dot(a_ref[d, hh, pl.ds(g0, CHUNK), :],
                                                          jnp.concatenate([db, db], axis=0))
                        oref = of_ref if d == 0 else ob_ref
                        oref[pl.ds(g0, CHUNK), hh * LANE:(hh + 1) * LANE] = o
            return carry

        lax.fori_loop(0, nch // per_iter, body, 0)

    phase2(0, nc_ctx, need_ctx)
    phase2(CT, nc_lat, True)

    nw = nw_ref[...]
    for hh in range(2):
        sl = slice(hh * LANE, (hh + 1) * LANE)
        o = of_ref[CT:CT + S, sl] + ob_ref[CT:CT + S, sl]
        ol_ref[:, sl] = _finish_rms(o, nw, zl_ref[:, sl]).astype(ol_ref.dtype)
        if need_ctx:
            o = of_ref[0:CT, sl] + ob_ref[0:CT, sl]
            oc_ref[:, sl] = _finish_rms(o, nw, zc_ref[:, sl]).astype(oc_ref.dtype)


def _gdn(p_l, ps_l, p_c, ps_c, conv_w, a_log, dt_bias, o_norm, need_ctx):
    B, S, _ = p_l.shape
    CT = p_c.shape[1]
    T = CT + S
    nch = T // CHUNK
    alog = jnp.zeros((1, N_SMALL), F32).at[0, SM_A:SM_A + 2 * N_HEADS].set(a_log.reshape(-1).astype(F32))
    dtb = jnp.zeros((1, N_SMALL), F32).at[0, SM_A:SM_A + 2 * N_HEADS].set(dt_bias.reshape(-1).astype(F32))

    def spec(n, width, off):
        return pl.BlockSpec((None, n, width), lambda b, p: (b, 0, off(p)))

    in_specs = []
    for n in (CT, S):
        in_specs += [spec(n, 2 * LANE, lambda p: GDN_Q // 2 + p), spec(n, 2 * LANE, lambda p: GDN_K // 2 + p),
                     spec(n, 2 * LANE, lambda p: GDN_V // 2 + p), spec(n, 2 * LANE, lambda p: GDN_Z // 2 + p),
                     spec(n, N_SMALL, lambda p: 0)]
    cw = lambda part: pl.BlockSpec((3, 2 * LANE), lambda b, p: (0, 2 * part + p))
    vec = pl.BlockSpec((1, LANE), lambda b, p: (0, 0))
    in_specs += [cw(0), cw(1), cw(2), vec, vec, vec]
    out_specs = [pl.BlockSpec((None, S, 2 * LANE), lambda b, p: (b, 0, p))]
    out_shape = [jax.ShapeDtypeStruct((B, S, GROUP_W), BF)]
    if need_ctx:
        out_specs.append(pl.BlockSpec((None, CT, 2 * LANE), lambda b, p: (b, 0, p)))
        out_shape.append(jax.ShapeDtypeStruct((B, CT, GROUP_W), BF))
    res = pl.pallas_call(
        functools.partial(_gdn_kernel, need_ctx=need_ctx, nc_ctx=CT // CHUNK, nc_lat=S // CHUNK),
        grid=(B, 2),
        in_specs=in_specs,
        out_specs=out_specs,
        out_shape=out_shape,
        scratch_shapes=[pltpu.VMEM((T, 2 * LANE), F32),
                        pltpu.VMEM((T, 2 * LANE), F32),
                        pltpu.VMEM((T, 2 * LANE), BF),
                        pltpu.VMEM((2, 2, T, LANE), F32),
                        pltpu.VMEM((2, 2, 2 * T, LANE), BF),
                        pltpu.VMEM((2, 2, T, GDN_BLK), BF),
                        pltpu.VMEM((2, 2, T, LANE), BF),
                        pltpu.VMEM((2, 2, nch * 8, LANE), F32),
                        pltpu.VMEM((2, 2, LANE, LANE), F32),
                        pltpu.VMEM((T, 2 * LANE), F32),
                        pltpu.VMEM((T, 2 * LANE), F32)],
        compiler_params=_cp(("parallel", "parallel")),
        name="gdn",
    )(p_c, p_c, p_c, p_c, ps_c, p_l, p_l, p_l, p_l, ps_l,
      conv_w, conv_w, conv_w, alog, dtb, o_norm.reshape(1, LANE))
    return (res[1] if need_ctx else None), res[0]


def _align_w_in(w):
    D = w.shape[0]
    big = jnp.concatenate([w[:, 0:3072], w[:, 3104:5152], w[:, 5168:6704]], axis=1)
    small = jnp.concatenate([w[:, 3072:3104], w[:, 5152:5168], jnp.zeros((D, N_SMALL - 48), w.dtype)], axis=1)
    return big.astype(BF), small.astype(BF)


def _pick(n, prefs):
    for p in prefs:
        if n % p == 0:
            return p
    return n


def kernel(x, c, ctx, c_ctx, ada_w, ada_b, norm1_w, norm2_w, w_in, w_out, na_q_norm, na_k_norm, na_rpb,
           gla_gate_up, gla_gate_b, gla_o_norm, gdn_conv_w, gdn_a_log, gdn_dt_bias, gdn_o_norm,
           ret_decay_logit, ret_gn_w, mlp_w1, mlp_w2):
    B, S, D = x.shape
    CT = ctx.shape[1]
    depth = ada_w.shape[0]
    R = ((B + 1 + 7) // 8) * 8
    cc = jnp.concatenate([c, c_ctx[None, :], jnp.zeros((R - B - 1, D), F32)], axis=0)
    mod_all = _ada(cc, ada_w, ada_b).reshape(depth, R, 6, D)
    na_tables = _natten_tables(na_rpb, S)

    tm_l = _pick(S, (1024, 512, 256))
    tm_c = _pick(B * CT, (1024, 512, 256))
    tn = _pick(N_BIG, (1664, 512, 256, 128))
    tm_o = _pick(S, (512, 256))
    tm_oc = _pick(B * CT, (512, 256))
    th = _pick(mlp_w1.shape[2], (1024, 512, 256))

    xl = x
    xc = ctx.reshape(1, B * CT, D)
    for layer in range(depth):
        need_ctx = layer < depth - 1
        mod = mod_all[layer]
        w_big, w_small = _align_w_in(w_in[layer])
        nw1 = norm1_w[layer].reshape(1, D)
        nw2 = norm2_w[layer].reshape(1, D)
        wo = w_out[layer].astype(BF)
        w1 = mlp_w1[layer].astype(BF)
        w2 = mlp_w2[layer].astype(BF)

        p_l, ps_l = _in_proj(xl, mod, None, nw1, w_big, w_small, tm_l, tn)
        p_c, ps_c = _in_proj(xc, mod, B, nw1, w_big, w_small, tm_c, tn)
        p_c = p_c.reshape(B, CT, N_BIG)
        ps_c = ps_c.reshape(B, CT, N_SMALL)

        na_c, na_l = _natten(p_l, p_c, na_q_norm[layer], na_k_norm[layer], na_tables, layer, need_ctx)
        gl_c, gl_l = _gla(p_l, ps_l, p_c, ps_c, gla_gate_up[layer], gla_gate_b[layer], gla_o_norm[layer], need_ctx)
        gd_c, gd_l = _gdn(p_l, ps_l, p_c, ps_c, gdn_conv_w[layer], gdn_a_log[layer], gdn_dt_bias[layer],
                          gdn_o_norm[layer], need_ctx)
        rt_c, rt_l = _ret(p_l, p_c, ret_decay_logit[layer], ret_gn_w[layer], need_ctx)

        xl, hl = _out_proj(xl, (na_l, gl_l, gd_l, rt_l), wo, mod, None, nw2, tm_o)
        xl = _mlp(xl, hl, mod, None, w1, w2, tm_o, th)
        if need_ctx:
            ys = tuple(t.reshape(1, B * CT, GROUP_W) for t in (na_c, gl_c, gd_c, rt_c))
            xc, hc = _out_proj(xc, ys, wo, mod, B, nw2, tm_oc)
            xc = _mlp(xc, hc, mod, B, w1, w2, tm_oc, th)
    return xl
```

```python
import functools

import numpy as np
import jax
import jax.numpy as jnp
from jax import lax
from jax.experimental import pallas as pl
from jax.experimental.pallas import tpu as pltpu

BF = jnp.bfloat16
F32 = jnp.float32

N_HEADS = 4
HEAD_DIM = 128
GROUP_W = N_HEADS * HEAD_DIM
GRID_W = 64
NA_WIN_ROWS = 8
NA_WIN_COLS = 16
NA_GROUP = 4
GLA_DK = 64
GLA_GATE_RANK = 16
GLA_GATE_TAU = 16.0
GDN_DK = 128
RET_DK = 64
ROPE_BASE = 10000.0
CHUNK = 64
GDN_BLK = 2 * CHUNK
EPS = 1e-6
NEG_INF = -1e30

LANE = 128
N_BIG = 52 * LANE
N_SMALL = LANE
NA_Q, NA_K, NA_V = 0, 4, 8
GLA_Q, GLA_K, GLA_V, GLA_G = 12, 14, 16, 20
GDN_Q, GDN_K, GDN_V, GDN_Z = 24, 28, 32, 36
RET_Q, RET_K, RET_V, RET_G = 40, 42, 44, 48
SM_RK, SM_A, SM_BT = 0, 32, 40

VMEM_LIMIT = 56 * 1024 * 1024


def _cp(sem, vmem=VMEM_LIMIT):
    return pltpu.CompilerParams(dimension_semantics=sem, vmem_limit_bytes=vmem)


def _aligned(x, m):
    return x if isinstance(x, int) else pl.multiple_of(x, m)


def _dot(a, b):
    return jnp.dot(a, b, preferred_element_type=F32)


def _dot_nt(a, b):
    return lax.dot_general(a, b, (((1,), (1,)), ((), ())), preferred_element_type=F32)


def _dot_tn(a, b):
    return lax.dot_general(a, b, (((0,), (0,)), ((), ())), preferred_element_type=F32)


def _split3(x):
    hi = x.astype(BF)
    r1 = x - hi.astype(F32)
    mid = r1.astype(BF)
    lo = (r1 - mid.astype(F32)).astype(BF)
    return hi, mid, lo


def _exact_dot(m_bf, x):
    hi, mid, lo = _split3(x)
    n = x.shape[1]
    r = _dot(m_bf, jnp.concatenate([hi, mid, lo], axis=1))
    return r[:, 0:n] + r[:, n:2 * n] + r[:, 2 * n:3 * n]


def _sigmoid(x):
    return 1.0 / (1.0 + jnp.exp(-x))


def _silu(x):
    return x * _sigmoid(x)


def _log_sigmoid(x):
    return jnp.minimum(x, 0.0) - jnp.log(1.0 + jnp.exp(-jnp.abs(x)))


def _softplus(x):
    return jnp.maximum(x, 0.0) + jnp.log(1.0 + jnp.exp(-jnp.abs(x)))


def _ln_mod(x, nw, shift, scale):
    ms = jnp.mean(x * x, axis=-1, keepdims=True)
    return (x * lax.rsqrt(ms + EPS) * nw) * (1.0 + scale) + shift


def _tri_consts():
    ri = lax.broadcasted_iota(jnp.int32, (CHUNK, CHUNK), 0)
    ci = lax.broadcasted_iota(jnp.int32, (CHUNK, CHUNK), 1)
    incl = (ri >= ci, ri <= ci)
    strict = (ri > ci, ri < ci)
    tri_bf = tuple(jnp.where(m, 1.0, 0.0).astype(BF) for m in incl)
    return ri, ci, incl, strict, tri_bf


def _ada_kernel(c_ref, w_ref, b_ref, o_ref):
    sc = _silu(c_ref[...]).astype(BF)
    o_ref[...] = _dot(sc, w_ref[...].astype(BF)) + b_ref[...]


def _ada(cc, ada_w, ada_b):
    L, D, N6 = ada_w.shape
    R = cc.shape[0]
    tn = 1024 if N6 % 1024 == 0 else N6
    return pl.pallas_call(
        _ada_kernel,
        grid=(L, N6 // tn),
        in_specs=[pl.BlockSpec((R, D), lambda l, j: (0, 0)),
                  pl.BlockSpec((None, D, tn), lambda l, j: (l, 0, j)),
                  pl.BlockSpec((None, 1, tn), lambda l, j: (l, 0, j))],
        out_specs=pl.BlockSpec((None, R, tn), lambda l, j: (l, 0, j)),
        out_shape=jax.ShapeDtypeStruct((L, R, N6), F32),
        compiler_params=_cp(("parallel", "parallel")),
        name="ada_ln",
    )(cc, ada_w, ada_b.reshape(L, 1, N6))


def _inproj_kernel(x_ref, mod_ref, nw_ref, w_ref, ws_ref, o_ref, os_ref, h_ref):
    @pl.when(pl.program_id(1) == 0)
    def _():
        h = _ln_mod(x_ref[...], nw_ref[...], mod_ref[0:1, :], mod_ref[1:2, :]).astype(BF)
        h_ref[...] = h
        os_ref[...] = _dot(h, ws_ref[...])

    o_ref[...] = _dot(h_ref[...], w_ref[...]).astype(o_ref.dtype)


def _in_proj(x3, mod, const_row, nw, w_big, w_small, tm, tn):
    Bn, Tn, D = x3.shape
    nt = Tn // tm
    if const_row is None:
        mod_map = lambda i, j: (i // nt, 0, 0)
    else:
        mod_map = lambda i, j: (const_row, 0, 0)
    return pl.pallas_call(
        _inproj_kernel,
        grid=(Bn * nt, N_BIG // tn),
        in_specs=[pl.BlockSpec((None, tm, D), lambda i, j: (i // nt, i % nt, 0)),
                  pl.BlockSpec((None, 6, D), mod_map),
                  pl.BlockSpec((1, D), lambda i, j: (0, 0)),
                  pl.BlockSpec((D, tn), lambda i, j: (0, j)),
                  pl.BlockSpec((D, N_SMALL), lambda i, j: (0, 0))],
        out_specs=[pl.BlockSpec((None, tm, tn), lambda i, j: (i // nt, i % nt, j)),
                   pl.BlockSpec((None, tm, N_SMALL), lambda i, j: (i // nt, i % nt, 0))],
        out_shape=[jax.ShapeDtypeStruct((Bn, Tn, N_BIG), BF),
                   jax.ShapeDtypeStruct((Bn, Tn, N_SMALL), F32)],
        scratch_shapes=[pltpu.VMEM((tm, D), BF)],
        compiler_params=_cp(("parallel", "arbitrary")),
        name="in_proj",
    )(x3, mod, nw, w_big, w_small)


def _outproj_kernel(x_ref, y0, y1, y2, y3, w_ref, mod_ref, nw_ref, o_ref, h_ref):
    acc = _dot(y0[...], w_ref[0 * GROUP_W:1 * GROUP_W, :])
    acc += _dot(y1[...], w_ref[1 * GROUP_W:2 * GROUP_W, :])
    acc += _dot(y2[...], w_ref[2 * GROUP_W:3 * GROUP_W, :])
    acc += _dot(y3[...], w_ref[3 * GROUP_W:4 * GROUP_W, :])
    xn = x_ref[...] + mod_ref[2:3, :] * acc
    o_ref[...] = xn
    h_ref[...] = _ln_mod(xn, nw_ref[...], mod_ref[3:4, :], mod_ref[4:5, :]).astype(BF)


def _out_proj(x3, ys, w_out, mod, const_row, nw2, tm):
    Bn, Tn, D = x3.shape
    nt = Tn // tm
    if const_row is None:
        mod_map = lambda i: (i // nt, 0, 0)
    else:
        mod_map = lambda i: (const_row, 0, 0)
    row_map = lambda i: (i // nt, i % nt, 0)
    return pl.pallas_call(
        _outproj_kernel,
        grid=(Bn * nt,),
        in_specs=[pl.BlockSpec((None, tm, D), row_map)]
                 + [pl.BlockSpec((None, tm, GROUP_W), row_map)] * 4
                 + [pl.BlockSpec((4 * GROUP_W, D), lambda i: (0, 0)),
                    pl.BlockSpec((None, 6, D), mod_map),
                    pl.BlockSpec((1, D), lambda i: (0, 0))],
        out_specs=[pl.BlockSpec((None, tm, D), row_map), pl.BlockSpec((None, tm, D), row_map)],
        out_shape=[jax.ShapeDtypeStruct((Bn, Tn, D), F32), jax.ShapeDtypeStruct((Bn, Tn, D), BF)],
        compiler_params=_cp(("parallel",)),
        name="out_proj",
    )(x3, *ys, w_out, mod, nw2)


def _mlp_kernel(x_ref, h_ref, mod_ref, w1_ref, w2_ref, o_ref, *, nk):
    k = pl.program_id(1)

    @pl.when(k == 0)
    def _():
        o_ref[...] = jnp.zeros_like(o_ref)

    hid = jnp.maximum(_dot(h_ref[...], w1_ref[...]), 0.0)
    o_ref[...] += _dot((hid * hid).astype(BF), w2_ref[...])

    @pl.when(k == nk - 1)
    def _():
        o_ref[...] = x_ref[...] + mod_ref[5:6, :] * o_ref[...]


def _mlp(x3, h3, mod, const_row, w1, w2, tm, th):
    Bn, Tn, D = x3.shape
    Hd = w1.shape[1]
    nt = Tn // tm
    nk = Hd // th
    if const_row is None:
        mod_map = lambda i, k: (i // nt, 0, 0)
    else:
        mod_map = lambda i, k: (const_row, 0, 0)
    row_map = lambda i, k: (i // nt, i % nt, 0)
    return pl.pallas_call(
        functools.partial(_mlp_kernel, nk=nk),
        grid=(Bn * nt, nk),
        in_specs=[pl.BlockSpec((None, tm, D), row_map),
                  pl.BlockSpec((None, tm, D), row_map),
                  pl.BlockSpec((None, 6, D), mod_map),
                  pl.BlockSpec((D, th), lambda i, k: (0, k)),
                  pl.BlockSpec((th, D), lambda i, k: (k, 0))],
        out_specs=pl.BlockSpec((None, tm, D), row_map),
        out_shape=jax.ShapeDtypeStruct((Bn, Tn, D), F32),
        compiler_params=_cp(("parallel", "arbitrary")),
        name="mlp",
    )(x3, h3, mod, w1, w2)


def _rms_head(x, w):
    x = x.astype(F32)
    return x * lax.rsqrt(jnp.mean(x * x, axis=-1, keepdims=True) + EPS) * w


def _natten_kernel(geo_ref, ql_ref, kl_ref, vl_ref, qc_ref, kc_ref, vc_ref, qw_ref, kw_ref, bias_ref, *rest,
                   need_ctx, rows, grp, span):
    if need_ctx:
        ol_ref, oc_ref, qs, ks = rest
    else:
        ol_ref, qs, ks = rest
    scale = HEAD_DIM ** -0.5
    qw = qw_ref[...]
    kw = kw_ref[...]
    qs[...] = (_rms_head(ql_ref[...], qw) * scale).astype(BF)
    ks[...] = _rms_head(kl_ref[...], kw).astype(BF)
    kc = _rms_head(kc_ref[...], kw).astype(BF)
    vc = vc_ref[...]
    if need_ctx:
        qc = (_rms_head(qc_ref[...], qw) * scale).astype(BF)
        s = _dot_nt(qc, kc)
        p = jnp.exp(s - jnp.max(s, axis=-1, keepdims=True))
        l = jnp.sum(p, axis=-1, keepdims=True)
        oc_ref[...] = (_dot(p.astype(BF), vc) / l).astype(oc_ref.dtype)

    nq = grp * GRID_W
    nk = span * GRID_W

    ngroups = rows // grp
    per_iter = 2 if ngroups % 2 == 0 else 1

    def body(it, carry):
        geo = []
        for j in range(per_iter):
            g = it * per_iter + j
            geo.append((pl.multiple_of(g * nq, nq), pl.multiple_of(geo_ref[0, g] * GRID_W, GRID_W), geo_ref[1, g]))
        qv = [qs[pl.ds(q0, nq), :] for (q0, k0, t) in geo]
        sws = [_dot_nt(q, ks[pl.ds(k0, nk), :]) + bias_ref[t] for q, (q0, k0, t) in zip(qv, geo)]
        scs = [_dot_nt(q, kc) for q in qv]
        probs = []
        for sw, sc in zip(sws, scs):
            m = jnp.maximum(jnp.max(sw, axis=-1, keepdims=True), jnp.max(sc, axis=-1, keepdims=True))
            pw = jnp.exp(sw - m)
            pc = jnp.exp(sc - m)
            l = jnp.sum(pw, axis=-1, keepdims=True) + jnp.sum(pc, axis=-1, keepdims=True)
            probs.append((pw.astype(BF), pc.astype(BF), l))
        for (pw, pc, l), (q0, k0, t) in zip(probs, geo):
            o = (_dot(pw, vl_ref[pl.ds(k0, nk), :]) + _dot(pc, vc)) / l
            ol_ref[pl.ds(q0, nq), :] = o.astype(ol_ref.dtype)
        return carry

    lax.fori_loop(0, ngroups // per_iter, body, 0)


def _natten_geometry(rows, kh):
    grp = NA_GROUP if rows % NA_GROUP == 0 else 1
    span = min(kh + grp - 1, rows)
    starts, type_ids, types = [], [], []
    for g in range(rows // grp):
        rs = [int(np.clip(g * grp + j - kh // 2, 0, rows - kh)) for j in range(grp)]
        us = int(np.clip(rs[0], 0, rows - span))
        sig = tuple((rs[j] - us, g * grp + j - rs[j]) for j in range(grp))
        assert all(0 <= off <= span - kh for off, _ in sig)
        if sig not in types:
            types.append(sig)
        starts.append(us)
        type_ids.append(types.index(sig))
    return grp, span, np.asarray([starts, type_ids], np.int32), types


def _natten_bias(rpb, kh, span, types):
    q = np.arange(GRID_W)[:, None]
    kc = np.arange(GRID_W)[None, :]
    col_off = np.clip(kc - q, -(NA_WIN_COLS - 1), NA_WIN_COLS - 1) + NA_WIN_COLS - 1
    onehot = (col_off[..., None] == np.arange(2 * NA_WIN_COLS - 1)).astype(np.float32)
    toe = jnp.einsum('lhrc,qkc->lhrqk', rpb.astype(F32), jnp.asarray(onehot), precision=lax.Precision.HIGHEST)
    cs = np.clip(q - NA_WIN_COLS // 2, 0, GRID_W - NA_WIN_COLS)
    valid = (kc >= cs) & (kc < cs + NA_WIN_COLS)
    toe = jnp.where(valid[None, None, None], toe, NEG_INF)
    L, H = rpb.shape[:2]
    pad = jnp.full((L, H, span, GRID_W, GRID_W), NEG_INF, F32)
    toe = jnp.concatenate([pad, toe, pad], axis=2)
    slabs, keep = [], []
    for sig in types:
        for off, d in sig:
            start = span + NA_WIN_ROWS - 1 - d - off
            slabs.append(toe[:, :, start:start + span])
            keep.append([0 <= i - off < kh for i in range(span)])
    slab = jnp.stack(slabs, axis=2)
    slab = jnp.where(np.asarray(keep)[None, None, :, :, None, None], slab, NEG_INF)
    slab = slab.transpose(0, 1, 2, 4, 3, 5)
    grp = len(types[0])
    return slab.reshape(L, H, len(types), grp * GRID_W, span * GRID_W)


def _natten_tables(rpb_all, S):
    rows = S // GRID_W
    kh = min(NA_WIN_ROWS, rows)
    grp, span, geo, types = _natten_geometry(rows, kh)
    return grp, span, geo, _natten_bias(rpb_all, kh, span, types)


def _natten(p_l, p_c, qw, kw, tables, layer, need_ctx):
    B, S, _ = p_l.shape
    CT = p_c.shape[1]
    rows = S // GRID_W
    grp, span, geo, bias_all = tables
    n_types = bias_all.shape[2]
    lat = lambda off: pl.BlockSpec((None, S, LANE), lambda b, h: (b, 0, off + h))
    ctx = lambda off: pl.BlockSpec((None, CT, LANE), lambda b, h: (b, 0, off + h))
    vec = pl.BlockSpec((1, LANE), lambda b, h: (0, 0))
    out_specs = [pl.BlockSpec((None, S, LANE), lambda b, h: (b, 0, h))]
    out_shape = [jax.ShapeDtypeStruct((B, S, GROUP_W), BF)]
    if need_ctx:
        out_specs.append(pl.BlockSpec((None, CT, LANE), lambda b, h: (b, 0, h)))
        out_shape.append(jax.ShapeDtypeStruct((B, CT, GROUP_W), BF))
    res = pl.pallas_call(
        functools.partial(_natten_kernel, need_ctx=need_ctx, rows=rows, grp=grp, span=span),
        grid=(B, N_HEADS),
        in_specs=[pl.BlockSpec(memory_space=pltpu.SMEM),
                  lat(NA_Q), lat(NA_K), lat(NA_V), ctx(NA_Q), ctx(NA_K), ctx(NA_V), vec, vec,
                  pl.BlockSpec((None, None, n_types, grp * GRID_W, span * GRID_W),
                               lambda b, h: (layer, h, 0, 0, 0))],
        out_specs=out_specs,
        out_shape=out_shape,
        scratch_shapes=[pltpu.VMEM((S, LANE), BF), pltpu.VMEM((S, LANE), BF)],
        compiler_params=_cp(("parallel", "parallel")),
        name="natten",
    )(jnp.asarray(geo), p_l, p_l, p_l, p_c, p_c, p_c, qw.reshape(1, LANE), kw.reshape(1, LANE), bias_all)
    return (res[1] if need_ctx else None), res[0]


def _pair_consts():
    lane = lax.broadcasted_iota(jnp.int32, (CHUNK, LANE), 1)
    row = lax.broadcasted_iota(jnp.int32, (CHUNK, LANE), 0)
    col = lane & (CHUNK - 1)
    hmask = (lane < CHUNK, lane >= CHUNK)
    incl2 = (row >= col, row <= col)
    r2 = lax.broadcasted_iota(jnp.int32, (LANE, 2 * LANE), 0)
    c2 = lax.broadcasted_iota(jnp.int32, (LANE, 2 * LANE), 1)
    bmask = (r2 < CHUNK) == (c2 < LANE)
    eye = lax.broadcasted_iota(jnp.int32, (LANE, LANE), 0) == lax.broadcasted_iota(jnp.int32, (LANE, LANE), 1)
    return hmask, incl2, bmask, eye


def _finish_rms(o, nw, g):
    y = o * lax.rsqrt(jnp.mean(o * o, axis=-1, keepdims=True) + EPS) * nw
    return y * _silu(g.astype(F32))


def _gla_kernel(qc_ref, kc_ref, vc_ref, gc_ref, sc_ref, ql_ref, kl_ref, vl_ref, gl_ref, sl_ref,
                gup_ref, gb_ref, nw_ref, *rest, need_ctx, nc_ctx, nc_lat):
    if need_ctx:
        ol_ref, oc_ref, lg_ref, st_ref, of_ref, ob_ref = rest
    else:
        ol_ref, lg_ref, st_ref, of_ref, ob_ref = rest
        oc_ref = None
    CT = nc_ctx * CHUNK
    S = nc_lat * CHUNK
    _, _, _, _, tri_bf = _tri_consts()
    hmask, incl2, bmask, eye = _pair_consts()
    zero_v = jnp.zeros((CHUNK, LANE), BF)

    for d in range(2):
        for (s_ref, r0, n) in ((sc_ref, 0, CT), (sl_ref, CT, S)):
            z = _dot(s_ref[...].astype(BF), gup_ref[d]) + gb_ref[d]
            lg_ref[d, r0:r0 + n, :] = _log_sigmoid(z) * (1.0 / GLA_GATE_TAU)
    st_ref[...] = jnp.zeros_like(st_ref)

    def segment(q_ref, k_ref, v_ref, row0, nch, write):
        per_iter = 4 if nch % 4 == 0 else (2 if nch % 2 == 0 else 1)

        def body(it, carry):
            pre = []
            for j in range(per_iter):
                n = it * per_iter + j
                for d in range(2):
                    c = n if d == 0 else nch - 1 - n
                    r0 = pl.multiple_of(c * CHUNK, CHUNK)
                    pre.append((d, r0, q_ref[pl.ds(r0, CHUNK), :].astype(F32), k_ref[pl.ds(r0, CHUNK), :].astype(F32)))
            cums = [_exact_dot(tri_bf[d], lg_ref[d, pl.ds(row0 + r0, CHUNK), :]) for (d, r0, _, _) in pre]
            work = []
            for (d, r0, q2, k2), cum in zip(pre, cums):
                tot = cum[CHUNK - 1:CHUNK, :] if d == 0 else cum[0:1, :]
                ge_col = jnp.sum(jnp.where(eye, jnp.exp(tot), 0.0), axis=1, keepdims=True)
                ke = (k2 * jnp.exp(tot - cum)).astype(BF)
                vp = v_ref[pl.ds(r0, CHUNK), :]
                qd = kdm = None
                if write:
                    qd = (q2 * (jnp.exp(cum) * (GLA_DK ** -0.5))).astype(BF)
                    kd = k2 * jnp.exp(-cum)
                    kdm = jnp.concatenate([jnp.where(hmask[0], kd, 0.0), jnp.where(hmask[1], kd, 0.0)],
                                          axis=0).astype(BF)
                work.append((d, r0, vp, qd, kdm, ke, ge_col))
            if write:
                atts = [jnp.where(incl2[d], _dot_nt(qd, kdm), 0.0).astype(BF)
                        for (d, r0, vp, qd, kdm, ke, ge_col) in work]
                intras = [_dot(att, jnp.concatenate(
                              [jnp.concatenate([vp[:, 0:LANE], zero_v], axis=1),
                               jnp.concatenate([zero_v, vp[:, LANE:2 * LANE]], axis=1)], axis=0))
                          for att, (d, r0, vp, qd, kdm, ke, ge_col) in zip(atts, work)]
            upds = [jnp.where(bmask, _dot_tn(ke, vp), 0.0) for (d, r0, vp, qd, kdm, ke, ge_col) in work]
            state = {d: st_ref[d] for d in range(2)}
            for i, (d, r0, vp, qd, kdm, ke, ge_col) in enumerate(work):
                st = state[d]
                if write:
                    oref = of_ref if d == 0 else ob_ref
                    oref[pl.ds(row0 + r0, CHUNK), :] = intras[i] + _dot(qd, st.astype(BF))
                state[d] = st * ge_col + upds[i]
            for d, st in state.items():
                st_ref[d] = st
            return carry

        lax.fori_loop(0, nch // per_iter, body, 0)

    segment(qc_ref, kc_ref, vc_ref, 0, nc_ctx, need_ctx)
    segment(ql_ref, kl_ref, vl_ref, CT, nc_lat, True)

    nw = nw_ref[...]
    for hh in range(2):
        sl = slice(hh * LANE, (hh + 1) * LANE)
        o = of_ref[CT:CT + S, sl] + ob_ref[CT:CT + S, sl]
        ol_ref[:, sl] = _finish_rms(o, nw, gl_ref[:, sl]).astype(ol_ref.dtype)
        if need_ctx:
            o = of_ref[0:CT, sl] + ob_ref[0:CT, sl]
            oc_ref[:, sl] = _finish_rms(o, nw, gc_ref[:, sl]).astype(oc_ref.dtype)


def _gla(p_l, ps_l, p_c, ps_c, gate_up, gate_b, o_norm, need_ctx):
    B, S, _ = p_l.shape
    CT = p_c.shape[1]
    T = CT + S
    gup = jnp.zeros((2, N_SMALL, N_HEADS * GLA_DK), F32)
    for d in range(2):
        gup = gup.at[d, SM_RK + d * GLA_GATE_RANK:SM_RK + (d + 1) * GLA_GATE_RANK].set(gate_up[d])
    gup = gup.astype(BF)
    gb = gate_b.reshape(2, 1, N_HEADS * GLA_DK)

    def spec(n, width, off):
        return pl.BlockSpec((None, n, width), lambda b, p: (b, 0, off(p)))

    in_specs = []
    for n in (CT, S):
        in_specs += [spec(n, LANE, lambda p: GLA_Q + p), spec(n, LANE, lambda p: GLA_K + p),
                     spec(n, 2 * LANE, lambda p: GLA_V // 2 + p), spec(n, 2 * LANE, lambda p: GLA_G // 2 + p),
                     spec(n, N_SMALL, lambda p: 0)]
    in_specs += [pl.BlockSpec((2, N_SMALL, LANE), lambda b, p: (0, 0, p)),
                 pl.BlockSpec((2, 1, LANE), lambda b, p: (0, 0, p)),
                 pl.BlockSpec((1, LANE), lambda b, p: (0, 0))]
    out_specs = [pl.BlockSpec((None, S, 2 * LANE), lambda b, p: (b, 0, p))]
    out_shape = [jax.ShapeDtypeStruct((B, S, GROUP_W), BF)]
    if need_ctx:
        out_specs.append(pl.BlockSpec((None, CT, 2 * LANE), lambda b, p: (b, 0, p)))
        out_shape.append(jax.ShapeDtypeStruct((B, CT, GROUP_W), BF))
    res = pl.pallas_call(
        functools.partial(_gla_kernel, need_ctx=need_ctx, nc_ctx=CT // CHUNK, nc_lat=S // CHUNK),
        grid=(B, 2),
        in_specs=in_specs,
        out_specs=out_specs,
        out_shape=out_shape,
        scratch_shapes=[pltpu.VMEM((2, T, LANE), F32),
                        pltpu.VMEM((2, LANE, 2 * LANE), F32),
                        pltpu.VMEM((T, 2 * LANE), F32),
                        pltpu.VMEM((T, 2 * LANE), F32)],
        compiler_params=_cp(("parallel", "parallel")),
        name="gla",
    )(p_c, p_c, p_c, p_c, ps_c, p_l, p_l, p_l, p_l, ps_l, gup, gb, o_norm.reshape(1, LANE))
    return (res[1] if need_ctx else None), res[0]


def _rope_tables(S):
    pos = np.arange(S)
    half = RET_DK // 2
    quarter = half // 2
    freqs = ROPE_BASE ** (-np.arange(quarter, dtype=np.float64) / quarter)
    cos = np.zeros((S, LANE), np.float64)
    sin_dn = np.zeros((S, LANE), np.float64)
    sin_up = np.zeros((S, LANE), np.float64)
    for head in range(2):
        for part, p in enumerate((pos // GRID_W, pos % GRID_W)):
            ang = p[:, None].astype(np.float64) * freqs[None, :]
            base = head * RET_DK + part * half
            cos[:, base:base + quarter] = np.cos(ang)
            cos[:, base + quarter:base + half] = np.cos(ang)
            sin_dn[:, base:base + quarter] = -np.sin(ang)
            sin_up[:, base + quarter:base + half] = np.sin(ang)
    return tuple(jnp.asarray(t, F32) for t in (cos, sin_dn, sin_up))


def _ret_kernel(qc_ref, kc_ref, vc_ref, gc_ref, ql_ref, kl_ref, vl_ref, gl_ref,
                cos_ref, sdn_ref, sup_ref, dl_ref, gnw_ref, *rest, need_ctx, nc_ctx, nc_lat):
    if need_ctx:
        ol_ref, oc_ref, qr_ref, kr_ref, st_ref, of_ref, ob_ref = rest
    else:
        ol_ref, qr_ref, kr_ref, st_ref, of_ref, ob_ref = rest
        oc_ref = None
    CT = nc_ctx * CHUNK
    S = nc_lat * CHUNK
    hmask, incl2, bmask, _ = _pair_consts()
    zero_v = jnp.zeros((CHUNK, LANE), BF)
    pp = pl.program_id(1)

    def rope(x):
        quarter = RET_DK // 4
        return (x * cos_ref[...] + pltpu.roll(x, LANE - quarter, 1) * sdn_ref[...]
                + pltpu.roll(x, quarter, 1) * sup_ref[...])

    qr_ref[0:CT, :] = qc_ref[...].astype(F32)
    kr_ref[0:CT, :] = kc_ref[...].astype(F32) * (RET_DK ** -0.5)
    qr_ref[CT:CT + S, :] = rope(ql_ref[...].astype(F32))
    kr_ref[CT:CT + S, :] = rope(kl_ref[...].astype(F32) * (RET_DK ** -0.5))
    st_ref[...] = jnp.zeros_like(st_ref)

    dmat, qfac, kfac, gend = {}, {}, {}, {}
    rowf = lax.broadcasted_iota(jnp.int32, (CHUNK, LANE), 0).astype(F32)
    colf = (lax.broadcasted_iota(jnp.int32, (CHUNK, LANE), 1) & (CHUNK - 1)).astype(F32)
    lane_v = lax.broadcasted_iota(jnp.int32, (CHUNK, 2 * LANE), 1)
    row_v = lax.broadcasted_iota(jnp.int32, (CHUNK, 2 * LANE), 0).astype(F32)
    row_k = lax.broadcasted_iota(jnp.int32, (LANE, 1), 0)
    for d in range(2):
        lg_a = _log_sigmoid(dl_ref[pl.ds(d * N_HEADS + 2 * pp, 1), 0:1])
        lg_b = _log_sigmoid(dl_ref[pl.ds(d * N_HEADS + 2 * pp + 1, 1), 0:1])
        lg2 = jnp.where(hmask[0], lg_a, lg_b)
        dist = (rowf - colf) if d == 0 else (colf - rowf)
        dmat[d] = jnp.exp(jnp.where(incl2[d], dist * lg2, -jnp.inf))
        steps_v = (row_v + 1.0) if d == 0 else (CHUNK - row_v)
        qfac[d] = jnp.exp(steps_v * jnp.where(lane_v < LANE, lg_a, lg_b))
        steps_k = (rowf + 1.0) if d == 0 else (CHUNK - rowf)
        kfac[d] = jnp.exp((CHUNK - steps_k) * lg2)
        gend[d] = jnp.exp(CHUNK * jnp.where(row_k < CHUNK, lg_a, lg_b))

    def segment(v_ref, row0, nch, write):
        per_iter = 4 if nch % 4 == 0 else (2 if nch % 2 == 0 else 1)

        def body(it, carry):
            work = []
            for j in range(per_iter):
                n = it * per_iter + j
                for d in range(2):
                    c = n if d == 0 else nch - 1 - n
                    r0 = pl.multiple_of(c * CHUNK, CHUNK)
                    k2 = kr_ref[pl.ds(row0 + r0, CHUNK), :]
                    vp = v_ref[pl.ds(r0, CHUNK), :]
                    qb = km = None
                    if write:
                        qb = qr_ref[pl.ds(row0 + r0, CHUNK), :].astype(BF)
                        km = jnp.concatenate([jnp.where(hmask[0], k2, 0.0), jnp.where(hmask[1], k2, 0.0)],
                                             axis=0).astype(BF)
                    work.append((d, r0, vp, qb, km, (k2 * kfac[d]).astype(BF)))
            if write:
                atts = [(_dot_nt(qb, km) * dmat[d]).astype(BF) for (d, r0, vp, qb, km, ke) in work]
                intras = [_dot(att, jnp.concatenate(
                              [jnp.concatenate([vp[:, 0:LANE], zero_v], axis=1),
                               jnp.concatenate([zero_v, vp[:, LANE:2 * LANE]], axis=1)], axis=0))
                          for att, (d, r0, vp, qb, km, ke) in zip(atts, work)]
            upds = [jnp.where(bmask, _dot_tn(ke, vp), 0.0) for (d, r0, vp, qb, km, ke) in work]
            state = {d: st_ref[d] for d in range(2)}
            for i, (d, r0, vp, qb, km, ke) in enumerate(work):
                st = state[d]
                if write:
                    oref = of_ref if d == 0 else ob_ref
                    oref[pl.ds(row0 + r0, CHUNK), :] = intras[i] + _dot(qb, st.astype(BF)) * qfac[d]
                state[d] = st * gend[d] + upds[i]
            for d, st in state.items():
                st_ref[d] = st
            return carry

        lax.fori_loop(0, nch // per_iter, body, 0)

    segment(vc_ref, 0, nc_ctx, need_ctx)
    segment(vl_ref, CT, nc_lat, True)

    def finish(o, w, g):
        mu = jnp.mean(o, axis=-1, keepdims=True)
        oc = o - mu
        var = jnp.mean(oc * oc, axis=-1, keepdims=True)
        return oc * lax.rsqrt(var + EPS) * w * _silu(g.astype(F32))

    for hh in range(2):
        sl = slice(hh * LANE, (hh + 1) * LANE)
        w = gnw_ref[:, sl]
        o = of_ref[CT:CT + S, sl] + ob_ref[CT:CT + S, sl]
        ol_ref[:, sl] = finish(o, w, gl_ref[:, sl]).astype(ol_ref.dtype)
        if need_ctx:
            o = of_ref[0:CT, sl] + ob_ref[0:CT, sl]
            oc_ref[:, sl] = finish(o, w, gc_ref[:, sl]).astype(oc_ref.dtype)


def _ret(p_l, p_c, decay_logit, gn_w, need_ctx):
    B, S, _ = p_l.shape
    CT = p_c.shape[1]
    T = CT + S
    cos, sdn, sup = _rope_tables(S)
    dl = jnp.broadcast_to(decay_logit.reshape(2 * N_HEADS, 1).astype(F32), (2 * N_HEADS, LANE))

    def spec(n, width, off):
        return pl.BlockSpec((None, n, width), lambda b, p: (b, 0, off(p)))

    in_specs = []
    for n in (CT, S):
        in_specs += [spec(n, LANE, lambda p: RET_Q + p), spec(n, LANE, lambda p: RET_K + p),
                     spec(n, 2 * LANE, lambda p: RET_V // 2 + p), spec(n, 2 * LANE, lambda p: RET_G // 2 + p)]
    tab = pl.BlockSpec((S, LANE), lambda b, p: (0, 0))
    in_specs += [tab, tab, tab,
                 pl.BlockSpec((2 * N_HEADS, LANE), lambda b, p: (0, 0)),
                 pl.BlockSpec((1, 2 * LANE), lambda b, p: (0, p))]
    out_specs = [pl.BlockSpec((None, S, 2 * LANE), lambda b, p: (b, 0, p))]
    out_shape = [jax.ShapeDtypeStruct((B, S, GROUP_W), BF)]
    if need_ctx:
        out_specs.append(pl.BlockSpec((None, CT, 2 * LANE), lambda b, p: (b, 0, p)))
        out_shape.append(jax.ShapeDtypeStruct((B, CT, GROUP_W), BF))
    res = pl.pallas_call(
        functools.partial(_ret_kernel, need_ctx=need_ctx, nc_ctx=CT // CHUNK, nc_lat=S // CHUNK),
        grid=(B, 2),
        in_specs=in_specs,
        out_specs=out_specs,
        out_shape=out_shape,
        scratch_shapes=[pltpu.VMEM((T, LANE), F32),
                        pltpu.VMEM((T, LANE), F32),
                        pltpu.VMEM((2, LANE, 2 * LANE), F32),
                        pltpu.VMEM((T, 2 * LANE), F32),
                        pltpu.VMEM((T, 2 * LANE), F32)],
        compiler_params=_cp(("parallel", "parallel")),
        name="retention",
    )(p_c, p_c, p_c, p_c, p_l, p_l, p_l, p_l, cos, sdn, sup, dl, gn_w.reshape(1, GROUP_W))
    return (res[1] if need_ctx else None), res[0]


def _gdn_kernel(qc_ref, kc_ref, vc_ref, zc_ref, sc_ref, ql_ref, kl_ref, vl_ref, zl_ref, sl_ref,
                cwq_ref, cwk_ref, cwv_ref, alog_ref, dtb_ref, nw_ref, *rest, need_ctx, nc_ctx, nc_lat):
    if need_ctx:
        ol_ref, oc_ref = rest[:2]
        rest = rest[2:]
    else:
        ol_ref = rest[0]
        oc_ref = None
        rest = rest[1:]
    qs, ks, vs, sm_ref, u_ref, wq_ref, a_ref, ke_ref, ge_ref, st_ref, of_ref, ob_ref = rest
    CT = nc_ctx * CHUNK
    S = nc_lat * CHUNK
    pp = pl.program_id(1)
    sm_ref[0:CT, :] = sc_ref[...]
    sm_ref[CT:CT + S, :] = sl_ref[...]
    ri = lax.broadcasted_iota(jnp.int32, (GDN_BLK, GDN_BLK), 0)
    ci = lax.broadcasted_iota(jnp.int32, (GDN_BLK, GDN_BLK), 1)
    same = (ri // CHUNK) == (ci // CHUNK)
    incl = (same & (ri >= ci), same & (ri <= ci))
    strict = (same & (ri > ci), same & (ri < ci))
    tri_bf = [jnp.where(m, 1.0, 0.0).astype(BF) for m in incl]
    eye_f = jnp.where(ri == ci, 1.0, 0.0)
    lane = lax.broadcasted_iota(jnp.int32, (GDN_BLK, LANE), 1)
    rowi = lax.broadcasted_iota(jnp.int32, (GDN_BLK, 1), 0)
    lvl_masks = []
    s = 1
    while s < CHUNK:
        lvl_masks.append(jnp.where(((ri // (2 * s)) == (ci // (2 * s))) & ((ri // s) != (ci // s)), 1.0, 0.0))
        s *= 2

    def conv_silu(x_ref, w_ref, n):
        x = x_ref[...].astype(F32)
        row = lax.broadcasted_iota(jnp.int32, x.shape, 0)
        xp = jnp.where(row == 0, 0.0, pltpu.roll(x, 1, 0))
        xn = jnp.where(row == n - 1, 0.0, pltpu.roll(x, n - 1, 0))
        return _silu(xp * w_ref[0:1, :] + x * w_ref[1:2, :] + xn * w_ref[2:3, :])

    def l2n(x):
        return x * lax.rsqrt(jnp.sum(x * x, axis=-1, keepdims=True) + EPS)

    for (q_ref, k_ref, v_ref, r0, n) in ((qc_ref, kc_ref, vc_ref, 0, CT), (ql_ref, kl_ref, vl_ref, CT, S)):
        q = conv_silu(q_ref, cwq_ref, n)
        k = conv_silu(k_ref, cwk_ref, n)
        v = conv_silu(v_ref, cwv_ref, n)
        for hh in range(2):
            sl = slice(hh * LANE, (hh + 1) * LANE)
            qs[r0:r0 + n, sl] = l2n(q[:, sl]) * (GDN_DK ** -0.5)
            ks[r0:r0 + n, sl] = l2n(k[:, sl])
        vs[r0:r0 + n, :] = v.astype(BF)

    neg_a = -jnp.exp(alog_ref[...])
    dtb = dtb_ref[...]

    def phase1(fwd_rows, bwd_rows):
        pipes = []
        a_list = []
        for d, rows in ((0, fwd_rows), (1, bwd_rows)):
            for g0 in rows:
                sm = sm_ref[pl.ds(g0, GDN_BLK), :]
                lg_all = neg_a * _softplus(sm + dtb)
                lb_all = _log_sigmoid(sm)
                cum_all = _exact_dot(tri_bf[d], lg_all)
                for hh in range(2):
                    h = 2 * pp + hh
                    sl = slice(hh * LANE, (hh + 1) * LANE)
                    k = ks[pl.ds(g0, GDN_BLK), sl]
                    q = qs[pl.ds(g0, GDN_BLK), sl]
                    v = vs[pl.ds(g0, GDN_BLK), sl]
                    kb = k.astype(BF)
                    kkqk = _dot_nt(jnp.concatenate([kb, q.astype(BF)], axis=0), kb)
                    kk = kkqk[0:GDN_BLK, :]
                    qk = kkqk[GDN_BLK:2 * GDN_BLK, :]
                    g = jnp.sum(jnp.where(lane == SM_A + d * N_HEADS + h, cum_all, 0.0), axis=-1, keepdims=True)
                    lb = jnp.sum(jnp.where(lane == SM_BT + d * N_HEADS + h, lb_all, 0.0), axis=-1, keepdims=True)
                    ends = (CHUNK - 1, GDN_BLK - 1) if d == 0 else (0, CHUNK)
                    tot_lo = g[ends[0]:ends[0] + 1, :]
                    tot_hi = g[ends[1]:ends[1] + 1, :]
                    tot = jnp.where(rowi < CHUNK, tot_lo, tot_hi)
                    hrow = jnp.broadcast_to(g - lb, (GDN_BLK, GDN_BLK)).T
                    e_in = jnp.exp(jnp.where(incl[d], g - hrow, -jnp.inf))
                    a_list.append(kk * jnp.where(strict[d], e_in, 0.0))
                    pipes.append((g0, hh, d, k, q, v, qk, e_in, g, lb, tot, tot_lo, tot_hi))
                yield
        a4 = jnp.stack(a_list, axis=0)
        x4 = eye_f[None] - a4 * lvl_masks[0][None]
        bdot = lambda p, r: jnp.einsum('pij,pjk->pik', p, r, preferred_element_type=F32)
        for msk in lvl_masks[1:]:
            t4 = (a4 * msk[None]).astype(BF)
            xb = x4.astype(BF)
            x4 = x4 - bdot(xb, bdot(t4, xb).astype(BF))
            yield
        x4b = x4.astype(BF)
        for p, (g0, hh, d, k, q, v, qk, e_in, g, lb, tot, tot_lo, tot_hi) in enumerate(pipes):
            gam = jnp.exp(g)
            rhs = jnp.concatenate([v, (k * gam).astype(BF)], axis=1)
            uw = _dot(x4b[p], rhs)
            u_ref[d, hh, pl.ds(g0, GDN_BLK), :] = uw[:, 0:LANE]
            w = uw[:, LANE:2 * LANE].astype(BF)
            qg = (q * gam).astype(BF)
            wq0 = _aligned(2 * g0, 2 * GDN_BLK)
            wq_ref[d, hh, pl.ds(wq0, 2 * GDN_BLK), :] = jnp.concatenate(
                [w[0:CHUNK], qg[0:CHUNK], w[CHUNK:GDN_BLK], qg[CHUNK:GDN_BLK]], axis=0)
            a_ref[d, hh, pl.ds(g0, GDN_BLK), :] = (qk * e_in).astype(BF)
            ke_ref[d, hh, pl.ds(g0, GDN_BLK), :] = (k * jnp.exp(tot - g + lb)).astype(BF)
            ge0 = _aligned((g0 // CHUNK) * 8, 16)
            ge_ref[d, hh, pl.ds(ge0, 16), :] = jnp.concatenate(
                [jnp.broadcast_to(jnp.exp(tot_lo), (8, LANE)), jnp.broadcast_to(jnp.exp(tot_hi), (8, LANE))], axis=0)
            if p % 4 == 3:
                yield

    def phase2(fwd_rows, bwd_rows):
        steps = []
        for gf, gb in zip(fwd_rows, bwd_rows):
            steps.append((gf, gb + CHUNK))
            steps.append((gf + CHUNK, gb))
        state = {(d, hh): st_ref[d, hh] for d in range(2) for hh in range(2)}
        outs = []
        for cf, cb in steps:
            chains = [(d, hh, _aligned(c, CHUNK)) for d, c in ((0, cf), (1, cb)) for hh in range(2)]
            wss = [_dot(wq_ref[d, hh, pl.ds(_aligned(2 * g0, 2 * CHUNK), 2 * CHUNK), :],
                        state[d, hh].astype(BF)) for (d, hh, g0) in chains]
            dbs = [(u_ref[d, hh, pl.ds(g0, CHUNK), :] - ws[0:CHUNK, :]).astype(BF)
                   for (d, hh, g0), ws in zip(chains, wss)]
            upd = [_dot_tn(ke_ref[d, hh, pl.ds(g0, CHUNK), :], db) for (d, hh, g0), db in zip(chains, dbs)]
            for (d, hh, g0), up in zip(chains, upd):
                ge = ge_ref[d, hh, pl.ds(_aligned((g0 // CHUNK) * 8, 8), 1), :]
                state[d, hh] = state[d, hh] * ge[:, 0:1] + up
            outs.append((chains, wss, dbs))
            yield
        for (d, hh), st in state.items():
            st_ref[d, hh] = st
        for chains, wss, dbs in outs:
            for (d, hh, g0), ws, db in zip(chains, wss, dbs):
                o = ws[CHUNK:2 * CHUNK, :] + _dot(a_ref[d, hh, pl.ds(g0, CHUNK), :],
                                                  jnp.concatenate([db, db], axis=0))
                oref = of_ref if d == 0 else ob_ref
                oref[pl.ds(g0, CHUNK), hh * LANE:(hh + 1) * LANE] = o
            yield

    def run(*gens):
        gens = list(gens)
        while gens:
            for gen in list(gens):
                try:
                    next(gen)
                except StopIteration:
                    gens.remove(gen)

    def seg_rows(base, nblk, first, count):
        fwd = [_aligned(base + (first + j) * GDN_BLK, GDN_BLK) for j in range(count)]
        bwd = [_aligned(base + (nblk - 1 - first - j) * GDN_BLK, GDN_BLK) for j in range(count)]
        return fwd, bwd

    nb_ctx = CT // GDN_BLK
    nb_lat = S // GDN_BLK
    grp = 4 if nb_lat % 4 == 0 else (2 if nb_lat % 2 == 0 else 1)
    n_grp = nb_lat // grp
    st_ref[...] = jnp.zeros_like(st_ref)

    ctx_rows = seg_rows(0, nb_ctx, 0, nb_ctx)
    run(phase1(*ctx_rows))
    run(phase1(*seg_rows(CT, nb_lat, 0, grp)), phase2(*ctx_rows))

    def stage(i, carry):
        run(phase1(*seg_rows(CT, nb_lat, i * grp, grp)), phase2(*seg_rows(CT, nb_lat, (i - 1) * grp, grp)))
        return carry

    lax.fori_loop(1, n_grp, stage, 0)
    run(phase2(*seg_rows(CT, nb_lat, (n_grp - 1) * grp, grp)))

    nw = nw_ref[...]
    for hh in range(2):
        sl = slice(hh * LANE, (hh + 1) * LANE)
        o = of_ref[CT:CT + S, sl] + ob_ref[CT:CT + S, sl]
        ol_ref[:, sl] = _finish_rms(o, nw, zl_ref[:, sl]).astype(ol_ref.dtype)
        if need_ctx:
            o = of_ref[0:CT, sl] + ob_ref[0:CT, sl]
            oc_ref[:, sl] = _finish_rms(o, nw, zc_ref[:, sl]).astype(oc_ref.dtype)


def _gdn(p_l, ps_l, p_c, ps_c, conv_w, a_log, dt_bias, o_norm, need_ctx):
    B, S, _ = p_l.shape
    CT = p_c.shape[1]
    T = CT + S
    nch = T // CHUNK
    alog = jnp.zeros((1, N_SMALL), F32).at[0, SM_A:SM_A + 2 * N_HEADS].set(a_log.reshape(-1).astype(F32))
    dtb = jnp.zeros((1, N_SMALL), F32).at[0, SM_A:SM_A + 2 * N_HEADS].set(dt_bias.reshape(-1).astype(F32))

    def spec(n, width, off):
        return pl.BlockSpec((None, n, width), lambda b, p: (b, 0, off(p)))

    in_specs = []
    for n in (CT, S):
        in_specs += [spec(n, 2 * LANE, lambda p: GDN_Q // 2 + p), spec(n, 2 * LANE, lambda p: GDN_K // 2 + p),
                     spec(n, 2 * LANE, lambda p: GDN_V // 2 + p), spec(n, 2 * LANE, lambda p: GDN_Z // 2 + p),
                     spec(n, N_SMALL, lambda p: 0)]
    cw = lambda part: pl.BlockSpec((3, 2 * LANE), lambda b, p: (0, 2 * part + p))
    vec = pl.BlockSpec((1, LANE), lambda b, p: (0, 0))
    in_specs += [cw(0), cw(1), cw(2), vec, vec, vec]
    out_specs = [pl.BlockSpec((None, S, 2 * LANE), lambda b, p: (b, 0, p))]
    out_shape = [jax.ShapeDtypeStruct((B, S, GROUP_W), BF)]
    if need_ctx:
        out_specs.append(pl.BlockSpec((None, CT, 2 * LANE), lambda b, p: (b, 0, p)))
        out_shape.append(jax.ShapeDtypeStruct((B, CT, GROUP_W), BF))
    res = pl.pallas_call(
        functools.partial(_gdn_kernel, need_ctx=need_ctx, nc_ctx=CT // CHUNK, nc_lat=S // CHUNK),
        grid=(B, 2),
        in_specs=in_specs,
        out_specs=out_specs,
        out_shape=out_shape,
        scratch_shapes=[pltpu.VMEM((T, 2 * LANE), F32),
                        pltpu.VMEM((T, 2 * LANE), F32),
                        pltpu.VMEM((T, 2 * LANE), BF),
                        pltpu.VMEM((T, N_SMALL), F32),
                        pltpu.VMEM((2, 2, T, LANE), F32),
                        pltpu.VMEM((2, 2, 2 * T, LANE), BF),
                        pltpu.VMEM((2, 2, T, GDN_BLK), BF),
                        pltpu.VMEM((2, 2, T, LANE), BF),
                        pltpu.VMEM((2, 2, nch * 8, LANE), F32),
                        pltpu.VMEM((2, 2, LANE, LANE), F32),
                        pltpu.VMEM((T, 2 * LANE), F32),
                        pltpu.VMEM((T, 2 * LANE), F32)],
        compiler_params=_cp(("parallel", "parallel")),
        name="gdn",
    )(p_c, p_c, p_c, p_c, ps_c, p_l, p_l, p_l, p_l, ps_l,
      conv_w, conv_w, conv_w, alog, dtb, o_norm.reshape(1, LANE))
    return (res[1] if need_ctx else None), res[0]


def _align_w_in(w):
    D = w.shape[0]
    big = jnp.concatenate([w[:, 0:3072], w[:, 3104:5152], w[:, 5168:6704]], axis=1)
    small = jnp.concatenate([w[:, 3072:3104], w[:, 5152:5168], jnp.zeros((D, N_SMALL - 48), w.dtype)], axis=1)
    return big.astype(BF), small.astype(BF)


def _pick(n, prefs):
    for p in prefs:
        if n % p == 0:
            return p
    return n


def kernel(x, c, ctx, c_ctx, ada_w, ada_b, norm1_w, norm2_w, w_in, w_out, na_q_norm, na_k_norm, na_rpb,
           gla_gate_up, gla_gate_b, gla_o_norm, gdn_conv_w, gdn_a_log, gdn_dt_bias, gdn_o_norm,
           ret_decay_logit, ret_gn_w, mlp_w1, mlp_w2):
    B, S, D = x.shape
    CT = ctx.shape[1]
    depth = ada_w.shape[0]
    R = ((B + 1 + 7) // 8) * 8
    cc = jnp.concatenate([c, c_ctx[None, :], jnp.zeros((R - B - 1, D), F32)], axis=0)
    mod_all = _ada(cc, ada_w, ada_b).reshape(depth, R, 6, D)
    na_tables = _natten_tables(na_rpb, S)

    tm_l = _pick(S, (1024, 512, 256))
    tm_c = _pick(B * CT, (1024, 512, 256))
    tn = _pick(N_BIG, (1664, 512, 256, 128))
    tm_o = _pick(S, (512, 256))
    tm_oc = _pick(B * CT, (512, 256))
    th = _pick(mlp_w1.shape[2], (1024, 512, 256))

    xl = x
    xc = ctx.reshape(1, B * CT, D)
    for layer in range(depth):
        need_ctx = layer < depth - 1
        mod = mod_all[layer]
        w_big, w_small = _align_w_in(w_in[layer])
        nw1 = norm1_w[layer].reshape(1, D)
        nw2 = norm2_w[layer].reshape(1, D)
        wo = w_out[layer].astype(BF)
        w1 = mlp_w1[layer].astype(BF)
        w2 = mlp_w2[layer].astype(BF)

        p_l, ps_l = _in_proj(xl, mod, None, nw1, w_big, w_small, tm_l, tn)
        p_c, ps_c = _in_proj(xc, mod, B, nw1, w_big, w_small, tm_c, tn)
        p_c = p_c.reshape(B, CT, N_BIG)
        ps_c = ps_c.reshape(B, CT, N_SMALL)

        na_c, na_l = _natten(p_l, p_c, na_q_norm[layer], na_k_norm[layer], na_tables, layer, need_ctx)
        gl_c, gl_l = _gla(p_l, ps_l, p_c, ps_c, gla_gate_up[layer], gla_gate_b[layer], gla_o_norm[layer], need_ctx)
        gd_c, gd_l = _gdn(p_l, ps_l, p_c, ps_c, gdn_conv_w[layer], gdn_a_log[layer], gdn_dt_bias[layer],
                          gdn_o_norm[layer], need_ctx)
        rt_c, rt_l = _ret(p_l, p_c, ret_decay_logit[layer], ret_gn_w[layer], need_ctx)

        xl, hl = _out_proj(xl, (na_l, gl_l, gd_l, rt_l), wo, mod, None, nw2, tm_o)
        xl = _mlp(xl, hl, mod, None, w1, w2, tm_o, th)
        if need_ctx:
            ys = tuple(t.reshape(1, B * CT, GROUP_W) for t in (na_c, gl_c, gd_c, rt_c))
            xc, hc = _out_proj(xc, ys, wo, mod, B, nw2, tm_oc)
            xc = _mlp(xc, hc, mod, B, w1, w2, tm_oc, th)
    return xl
```

```python
import functools

import numpy as np
import jax
import jax.numpy as jnp
from jax import lax
from jax.experimental import pallas as pl
from jax.experimental.pallas import tpu as pltpu

BF = jnp.bfloat16
F32 = jnp.float32

N_HEADS = 4
HEAD_DIM = 128
GROUP_W = N_HEADS * HEAD_DIM
GRID_W = 64
NA_WIN_ROWS = 8
NA_WIN_COLS = 16
NA_GROUP = 4
GLA_DK = 64
GLA_GATE_RANK = 16
GLA_GATE_TAU = 16.0
GDN_DK = 128
RET_DK = 64
ROPE_BASE = 10000.0
CHUNK = 64
GDN_BLK = 2 * CHUNK
EPS = 1e-6
NEG_INF = -1e30

LANE = 128
N_BIG = 52 * LANE
N_SMALL = LANE
NA_Q, NA_K, NA_V = 0, 4, 8
GLA_Q, GLA_K, GLA_V, GLA_G = 12, 14, 16, 20
GDN_Q, GDN_K, GDN_V, GDN_Z = 24, 28, 32, 36
RET_Q, RET_K, RET_V, RET_G = 40, 42, 44, 48
SM_RK, SM_A, SM_BT = 0, 32, 40

VMEM_LIMIT = 56 * 1024 * 1024


def _cp(sem, vmem=VMEM_LIMIT):
    return pltpu.CompilerParams(dimension_semantics=sem, vmem_limit_bytes=vmem)


def _aligned(x, m):
    return x if isinstance(x, int) else pl.multiple_of(x, m)


def _dot(a, b):
    return jnp.dot(a, b, preferred_element_type=F32)


def _dot_nt(a, b):
    return lax.dot_general(a, b, (((1,), (1,)), ((), ())), preferred_element_type=F32)


def _dot_tn(a, b):
    return lax.dot_general(a, b, (((0,), (0,)), ((), ())), preferred_element_type=F32)


def _split3(x):
    hi = x.astype(BF)
    r1 = x - hi.astype(F32)
    mid = r1.astype(BF)
    lo = (r1 - mid.astype(F32)).astype(BF)
    return hi, mid, lo


def _exact_dot(m_bf, x):
    hi, mid, lo = _split3(x)
    n = x.shape[1]
    r = _dot(m_bf, jnp.concatenate([hi, mid, lo], axis=1))
    return r[:, 0:n] + r[:, n:2 * n] + r[:, 2 * n:3 * n]


def _sigmoid(x):
    return 1.0 / (1.0 + jnp.exp(-x))


def _silu(x):
    return x * _sigmoid(x)


def _log_sigmoid(x):
    return jnp.minimum(x, 0.0) - jnp.log(1.0 + jnp.exp(-jnp.abs(x)))


def _softplus(x):
    return jnp.maximum(x, 0.0) + jnp.log(1.0 + jnp.exp(-jnp.abs(x)))


def _ln_mod(x, nw, shift, scale):
    ms = jnp.mean(x * x, axis=-1, keepdims=True)
    return (x * lax.rsqrt(ms + EPS) * nw) * (1.0 + scale) + shift


def _tri_consts():
    ri = lax.broadcasted_iota(jnp.int32, (CHUNK, CHUNK), 0)
    ci = lax.broadcasted_iota(jnp.int32, (CHUNK, CHUNK), 1)
    incl = (ri >= ci, ri <= ci)
    strict = (ri > ci, ri < ci)
    tri_bf = tuple(jnp.where(m, 1.0, 0.0).astype(BF) for m in incl)
    return ri, ci, incl, strict, tri_bf


def _ada_kernel(c_ref, w_ref, b_ref, o_ref):
    sc = _silu(c_ref[...]).astype(BF)
    o_ref[...] = _dot(sc, w_ref[...].astype(BF)) + b_ref[...]


def _ada(cc, ada_w, ada_b):
    L, D, N6 = ada_w.shape
    R = cc.shape[0]
    tn = _pick(N6, (2048, 1024))
    return pl.pallas_call(
        _ada_kernel,
        grid=(L, N6 // tn),
        in_specs=[pl.BlockSpec((R, D), lambda l, j: (0, 0)),
                  pl.BlockSpec((None, D, tn), lambda l, j: (l, 0, j)),
                  pl.BlockSpec((None, 1, tn), lambda l, j: (l, 0, j))],
        out_specs=pl.BlockSpec((None, R, tn), lambda l, j: (l, 0, j)),
        out_shape=jax.ShapeDtypeStruct((L, R, N6), F32),
        compiler_params=_cp(("parallel", "parallel")),
        name="ada_ln",
    )(cc, ada_w, ada_b.reshape(L, 1, N6))


def _inproj_kernel(x_ref, mod_ref, nw_ref, w_ref, ws_ref, o_ref, os_ref, h_ref):
    @pl.when(pl.program_id(1) == 0)
    def _():
        h = _ln_mod(x_ref[...], nw_ref[...], mod_ref[0:1, :], mod_ref[1:2, :]).astype(BF)
        h_ref[...] = h
        os_ref[...] = _dot(h, ws_ref[...])

    o_ref[...] = _dot(h_ref[...], w_ref[...]).astype(o_ref.dtype)


def _in_proj(x3, mod, const_row, nw, w_big, w_small, layer, tm, tn):
    Bn, Tn, D = x3.shape
    nt = Tn // tm
    if const_row is None:
        mod_map = lambda i, j: (i // nt, 0, 0)
    else:
        mod_map = lambda i, j: (const_row, 0, 0)
    return pl.pallas_call(
        _inproj_kernel,
        grid=(Bn * nt, N_BIG // tn),
        in_specs=[pl.BlockSpec((None, tm, D), lambda i, j: (i // nt, i % nt, 0)),
                  pl.BlockSpec((None, 6, D), mod_map),
                  pl.BlockSpec((1, D), lambda i, j: (0, 0)),
                  pl.BlockSpec((None, D, tn), lambda i, j: (layer, 0, j)),
                  pl.BlockSpec((None, D, N_SMALL), lambda i, j: (layer, 0, 0))],
        out_specs=[pl.BlockSpec((None, tm, tn), lambda i, j: (i // nt, i % nt, j)),
                   pl.BlockSpec((None, tm, N_SMALL), lambda i, j: (i // nt, i % nt, 0))],
        out_shape=[jax.ShapeDtypeStruct((Bn, Tn, N_BIG), BF),
                   jax.ShapeDtypeStruct((Bn, Tn, N_SMALL), F32)],
        scratch_shapes=[pltpu.VMEM((tm, D), BF)],
        compiler_params=_cp(("parallel", "arbitrary")),
        name="in_proj",
    )(x3, mod, nw, w_big, w_small)


def _outproj_kernel(x_ref, y0, y1, y2, y3, w_ref, mod_ref, nw_ref, o_ref, h_ref):
    acc = _dot(y0[...], w_ref[0 * GROUP_W:1 * GROUP_W, :])
    acc += _dot(y1[...], w_ref[1 * GROUP_W:2 * GROUP_W, :])
    acc += _dot(y2[...], w_ref[2 * GROUP_W:3 * GROUP_W, :])
    acc += _dot(y3[...], w_ref[3 * GROUP_W:4 * GROUP_W, :])
    xn = x_ref[...] + mod_ref[2:3, :] * acc
    o_ref[...] = xn
    h_ref[...] = _ln_mod(xn, nw_ref[...], mod_ref[3:4, :], mod_ref[4:5, :]).astype(BF)


def _out_proj(x3, ys, w_out, layer, mod, const_row, nw2, tm):
    Bn, Tn, D = x3.shape
    nt = Tn // tm
    if const_row is None:
        mod_map = lambda i: (i // nt, 0, 0)
    else:
        mod_map = lambda i: (const_row, 0, 0)
    row_map = lambda i: (i // nt, i % nt, 0)
    return pl.pallas_call(
        _outproj_kernel,
        grid=(Bn * nt,),
        in_specs=[pl.BlockSpec((None, tm, D), row_map)]
                 + [pl.BlockSpec((None, tm, GROUP_W), row_map)] * 4
                 + [pl.BlockSpec((None, 4 * GROUP_W, D), lambda i: (layer, 0, 0)),
                    pl.BlockSpec((None, 6, D), mod_map),
                    pl.BlockSpec((1, D), lambda i: (0, 0))],
        out_specs=[pl.BlockSpec((None, tm, D), row_map), pl.BlockSpec((None, tm, D), row_map)],
        out_shape=[jax.ShapeDtypeStruct((Bn, Tn, D), F32), jax.ShapeDtypeStruct((Bn, Tn, D), BF)],
        compiler_params=_cp(("parallel",)),
        name="out_proj",
    )(x3, *ys, w_out, mod, nw2)


def _mlp_kernel(x_ref, h_ref, mod_ref, w1_ref, w2_ref, o_ref, *, nk):
    k = pl.program_id(1)

    @pl.when(k == 0)
    def _():
        o_ref[...] = jnp.zeros_like(o_ref)

    hid = jnp.maximum(_dot(h_ref[...], w1_ref[...]), 0.0)
    o_ref[...] += _dot((hid * hid).astype(BF), w2_ref[...])

    @pl.when(k == nk - 1)
    def _():
        o_ref[...] = x_ref[...] + mod_ref[5:6, :] * o_ref[...]


def _mlp(x3, h3, mod, const_row, w1, w2, layer, tm, th):
    Bn, Tn, D = x3.shape
    Hd = w1.shape[2]
    nt = Tn // tm
    nk = Hd // th
    if const_row is None:
        mod_map = lambda i, k: (i // nt, 0, 0)
    else:
        mod_map = lambda i, k: (const_row, 0, 0)
    row_map = lambda i, k: (i // nt, i % nt, 0)
    return pl.pallas_call(
        functools.partial(_mlp_kernel, nk=nk),
        grid=(Bn * nt, nk),
        in_specs=[pl.BlockSpec((None, tm, D), row_map),
                  pl.BlockSpec((None, tm, D), row_map),
                  pl.BlockSpec((None, 6, D), mod_map),
                  pl.BlockSpec((None, D, th), lambda i, k: (layer, 0, k)),
                  pl.BlockSpec((None, th, D), lambda i, k: (layer, k, 0))],
        out_specs=pl.BlockSpec((None, tm, D), row_map),
        out_shape=jax.ShapeDtypeStruct((Bn, Tn, D), F32),
        compiler_params=_cp(("parallel", "arbitrary")),
        name="mlp",
    )(x3, h3, mod, w1, w2)


def _rms_head(x, w):
    x = x.astype(F32)
    return x * lax.rsqrt(jnp.mean(x * x, axis=-1, keepdims=True) + EPS) * w


def _natten_kernel(geo_ref, ql_ref, kl_ref, vl_ref, qc_ref, kc_ref, vc_ref, qw_ref, kw_ref, bias_ref, *rest,
                   need_ctx, rows, grp, span):
    if need_ctx:
        ol_ref, oc_ref, qs, ks = rest
    else:
        ol_ref, qs, ks = rest
    scale = HEAD_DIM ** -0.5
    qw = qw_ref[...]
    kw = kw_ref[...]
    qs[...] = (_rms_head(ql_ref[...], qw) * scale).astype(BF)
    ks[...] = _rms_head(kl_ref[...], kw).astype(BF)
    kc = _rms_head(kc_ref[...], kw).astype(BF)
    vc = vc_ref[...]
    if need_ctx:
        qc = (_rms_head(qc_ref[...], qw) * scale).astype(BF)
        s = _dot_nt(qc, kc)
        p = jnp.exp(s - jnp.max(s, axis=-1, keepdims=True))
        l = jnp.sum(p, axis=-1, keepdims=True)
        oc_ref[...] = (_dot(p.astype(BF), vc) / l).astype(oc_ref.dtype)

    nq = grp * GRID_W
    nk = span * GRID_W

    ngroups = rows // grp
    per_iter = 2 if ngroups % 2 == 0 else 1

    def body(it, carry):
        geo = []
        for j in range(per_iter):
            g = it * per_iter + j
            geo.append((pl.multiple_of(g * nq, nq), pl.multiple_of(geo_ref[0, g] * GRID_W, GRID_W), geo_ref[1, g]))
        qv = [qs[pl.ds(q0, nq), :] for (q0, k0, t) in geo]
        sws = [_dot_nt(q, ks[pl.ds(k0, nk), :]) + bias_ref[t] for q, (q0, k0, t) in zip(qv, geo)]
        scs = [_dot_nt(q, kc) for q in qv]
        probs = []
        for sw, sc in zip(sws, scs):
            m = jnp.maximum(jnp.max(sw, axis=-1, keepdims=True), jnp.max(sc, axis=-1, keepdims=True))
            pw = jnp.exp(sw - m)
            pc = jnp.exp(sc - m)
            l = jnp.sum(pw, axis=-1, keepdims=True) + jnp.sum(pc, axis=-1, keepdims=True)
            probs.append((pw.astype(BF), pc.astype(BF), l))
        for (pw, pc, l), (q0, k0, t) in zip(probs, geo):
            o = (_dot(pw, vl_ref[pl.ds(k0, nk), :]) + _dot(pc, vc)) / l
            ol_ref[pl.ds(q0, nq), :] = o.astype(ol_ref.dtype)
        return carry

    lax.fori_loop(0, ngroups // per_iter, body, 0)


def _natten_geometry(rows, kh):
    grp = NA_GROUP if rows % NA_GROUP == 0 else 1
    span = min(kh + grp - 1, rows)
    starts, type_ids, types = [], [], []
    for g in range(rows // grp):
        rs = [int(np.clip(g * grp + j - kh // 2, 0, rows - kh)) for j in range(grp)]
        us = int(np.clip(rs[0], 0, rows - span))
        sig = tuple((rs[j] - us, g * grp + j - rs[j]) for j in range(grp))
        assert all(0 <= off <= span - kh for off, _ in sig)
        if sig not in types:
            types.append(sig)
        starts.append(us)
        type_ids.append(types.index(sig))
    return grp, span, np.asarray([starts, type_ids], np.int32), types


def _natten_bias(rpb, kh, span, types):
    q = np.arange(GRID_W)[:, None]
    kc = np.arange(GRID_W)[None, :]
    col_off = np.clip(kc - q, -(NA_WIN_COLS - 1), NA_WIN_COLS - 1) + NA_WIN_COLS - 1
    onehot = (col_off[..., None] == np.arange(2 * NA_WIN_COLS - 1)).astype(np.float32)
    toe = jnp.einsum('lhrc,qkc->lhrqk', rpb.astype(F32), jnp.asarray(onehot), precision=lax.Precision.HIGHEST)
    cs = np.clip(q - NA_WIN_COLS // 2, 0, GRID_W - NA_WIN_COLS)
    valid = (kc >= cs) & (kc < cs + NA_WIN_COLS)
    toe = jnp.where(valid[None, None, None], toe, NEG_INF)
    L, H = rpb.shape[:2]
    pad = jnp.full((L, H, span, GRID_W, GRID_W), NEG_INF, F32)
    toe = jnp.concatenate([pad, toe, pad], axis=2).transpose(0, 1, 3, 2, 4)
    slabs, keep = [], []
    for sig in types:
        for off, d in sig:
            start = span + NA_WIN_ROWS - 1 - d - off
            slabs.append(toe[:, :, :, start:start + span])
            keep.append([0 <= i - off < kh for i in range(span)])
    slab = jnp.stack(slabs, axis=2)
    slab = jnp.where(np.asarray(keep)[None, None, :, None, :, None], slab, NEG_INF)
    grp = len(types[0])
    return slab.reshape(L, H, len(types), grp * GRID_W, span * GRID_W)


def _natten_tables(rpb_all, S):
    rows = S // GRID_W
    kh = min(NA_WIN_ROWS, rows)
    grp, span, geo, types = _natten_geometry(rows, kh)
    return grp, span, geo, _natten_bias(rpb_all, kh, span, types)


def _natten(p_l, p_c, qw, kw, tables, layer, need_ctx):
    B, S, _ = p_l.shape
    CT = p_c.shape[1]
    rows = S // GRID_W
    grp, span, geo, bias_all = tables
    n_types = bias_all.shape[2]
    lat = lambda off: pl.BlockSpec((None, S, LANE), lambda b, h: (b, 0, off + h))
    ctx = lambda off: pl.BlockSpec((None, CT, LANE), lambda b, h: (b, 0, off + h))
    vec = pl.BlockSpec((1, LANE), lambda b, h: (0, 0))
    out_specs = [pl.BlockSpec((None, S, LANE), lambda b, h: (b, 0, h))]
    out_shape = [jax.ShapeDtypeStruct((B, S, GROUP_W), BF)]
    if need_ctx:
        out_specs.append(pl.BlockSpec((None, CT, LANE), lambda b, h: (b, 0, h)))
        out_shape.append(jax.ShapeDtypeStruct((B, CT, GROUP_W), BF))
    res = pl.pallas_call(
        functools.partial(_natten_kernel, need_ctx=need_ctx, rows=rows, grp=grp, span=span),
        grid=(B, N_HEADS),
        in_specs=[pl.BlockSpec(memory_space=pltpu.SMEM),
                  lat(NA_Q), lat(NA_K), lat(NA_V), ctx(NA_Q), ctx(NA_K), ctx(NA_V), vec, vec,
                  pl.BlockSpec((None, None, n_types, grp * GRID_W, span * GRID_W),
                               lambda b, h: (layer, h, 0, 0, 0))],
        out_specs=out_specs,
        out_shape=out_shape,
        scratch_shapes=[pltpu.VMEM((S, LANE), BF), pltpu.VMEM((S, LANE), BF)],
        compiler_params=_cp(("parallel", "parallel")),
        name="natten",
    )(jnp.asarray(geo), p_l, p_l, p_l, p_c, p_c, p_c, qw.reshape(1, LANE), kw.reshape(1, LANE), bias_all)
    return (res[1] if need_ctx else None), res[0]


def _pair_consts():
    lane = lax.broadcasted_iota(jnp.int32, (CHUNK, LANE), 1)
    row = lax.broadcasted_iota(jnp.int32, (CHUNK, LANE), 0)
    col = lane & (CHUNK - 1)
    hmask = (lane < CHUNK, lane >= CHUNK)
    incl2 = (row >= col, row <= col)
    r2 = lax.broadcasted_iota(jnp.int32, (LANE, 2 * LANE), 0)
    c2 = lax.broadcasted_iota(jnp.int32, (LANE, 2 * LANE), 1)
    bmask = (r2 < CHUNK) == (c2 < LANE)
    eye = lax.broadcasted_iota(jnp.int32, (LANE, LANE), 0) == lax.broadcasted_iota(jnp.int32, (LANE, LANE), 1)
    return hmask, incl2, bmask, eye


def _finish_rms(o, nw, g):
    y = o * lax.rsqrt(jnp.mean(o * o, axis=-1, keepdims=True) + EPS) * nw
    return y * _silu(g.astype(F32))


def _gla_kernel(qc_ref, kc_ref, vc_ref, gc_ref, sc_ref, ql_ref, kl_ref, vl_ref, gl_ref, sl_ref,
                gup_ref, gb_ref, nw_ref, *rest, need_ctx, nc_ctx, nc_lat):
    if need_ctx:
        ol_ref, oc_ref, lg_ref, st_ref, of_ref, ob_ref = rest
    else:
        ol_ref, lg_ref, st_ref, of_ref, ob_ref = rest
        oc_ref = None
    CT = nc_ctx * CHUNK
    S = nc_lat * CHUNK
    _, _, _, _, tri_bf = _tri_consts()
    hmask, incl2, bmask, eye = _pair_consts()
    zero_v = jnp.zeros((CHUNK, LANE), BF)

    for d in range(2):
        for (s_ref, r0, n) in ((sc_ref, 0, CT), (sl_ref, CT, S)):
            z = _dot(s_ref[...].astype(BF), gup_ref[d]) + gb_ref[d]
            lg_ref[d, r0:r0 + n, :] = _log_sigmoid(z) * (1.0 / GLA_GATE_TAU)
    st_ref[...] = jnp.zeros_like(st_ref)

    def segment(q_ref, k_ref, v_ref, row0, nch, write):
        per_iter = 4 if nch % 4 == 0 else (2 if nch % 2 == 0 else 1)

        def body(it, carry):
            pre = []
            for j in range(per_iter):
                n = it * per_iter + j
                for d in range(2):
                    c = n if d == 0 else nch - 1 - n
                    r0 = pl.multiple_of(c * CHUNK, CHUNK)
                    pre.append((d, r0, q_ref[pl.ds(r0, CHUNK), :].astype(F32), k_ref[pl.ds(r0, CHUNK), :].astype(F32)))
            cums = [_exact_dot(tri_bf[d], lg_ref[d, pl.ds(row0 + r0, CHUNK), :]) for (d, r0, _, _) in pre]
            work = []
            for (d, r0, q2, k2), cum in zip(pre, cums):
                tot = cum[CHUNK - 1:CHUNK, :] if d == 0 else cum[0:1, :]
                ge_col = jnp.sum(jnp.where(eye, jnp.exp(tot), 0.0), axis=1, keepdims=True)
                ke = (k2 * jnp.exp(tot - cum)).astype(BF)
                vp = v_ref[pl.ds(r0, CHUNK), :]
                qd = kdm = None
                if write:
                    qd = (q2 * (jnp.exp(cum) * (GLA_DK ** -0.5))).astype(BF)
                    kd = k2 * jnp.exp(-cum)
                    kdm = jnp.concatenate([jnp.where(hmask[0], kd, 0.0), jnp.where(hmask[1], kd, 0.0)],
                                          axis=0).astype(BF)
                work.append((d, r0, vp, qd, kdm, ke, ge_col))
            if write:
                atts = [jnp.where(incl2[d], _dot_nt(qd, kdm), 0.0).astype(BF)
                        for (d, r0, vp, qd, kdm, ke, ge_col) in work]
                intras = [_dot(att, jnp.concatenate(
                              [jnp.concatenate([vp[:, 0:LANE], zero_v], axis=1),
                               jnp.concatenate([zero_v, vp[:, LANE:2 * LANE]], axis=1)], axis=0))
                          for att, (d, r0, vp, qd, kdm, ke, ge_col) in zip(atts, work)]
            upds = [jnp.where(bmask, _dot_tn(ke, vp), 0.0) for (d, r0, vp, qd, kdm, ke, ge_col) in work]
            state = {d: st_ref[d] for d in range(2)}
            for i, (d, r0, vp, qd, kdm, ke, ge_col) in enumerate(work):
                st = state[d]
                if write:
                    oref = of_ref if d == 0 else ob_ref
                    oref[pl.ds(row0 + r0, CHUNK), :] = intras[i] + _dot(qd, st.astype(BF))
                state[d] = st * ge_col + upds[i]
            for d, st in state.items():
                st_ref[d] = st
            return carry

        lax.fori_loop(0, nch // per_iter, body, 0)

    segment(qc_ref, kc_ref, vc_ref, 0, nc_ctx, need_ctx)
    segment(ql_ref, kl_ref, vl_ref, CT, nc_lat, True)

    nw = nw_ref[...]
    for hh in range(2):
        sl = slice(hh * LANE, (hh + 1) * LANE)
        o = of_ref[CT:CT + S, sl] + ob_ref[CT:CT + S, sl]
        ol_ref[:, sl] = _finish_rms(o, nw, gl_ref[:, sl]).astype(ol_ref.dtype)
        if need_ctx:
            o = of_ref[0:CT, sl] + ob_ref[0:CT, sl]
            oc_ref[:, sl] = _finish_rms(o, nw, gc_ref[:, sl]).astype(oc_ref.dtype)


def _gla(p_l, ps_l, p_c, ps_c, gate_up, gate_b, o_norm, need_ctx):
    B, S, _ = p_l.shape
    CT = p_c.shape[1]
    T = CT + S
    gup = jnp.zeros((2, N_SMALL, N_HEADS * GLA_DK), F32)
    for d in range(2):
        gup = gup.at[d, SM_RK + d * GLA_GATE_RANK:SM_RK + (d + 1) * GLA_GATE_RANK].set(gate_up[d])
    gup = gup.astype(BF)
    gb = gate_b.reshape(2, 1, N_HEADS * GLA_DK)

    def spec(n, width, off):
        return pl.BlockSpec((None, n, width), lambda b, p: (b, 0, off(p)))

    in_specs = []
    for n in (CT, S):
        in_specs += [spec(n, LANE, lambda p: GLA_Q + p), spec(n, LANE, lambda p: GLA_K + p),
                     spec(n, 2 * LANE, lambda p: GLA_V // 2 + p), spec(n, 2 * LANE, lambda p: GLA_G // 2 + p),
                     spec(n, N_SMALL, lambda p: 0)]
    in_specs += [pl.BlockSpec((2, N_SMALL, LANE), lambda b, p: (0, 0, p)),
                 pl.BlockSpec((2, 1, LANE), lambda b, p: (0, 0, p)),
                 pl.BlockSpec((1, LANE), lambda b, p: (0, 0))]
    out_specs = [pl.BlockSpec((None, S, 2 * LANE), lambda b, p: (b, 0, p))]
    out_shape = [jax.ShapeDtypeStruct((B, S, GROUP_W), BF)]
    if need_ctx:
        out_specs.append(pl.BlockSpec((None, CT, 2 * LANE), lambda b, p: (b, 0, p)))
        out_shape.append(jax.ShapeDtypeStruct((B, CT, GROUP_W), BF))
    res = pl.pallas_call(
        functools.partial(_gla_kernel, need_ctx=need_ctx, nc_ctx=CT // CHUNK, nc_lat=S // CHUNK),
        grid=(B, 2),
        in_specs=in_specs,
        out_specs=out_specs,
        out_shape=out_shape,
        scratch_shapes=[pltpu.VMEM((2, T, LANE), F32),
                        pltpu.VMEM((2, LANE, 2 * LANE), F32),
                        pltpu.VMEM((T, 2 * LANE), F32),
                        pltpu.VMEM((T, 2 * LANE), F32)],
        compiler_params=_cp(("parallel", "parallel")),
        name="gla",
    )(p_c, p_c, p_c, p_c, ps_c, p_l, p_l, p_l, p_l, ps_l, gup, gb, o_norm.reshape(1, LANE))
    return (res[1] if need_ctx else None), res[0]


def _rope_tables(S):
    pos = np.arange(S)
    half = RET_DK // 2
    quarter = half // 2
    freqs = ROPE_BASE ** (-np.arange(quarter, dtype=np.float64) / quarter)
    cos = np.zeros((S, LANE), np.float64)
    sin_dn = np.zeros((S, LANE), np.float64)
    sin_up = np.zeros((S, LANE), np.float64)
    for head in range(2):
        for part, p in enumerate((pos // GRID_W, pos % GRID_W)):
            ang = p[:, None].astype(np.float64) * freqs[None, :]
            base = head * RET_DK + part * half
            cos[:, base:base + quarter] = np.cos(ang)
            cos[:, base + quarter:base + half] = np.cos(ang)
            sin_dn[:, base:base + quarter] = -np.sin(ang)
            sin_up[:, base + quarter:base + half] = np.sin(ang)
    return tuple(jnp.asarray(t, F32) for t in (cos, sin_dn, sin_up))


def _ret_kernel(qc_ref, kc_ref, vc_ref, gc_ref, ql_ref, kl_ref, vl_ref, gl_ref,
                cos_ref, sdn_ref, sup_ref, dl_ref, gnw_ref, *rest, need_ctx, nc_ctx, nc_lat):
    if need_ctx:
        ol_ref, oc_ref, qr_ref, kr_ref, st_ref, of_ref, ob_ref = rest
    else:
        ol_ref, qr_ref, kr_ref, st_ref, of_ref, ob_ref = rest
        oc_ref = None
    CT = nc_ctx * CHUNK
    S = nc_lat * CHUNK
    hmask, incl2, bmask, _ = _pair_consts()
    zero_v = jnp.zeros((CHUNK, LANE), BF)
    pp = pl.program_id(1)

    def rope(x):
        quarter = RET_DK // 4
        return (x * cos_ref[...] + pltpu.roll(x, LANE - quarter, 1) * sdn_ref[...]
                + pltpu.roll(x, quarter, 1) * sup_ref[...])

    qr_ref[0:CT, :] = qc_ref[...].astype(F32)
    kr_ref[0:CT, :] = kc_ref[...].astype(F32) * (RET_DK ** -0.5)
    qr_ref[CT:CT + S, :] = rope(ql_ref[...].astype(F32))
    kr_ref[CT:CT + S, :] = rope(kl_ref[...].astype(F32) * (RET_DK ** -0.5))
    st_ref[...] = jnp.zeros_like(st_ref)

    dmat, qfac, kfac, gend = {}, {}, {}, {}
    rowf = lax.broadcasted_iota(jnp.int32, (CHUNK, LANE), 0).astype(F32)
    colf = (lax.broadcasted_iota(jnp.int32, (CHUNK, LANE), 1) & (CHUNK - 1)).astype(F32)
    lane_v = lax.broadcasted_iota(jnp.int32, (CHUNK, 2 * LANE), 1)
    row_v = lax.broadcasted_iota(jnp.int32, (CHUNK, 2 * LANE), 0).astype(F32)
    row_k = lax.broadcasted_iota(jnp.int32, (LANE, 1), 0)
    for d in range(2):
        lg_a = _log_sigmoid(dl_ref[pl.ds(d * N_HEADS + 2 * pp, 1), 0:1])
        lg_b = _log_sigmoid(dl_ref[pl.ds(d * N_HEADS + 2 * pp + 1, 1), 0:1])
        lg2 = jnp.where(hmask[0], lg_a, lg_b)
        dist = (rowf - colf) if d == 0 else (colf - rowf)
        dmat[d] = jnp.exp(jnp.where(incl2[d], dist * lg2, -jnp.inf))
        steps_v = (row_v + 1.0) if d == 0 else (CHUNK - row_v)
        qfac[d] = jnp.exp(steps_v * jnp.where(lane_v < LANE, lg_a, lg_b))
        steps_k = (rowf + 1.0) if d == 0 else (CHUNK - rowf)
        kfac[d] = jnp.exp((CHUNK - steps_k) * lg2)
        gend[d] = jnp.exp(CHUNK * jnp.where(row_k < CHUNK, lg_a, lg_b))

    def segment(v_ref, row0, nch, write):
        per_iter = 4 if nch % 4 == 0 else (2 if nch % 2 == 0 else 1)

        def body(it, carry):
            work = []
            for j in range(per_iter):
                n = it * per_iter + j
                for d in range(2):
                    c = n if d == 0 else nch - 1 - n
                    r0 = pl.multiple_of(c * CHUNK, CHUNK)
                    k2 = kr_ref[pl.ds(row0 + r0, CHUNK), :]
                    vp = v_ref[pl.ds(r0, CHUNK), :]
                    qb = km = None
                    if write:
                        qb = qr_ref[pl.ds(row0 + r0, CHUNK), :].astype(BF)
                        km = jnp.concatenate([jnp.where(hmask[0], k2, 0.0), jnp.where(hmask[1], k2, 0.0)],
                                             axis=0).astype(BF)
                    work.append((d, r0, vp, qb, km, (k2 * kfac[d]).astype(BF)))
            if write:
                atts = [(_dot_nt(qb, km) * dmat[d]).astype(BF) for (d, r0, vp, qb, km, ke) in work]
                intras = [_dot(att, jnp.concatenate(
                              [jnp.concatenate([vp[:, 0:LANE], zero_v], axis=1),
                               jnp.concatenate([zero_v, vp[:, LANE:2 * LANE]], axis=1)], axis=0))
                          for att, (d, r0, vp, qb, km, ke) in zip(atts, work)]
            upds = [jnp.where(bmask, _dot_tn(ke, vp), 0.0) for (d, r0, vp, qb, km, ke) in work]
            state = {d: st_ref[d] for d in range(2)}
            for i, (d, r0, vp, qb, km, ke) in enumerate(work):
                st = state[d]
                if write:
                    oref = of_ref if d == 0 else ob_ref
                    oref[pl.ds(row0 + r0, CHUNK), :] = intras[i] + _dot(qb, st.astype(BF)) * qfac[d]
                state[d] = st * gend[d] + upds[i]
            for d, st in state.items():
                st_ref[d] = st
            return carry

        lax.fori_loop(0, nch // per_iter, body, 0)

    segment(vc_ref, 0, nc_ctx, need_ctx)
    segment(vl_ref, CT, nc_lat, True)

    def finish(o, w, g):
        mu = jnp.mean(o, axis=-1, keepdims=True)
        oc = o - mu
        var = jnp.mean(oc * oc, axis=-1, keepdims=True)
        return oc * lax.rsqrt(var + EPS) * w * _silu(g.astype(F32))

    for hh in range(2):
        sl = slice(hh * LANE, (hh + 1) * LANE)
        w = gnw_ref[:, sl]
        o = of_ref[CT:CT + S, sl] + ob_ref[CT:CT + S, sl]
        ol_ref[:, sl] = finish(o, w, gl_ref[:, sl]).astype(ol_ref.dtype)
        if need_ctx:
            o = of_ref[0:CT, sl] + ob_ref[0:CT, sl]
            oc_ref[:, sl] = finish(o, w, gc_ref[:, sl]).astype(oc_ref.dtype)


def _ret(p_l, p_c, decay_logit, gn_w, need_ctx):
    B, S, _ = p_l.shape
    CT = p_c.shape[1]
    T = CT + S
    cos, sdn, sup = _rope_tables(S)
    dl = jnp.broadcast_to(decay_logit.reshape(2 * N_HEADS, 1).astype(F32), (2 * N_HEADS, LANE))

    def spec(n, width, off):
        return pl.BlockSpec((None, n, width), lambda b, p: (b, 0, off(p)))

    in_specs = []
    for n in (CT, S):
        in_specs += [spec(n, LANE, lambda p: RET_Q + p), spec(n, LANE, lambda p: RET_K + p),
                     spec(n, 2 * LANE, lambda p: RET_V // 2 + p), spec(n, 2 * LANE, lambda p: RET_G // 2 + p)]
    tab = pl.BlockSpec((S, LANE), lambda b, p: (0, 0))
    in_specs += [tab, tab, tab,
                 pl.BlockSpec((2 * N_HEADS, LANE), lambda b, p: (0, 0)),
                 pl.BlockSpec((1, 2 * LANE), lambda b, p: (0, p))]
    out_specs = [pl.BlockSpec((None, S, 2 * LANE), lambda b, p: (b, 0, p))]
    out_shape = [jax.ShapeDtypeStruct((B, S, GROUP_W), BF)]
    if need_ctx:
        out_specs.append(pl.BlockSpec((None, CT, 2 * LANE), lambda b, p: (b, 0, p)))
        out_shape.append(jax.ShapeDtypeStruct((B, CT, GROUP_W), BF))
    res = pl.pallas_call(
        functools.partial(_ret_kernel, need_ctx=need_ctx, nc_ctx=CT // CHUNK, nc_lat=S // CHUNK),
        grid=(B, 2),
        in_specs=in_specs,
        out_specs=out_specs,
        out_shape=out_shape,
        scratch_shapes=[pltpu.VMEM((T, LANE), F32),
                        pltpu.VMEM((T, LANE), F32),
                        pltpu.VMEM((2, LANE, 2 * LANE), F32),
                        pltpu.VMEM((T, 2 * LANE), F32),
                        pltpu.VMEM((T, 2 * LANE), F32)],
        compiler_params=_cp(("parallel", "parallel")),
        name="retention",
    )(p_c, p_c, p_c, p_c, p_l, p_l, p_l, p_l, cos, sdn, sup, dl, gn_w.reshape(1, GROUP_W))
    return (res[1] if need_ctx else None), res[0]


def _gdn_kernel(qc_ref, kc_ref, vc_ref, zc_ref, sc_ref, ql_ref, kl_ref, vl_ref, zl_ref, sl_ref,
                cwq_ref, cwk_ref, cwv_ref, alog_ref, dtb_ref, nw_ref, *rest, need_ctx, nc_ctx, nc_lat):
    if need_ctx:
        ol_ref, oc_ref = rest[:2]
        rest = rest[2:]
    else:
        ol_ref = rest[0]
        oc_ref = None
        rest = rest[1:]
    qs, ks, vs, sm_ref, u_ref, wq_ref, a_ref, ke_ref, ge_ref, st_ref, of_ref, ob_ref = rest
    CT = nc_ctx * CHUNK
    S = nc_lat * CHUNK
    pp = pl.program_id(1)
    sm_ref[0:CT, :] = sc_ref[...]
    sm_ref[CT:CT + S, :] = sl_ref[...]
    ri = lax.broadcasted_iota(jnp.int32, (GDN_BLK, GDN_BLK), 0)
    ci = lax.broadcasted_iota(jnp.int32, (GDN_BLK, GDN_BLK), 1)
    same = (ri // CHUNK) == (ci // CHUNK)
    incl = (same & (ri >= ci), same & (ri <= ci))
    strict = (same & (ri > ci), same & (ri < ci))
    tri_bf = [jnp.where(m, 1.0, 0.0).astype(BF) for m in incl]
    eye_f = jnp.where(ri == ci, 1.0, 0.0)
    lane = lax.broadcasted_iota(jnp.int32, (GDN_BLK, LANE), 1)
    rowi = lax.broadcasted_iota(jnp.int32, (GDN_BLK, 1), 0)
    lvl_masks = []
    s = 1
    while s < CHUNK:
        lvl_masks.append(jnp.where(((ri // (2 * s)) == (ci // (2 * s))) & ((ri // s) != (ci // s)), 1.0, 0.0))
        s *= 2

    def conv_silu(x_ref, w_ref, n):
        x = x_ref[...].astype(F32)
        row = lax.broadcasted_iota(jnp.int32, x.shape, 0)
        xp = jnp.where(row == 0, 0.0, pltpu.roll(x, 1, 0))
        xn = jnp.where(row == n - 1, 0.0, pltpu.roll(x, n - 1, 0))
        return _silu(xp * w_ref[0:1, :] + x * w_ref[1:2, :] + xn * w_ref[2:3, :])

    def l2n(x):
        return x * lax.rsqrt(jnp.sum(x * x, axis=-1, keepdims=True) + EPS)

    for (q_ref, k_ref, v_ref, r0, n) in ((qc_ref, kc_ref, vc_ref, 0, CT), (ql_ref, kl_ref, vl_ref, CT, S)):
        q = conv_silu(q_ref, cwq_ref, n)
        k = conv_silu(k_ref, cwk_ref, n)
        v = conv_silu(v_ref, cwv_ref, n)
        for hh in range(2):
            sl = slice(hh * LANE, (hh + 1) * LANE)
            qs[r0:r0 + n, sl] = l2n(q[:, sl]) * (GDN_DK ** -0.5)
            ks[r0:r0 + n, sl] = l2n(k[:, sl])
        vs[r0:r0 + n, :] = v.astype(BF)

    neg_a = -jnp.exp(alog_ref[...])
    dtb = dtb_ref[...]

    def phase1(fwd_rows, bwd_rows):
        pipes = []
        a_list = []
        for d, rows in ((0, fwd_rows), (1, bwd_rows)):
            for g0 in rows:
                sm = sm_ref[pl.ds(g0, GDN_BLK), :]
                lg_all = neg_a * _softplus(sm + dtb)
                lb_all = _log_sigmoid(sm)
                cum_all = _exact_dot(tri_bf[d], lg_all)
                for hh in range(2):
                    h = 2 * pp + hh
                    sl = slice(hh * LANE, (hh + 1) * LANE)
                    k = ks[pl.ds(g0, GDN_BLK), sl]
                    q = qs[pl.ds(g0, GDN_BLK), sl]
                    v = vs[pl.ds(g0, GDN_BLK), sl]
                    kb = k.astype(BF)
                    kkqk = _dot_nt(jnp.concatenate([kb, q.astype(BF)], axis=0), kb)
                    kk = kkqk[0:GDN_BLK, :]
                    qk = kkqk[GDN_BLK:2 * GDN_BLK, :]
                    g = jnp.sum(jnp.where(lane == SM_A + d * N_HEADS + h, cum_all, 0.0), axis=-1, keepdims=True)
                    lb = jnp.sum(jnp.where(lane == SM_BT + d * N_HEADS + h, lb_all, 0.0), axis=-1, keepdims=True)
                    ends = (CHUNK - 1, GDN_BLK - 1) if d == 0 else (0, CHUNK)
                    tot_lo = g[ends[0]:ends[0] + 1, :]
                    tot_hi = g[ends[1]:ends[1] + 1, :]
                    tot = jnp.where(rowi < CHUNK, tot_lo, tot_hi)
                    hrow = jnp.broadcast_to(g - lb, (GDN_BLK, GDN_BLK)).T
                    e_in = jnp.exp(jnp.where(incl[d], g - hrow, -jnp.inf))
                    a_list.append(kk * jnp.where(strict[d], e_in, 0.0))
                    pipes.append((g0, hh, d, k, q, v, qk, e_in, g, lb, tot, tot_lo, tot_hi))
                yield
        a4 = jnp.stack(a_list, axis=0)
        x4 = eye_f[None] - a4 * lvl_masks[0][None]
        bdot = lambda p, r: jnp.einsum('pij,pjk->pik', p, r, preferred_element_type=F32)
        for msk in lvl_masks[1:]:
            t4 = (a4 * msk[None]).astype(BF)
            xb = x4.astype(BF)
            x4 = x4 - bdot(xb, bdot(t4, xb).astype(BF))
            yield
        x4b = x4.astype(BF)
        for p, (g0, hh, d, k, q, v, qk, e_in, g, lb, tot, tot_lo, tot_hi) in enumerate(pipes):
            gam = jnp.exp(g)
            rhs = jnp.concatenate([v, (k * gam).astype(BF)], axis=1)
            uw = _dot(x4b[p], rhs)
            u_ref[d, hh, pl.ds(g0, GDN_BLK), :] = uw[:, 0:LANE]
            w = uw[:, LANE:2 * LANE].astype(BF)
            qg = (q * gam).astype(BF)
            wq0 = _aligned(2 * g0, 2 * GDN_BLK)
            wq_ref[d, hh, pl.ds(wq0, 2 * GDN_BLK), :] = jnp.concatenate(
                [w[0:CHUNK], qg[0:CHUNK], w[CHUNK:GDN_BLK], qg[CHUNK:GDN_BLK]], axis=0)
            a_ref[d, hh, pl.ds(g0, GDN_BLK), :] = (qk * e_in).astype(BF)
            ke_ref[d, hh, pl.ds(g0, GDN_BLK), :] = (k * jnp.exp(tot - g + lb)).astype(BF)
            ge0 = _aligned((g0 // CHUNK) * 8, 16)
            ge_ref[d, hh, pl.ds(ge0, 16), :] = jnp.concatenate(
                [jnp.broadcast_to(jnp.exp(tot_lo), (8, LANE)), jnp.broadcast_to(jnp.exp(tot_hi), (8, LANE))], axis=0)
            if p % 4 == 3:
                yield

    def phase2(fwd_rows, bwd_rows):
        steps = []
        for gf, gb in zip(fwd_rows, bwd_rows):
            steps.append((gf, gb + CHUNK))
            steps.append((gf + CHUNK, gb))
        state = {(d, hh): st_ref[d, hh] for d in range(2) for hh in range(2)}
        outs = []
        for cf, cb in steps:
            chains = [(d, hh, _aligned(c, CHUNK)) for d, c in ((0, cf), (1, cb)) for hh in range(2)]
            wss = [_dot(wq_ref[d, hh, pl.ds(_aligned(2 * g0, 2 * CHUNK), 2 * CHUNK), :],
                        state[d, hh].astype(BF)) for (d, hh, g0) in chains]
            dbs = [(u_ref[d, hh, pl.ds(g0, CHUNK), :] - ws[0:CHUNK, :]).astype(BF)
                   for (d, hh, g0), ws in zip(chains, wss)]
            upd = [_dot_tn(ke_ref[d, hh, pl.ds(g0, CHUNK), :], db) for (d, hh, g0), db in zip(chains, dbs)]
            for (d, hh, g0), up in zip(chains, upd):
                ge = ge_ref[d, hh, pl.ds(_aligned((g0 // CHUNK) * 8, 8), 1), :]
                state[d, hh] = state[d, hh] * ge[:, 0:1] + up
            outs.append((chains, wss, dbs))
            yield
        for (d, hh), st in state.items():
            st_ref[d, hh] = st
        for chains, wss, dbs in outs:
            for (d, hh, g0), ws, db in zip(chains, wss, dbs):
                o = ws[CHUNK:2 * CHUNK, :] + _dot(a_ref[d, hh, pl.ds(g0, CHUNK), :],
                                                  jnp.concatenate([db, db], axis=0))
                oref = of_ref if d == 0 else ob_ref
                oref[pl.ds(g0, CHUNK), hh * LANE:(hh + 1) * LANE] = o
            yield

    def run(*gens):
        gens = list(gens)
        while gens:
            for gen in list(gens):
                try:
                    next(gen)
                except StopIteration:
                    gens.remove(gen)

    def seg_rows(base, nblk, first, count):
        fwd = [_aligned(base + (first + j) * GDN_BLK, GDN_BLK) for j in range(count)]
        bwd = [_aligned(base + (nblk - 1 - first - j) * GDN_BLK, GDN_BLK) for j in range(count)]
        return fwd, bwd

    nb_ctx = CT // GDN_BLK
    nb_lat = S // GDN_BLK
    grp = 4 if nb_lat % 4 == 0 else (2 if nb_lat % 2 == 0 else 1)
    n_grp = nb_lat // grp
    st_ref[...] = jnp.zeros_like(st_ref)

    ctx_rows = seg_rows(0, nb_ctx, 0, nb_ctx)
    run(phase1(*ctx_rows))
    run(phase1(*seg_rows(CT, nb_lat, 0, grp)), phase2(*ctx_rows))

    def stage(i, carry):
        run(phase1(*seg_rows(CT, nb_lat, i * grp, grp)), phase2(*seg_rows(CT, nb_lat, (i - 1) * grp, grp)))
        return carry

    lax.fori_loop(1, n_grp, stage, 0)
    run(phase2(*seg_rows(CT, nb_lat, (n_grp - 1) * grp, grp)))

    nw = nw_ref[...]
    for hh in range(2):
        sl = slice(hh * LANE, (hh + 1) * LANE)
        o = of_ref[CT:CT + S, sl] + ob_ref[CT:CT + S, sl]
        ol_ref[:, sl] = _finish_rms(o, nw, zl_ref[:, sl]).astype(ol_ref.dtype)
        if need_ctx:
            o = of_ref[0:CT, sl] + ob_ref[0:CT, sl]
            oc_ref[:, sl] = _finish_rms(o, nw, zc_ref[:, sl]).astype(oc_ref.dtype)


def _gdn(p_l, ps_l, p_c, ps_c, conv_w, a_log, dt_bias, o_norm, need_ctx):
    B, S, _ = p_l.shape
    CT = p_c.shape[1]
    T = CT + S
    nch = T // CHUNK
    alog = jnp.zeros((1, N_SMALL), F32).at[0, SM_A:SM_A + 2 * N_HEADS].set(a_log.reshape(-1).astype(F32))
    dtb = jnp.zeros((1, N_SMALL), F32).at[0, SM_A:SM_A + 2 * N_HEADS].set(dt_bias.reshape(-1).astype(F32))

    def spec(n, width, off):
        return pl.BlockSpec((None, n, width), lambda b, p: (b, 0, off(p)))

    in_specs = []
    for n in (CT, S):
        in_specs += [spec(n, 2 * LANE, lambda p: GDN_Q // 2 + p), spec(n, 2 * LANE, lambda p: GDN_K // 2 + p),
                     spec(n, 2 * LANE, lambda p: GDN_V // 2 + p), spec(n, 2 * LANE, lambda p: GDN_Z // 2 + p),
                     spec(n, N_SMALL, lambda p: 0)]
    cw = lambda part: pl.BlockSpec((3, 2 * LANE), lambda b, p: (0, 2 * part + p))
    vec = pl.BlockSpec((1, LANE), lambda b, p: (0, 0))
    in_specs += [cw(0), cw(1), cw(2), vec, vec, vec]
    out_specs = [pl.BlockSpec((None, S, 2 * LANE), lambda b, p: (b, 0, p))]
    out_shape = [jax.ShapeDtypeStruct((B, S, GROUP_W), BF)]
    if need_ctx:
        out_specs.append(pl.BlockSpec((None, CT, 2 * LANE), lambda b, p: (b, 0, p)))
        out_shape.append(jax.ShapeDtypeStruct((B, CT, GROUP_W), BF))
    res = pl.pallas_call(
        functools.partial(_gdn_kernel, need_ctx=need_ctx, nc_ctx=CT // CHUNK, nc_lat=S // CHUNK),
        grid=(B, 2),
        in_specs=in_specs,
        out_specs=out_specs,
        out_shape=out_shape,
        scratch_shapes=[pltpu.VMEM((T, 2 * LANE), F32),
                        pltpu.VMEM((T, 2 * LANE), F32),
                        pltpu.VMEM((T, 2 * LANE), BF),
                        pltpu.VMEM((T, N_SMALL), F32),
                        pltpu.VMEM((2, 2, T, LANE), F32),
                        pltpu.VMEM((2, 2, 2 * T, LANE), BF),
                        pltpu.VMEM((2, 2, T, GDN_BLK), BF),
                        pltpu.VMEM((2, 2, T, LANE), BF),
                        pltpu.VMEM((2, 2, nch * 8, LANE), F32),
                        pltpu.VMEM((2, 2, LANE, LANE), F32),
                        pltpu.VMEM((T, 2 * LANE), F32),
                        pltpu.VMEM((T, 2 * LANE), F32)],
        compiler_params=_cp(("parallel", "parallel")),
        name="gdn",
    )(p_c, p_c, p_c, p_c, ps_c, p_l, p_l, p_l, p_l, ps_l,
      conv_w, conv_w, conv_w, alog, dtb, o_norm.reshape(1, LANE))
    return (res[1] if need_ctx else None), res[0]


def _align_w_in(w):
    big = jnp.concatenate([w[..., 0:3072], w[..., 3104:5152], w[..., 5168:6704]], axis=-1)
    small = jnp.concatenate([w[..., 3072:3104], w[..., 5152:5168],
                             jnp.zeros(w.shape[:-1] + (N_SMALL - 48,), w.dtype)], axis=-1)
    return big.astype(BF), small.astype(BF)


def _pick(n, prefs):
    for p in prefs:
        if n % p == 0:
            return p
    return n


def kernel(x, c, ctx, c_ctx, ada_w, ada_b, norm1_w, norm2_w, w_in, w_out, na_q_norm, na_k_norm, na_rpb,
           gla_gate_up, gla_gate_b, gla_o_norm, gdn_conv_w, gdn_a_log, gdn_dt_bias, gdn_o_norm,
           ret_decay_logit, ret_gn_w, mlp_w1, mlp_w2):
    B, S, D = x.shape
    CT = ctx.shape[1]
    depth = ada_w.shape[0]
    R = ((B + 1 + 7) // 8) * 8
    cc = jnp.concatenate([c, c_ctx[None, :], jnp.zeros((R - B - 1, D), F32)], axis=0)
    mod_all = _ada(cc, ada_w, ada_b).reshape(depth, R, 6, D)
    na_tables = _natten_tables(na_rpb, S)

    tm_l = _pick(S, (1024, 512, 256))
    tm_c = _pick(B * CT, (1024, 512, 256))
    tn = _pick(N_BIG, (1664, 512, 256, 128))
    tm_o = _pick(S, (512, 256))
    tm_oc = _pick(B * CT, (512, 256))
    th = _pick(mlp_w1.shape[2], (1024, 512, 256))

    w_big, w_small = _align_w_in(w_in)
    wo = w_out.astype(BF)
    w1 = mlp_w1.astype(BF)
    w2 = mlp_w2.astype(BF)

    xl = x
    xc = ctx.reshape(1, B * CT, D)
    for layer in range(depth):
        need_ctx = layer < depth - 1
        mod = mod_all[layer]
        nw1 = norm1_w[layer].reshape(1, D)
        nw2 = norm2_w[layer].reshape(1, D)

        p_l, ps_l = _in_proj(xl, mod, None, nw1, w_big, w_small, layer, tm_l, tn)
        p_c, ps_c = _in_proj(xc, mod, B, nw1, w_big, w_small, layer, tm_c, tn)
        p_c = p_c.reshape(B, CT, N_BIG)
        ps_c = ps_c.reshape(B, CT, N_SMALL)

        na_c, na_l = _natten(p_l, p_c, na_q_norm[layer], na_k_norm[layer], na_tables, layer, need_ctx)
        gl_c, gl_l = _gla(p_l, ps_l, p_c, ps_c, gla_gate_up[layer], gla_gate_b[layer], gla_o_norm[layer], need_ctx)
        gd_c, gd_l = _gdn(p_l, ps_l, p_c, ps_c, gdn_conv_w[layer], gdn_a_log[layer], gdn_dt_bias[layer],
                          gdn_o_norm[layer], need_ctx)
        rt_c, rt_l = _ret(p_l, p_c, ret_decay_logit[layer], ret_gn_w[layer], need_ctx)

        xl, hl = _out_proj(xl, (na_l, gl_l, gd_l, rt_l), wo, layer, mod, None, nw2, tm_o)
        xl = _mlp(xl, hl, mod, None, w1, w2, layer, tm_o, th)
        if need_ctx:
            ys = tuple(t.reshape(1, B * CT, GROUP_W) for t in (na_c, gl_c, gd_c, rt_c))
            xc, hc = _out_proj(xc, ys, wo, layer, mod, B, nw2, tm_oc)
            xc = _mlp(xc, hc, mod, B, w1, w2, layer, tm_oc, th)
    return xl
```

```python
import functools

import numpy as np
import jax
import jax.numpy as jnp
from jax import lax
from jax.experimental import pallas as pl
from jax.experimental.pallas import tpu as pltpu

BF = jnp.bfloat16
F32 = jnp.float32

N_HEADS = 4
HEAD_DIM = 128
GROUP_W = N_HEADS * HEAD_DIM
GRID_W = 64
NA_WIN_ROWS = 8
NA_WIN_COLS = 16
NA_GROUP = 4
GLA_DK = 64
GLA_GATE_RANK = 16
GLA_GATE_TAU = 16.0
GDN_DK = 128
RET_DK = 64
ROPE_BASE = 10000.0
CHUNK = 64
GDN_BLK = 2 * CHUNK
EPS = 1e-6
NEG_INF = -1e30

LANE = 128
N_BIG = 52 * LANE
N_SMALL = LANE
NA_Q, NA_K, NA_V = 0, 4, 8
GLA_Q, GLA_K, GLA_V, GLA_G = 12, 14, 16, 20
GDN_Q, GDN_K, GDN_V, GDN_Z = 24, 28, 32, 36
RET_Q, RET_K, RET_V, RET_G = 40, 42, 44, 48
SM_RK, SM_A, SM_BT = 0, 32, 40

VMEM_LIMIT = 56 * 1024 * 1024


def _cp(sem, vmem=VMEM_LIMIT):
    return pltpu.CompilerParams(dimension_semantics=sem, vmem_limit_bytes=vmem)


def _aligned(x, m):
    return x if isinstance(x, int) else pl.multiple_of(x, m)


def _dot(a, b):
    return jnp.dot(a, b, preferred_element_type=F32)


def _dot_nt(a, b):
    return lax.dot_general(a, b, (((1,), (1,)), ((), ())), preferred_element_type=F32)


def _dot_tn(a, b):
    return lax.dot_general(a, b, (((0,), (0,)), ((), ())), preferred_element_type=F32)


def _split3(x):
    hi = x.astype(BF)
    r1 = x - hi.astype(F32)
    mid = r1.astype(BF)
    lo = (r1 - mid.astype(F32)).astype(BF)
    return hi, mid, lo


def _exact_dot(m_bf, x):
    hi, mid, lo = _split3(x)
    n = x.shape[1]
    r = _dot(m_bf, jnp.concatenate([hi, mid, lo], axis=1))
    return r[:, 0:n] + r[:, n:2 * n] + r[:, 2 * n:3 * n]


def _sigmoid(x):
    return 1.0 / (1.0 + jnp.exp(-x))


def _silu(x):
    return x * _sigmoid(x)


def _log_sigmoid(x):
    return jnp.minimum(x, 0.0) - jnp.log(1.0 + jnp.exp(-jnp.abs(x)))


def _softplus(x):
    return jnp.maximum(x, 0.0) + jnp.log(1.0 + jnp.exp(-jnp.abs(x)))


def _ln_mod(x, nw, shift, scale):
    ms = jnp.mean(x * x, axis=-1, keepdims=True)
    return (x * lax.rsqrt(ms + EPS) * nw) * (1.0 + scale) + shift


def _tri_consts():
    ri = lax.broadcasted_iota(jnp.int32, (CHUNK, CHUNK), 0)
    ci = lax.broadcasted_iota(jnp.int32, (CHUNK, CHUNK), 1)
    incl = (ri >= ci, ri <= ci)
    strict = (ri > ci, ri < ci)
    tri_bf = tuple(jnp.where(m, 1.0, 0.0).astype(BF) for m in incl)
    return ri, ci, incl, strict, tri_bf


def _ada_kernel(c_ref, w_ref, b_ref, o_ref):
    sc = _silu(c_ref[...]).astype(BF)
    o_ref[...] = _dot(sc, w_ref[...].astype(BF)) + b_ref[...]


def _ada(cc, ada_w, ada_b):
    L, D, N6 = ada_w.shape
    R = cc.shape[0]
    tn = _pick(N6, (2048, 1024))
    return pl.pallas_call(
        _ada_kernel,
        grid=(L, N6 // tn),
        in_specs=[pl.BlockSpec((R, D), lambda l, j: (0, 0)),
                  pl.BlockSpec((None, D, tn), lambda l, j: (l, 0, j)),
                  pl.BlockSpec((None, 1, tn), lambda l, j: (l, 0, j))],
        out_specs=pl.BlockSpec((None, R, tn), lambda l, j: (l, 0, j)),
        out_shape=jax.ShapeDtypeStruct((L, R, N6), F32),
        compiler_params=_cp(("parallel", "parallel")),
        name="ada_ln",
    )(cc, ada_w, ada_b.reshape(L, 1, N6))


def _inproj_kernel(x_ref, mod_ref, nw_ref, w_ref, ws_ref, o_ref, os_ref, h_ref):
    @pl.when(pl.program_id(1) == 0)
    def _():
        h = _ln_mod(x_ref[...], nw_ref[...], mod_ref[0:1, :], mod_ref[1:2, :]).astype(BF)
        h_ref[...] = h
        os_ref[...] = _dot(h, ws_ref[...])

    o_ref[...] = _dot(h_ref[...], w_ref[...]).astype(o_ref.dtype)


def _in_proj(x3, mod, const_row, nw, w_big, w_small, layer, tm, tn):
    Bn, Tn, D = x3.shape
    nt = Tn // tm
    if const_row is None:
        mod_map = lambda i, j: (i // nt, 0, 0)
    else:
        mod_map = lambda i, j: (const_row, 0, 0)
    return pl.pallas_call(
        _inproj_kernel,
        grid=(Bn * nt, N_BIG // tn),
        in_specs=[pl.BlockSpec((None, tm, D), lambda i, j: (i // nt, i % nt, 0)),
                  pl.BlockSpec((None, 6, D), mod_map),
                  pl.BlockSpec((1, D), lambda i, j: (0, 0)),
                  pl.BlockSpec((None, D, tn), lambda i, j: (layer, 0, j)),
                  pl.BlockSpec((None, D, N_SMALL), lambda i, j: (layer, 0, 0))],
        out_specs=[pl.BlockSpec((None, tm, tn), lambda i, j: (i // nt, i % nt, j)),
                   pl.BlockSpec((None, tm, N_SMALL), lambda i, j: (i // nt, i % nt, 0))],
        out_shape=[jax.ShapeDtypeStruct((Bn, Tn, N_BIG), BF),
                   jax.ShapeDtypeStruct((Bn, Tn, N_SMALL), F32)],
        scratch_shapes=[pltpu.VMEM((tm, D), BF)],
        compiler_params=_cp(("parallel", "arbitrary")),
        name="in_proj",
    )(x3, mod, nw, w_big, w_small)


def _outproj_kernel(x_ref, y0, y1, y2, y3, w_ref, mod_ref, nw_ref, o_ref, h_ref):
    acc = _dot(y0[...], w_ref[0 * GROUP_W:1 * GROUP_W, :])
    acc += _dot(y1[...], w_ref[1 * GROUP_W:2 * GROUP_W, :])
    acc += _dot(y2[...], w_ref[2 * GROUP_W:3 * GROUP_W, :])
    acc += _dot(y3[...], w_ref[3 * GROUP_W:4 * GROUP_W, :])
    xn = x_ref[...] + mod_ref[2:3, :] * acc
    o_ref[...] = xn
    h_ref[...] = _ln_mod(xn, nw_ref[...], mod_ref[3:4, :], mod_ref[4:5, :]).astype(BF)


def _out_proj(x3, ys, w_out, layer, mod, const_row, nw2, tm):
    Bn, Tn, D = x3.shape
    nt = Tn // tm
    if const_row is None:
        mod_map = lambda i: (i // nt, 0, 0)
    else:
        mod_map = lambda i: (const_row, 0, 0)
    row_map = lambda i: (i // nt, i % nt, 0)
    return pl.pallas_call(
        _outproj_kernel,
        grid=(Bn * nt,),
        in_specs=[pl.BlockSpec((None, tm, D), row_map)]
                 + [pl.BlockSpec((None, tm, GROUP_W), row_map)] * 4
                 + [pl.BlockSpec((None, 4 * GROUP_W, D), lambda i: (layer, 0, 0)),
                    pl.BlockSpec((None, 6, D), mod_map),
                    pl.BlockSpec((1, D), lambda i: (0, 0))],
        out_specs=[pl.BlockSpec((None, tm, D), row_map), pl.BlockSpec((None, tm, D), row_map)],
        out_shape=[jax.ShapeDtypeStruct((Bn, Tn, D), F32), jax.ShapeDtypeStruct((Bn, Tn, D), BF)],
        compiler_params=_cp(("parallel",)),
        name="out_proj",
    )(x3, *ys, w_out, mod, nw2)


def _mlp_kernel(x_ref, h_ref, mod_ref, w1_ref, w2_ref, o_ref, *, nk):
    k = pl.program_id(1)

    @pl.when(k == 0)
    def _():
        o_ref[...] = jnp.zeros_like(o_ref)

    hid = jnp.maximum(_dot(h_ref[...], w1_ref[...]), 0.0)
    o_ref[...] += _dot((hid * hid).astype(BF), w2_ref[...])

    @pl.when(k == nk - 1)
    def _():
        o_ref[...] = x_ref[...] + mod_ref[5:6, :] * o_ref[...]


def _mlp(x3, h3, mod, const_row, w1, w2, layer, tm, th):
    Bn, Tn, D = x3.shape
    Hd = w1.shape[2]
    nt = Tn // tm
    nk = Hd // th
    if const_row is None:
        mod_map = lambda i, k: (i // nt, 0, 0)
    else:
        mod_map = lambda i, k: (const_row, 0, 0)
    row_map = lambda i, k: (i // nt, i % nt, 0)
    return pl.pallas_call(
        functools.partial(_mlp_kernel, nk=nk),
        grid=(Bn * nt, nk),
        in_specs=[pl.BlockSpec((None, tm, D), row_map),
                  pl.BlockSpec((None, tm, D), row_map),
                  pl.BlockSpec((None, 6, D), mod_map),
                  pl.BlockSpec((None, D, th), lambda i, k: (layer, 0, k)),
                  pl.BlockSpec((None, th, D), lambda i, k: (layer, k, 0))],
        out_specs=pl.BlockSpec((None, tm, D), row_map),
        out_shape=jax.ShapeDtypeStruct((Bn, Tn, D), F32),
        compiler_params=_cp(("parallel", "arbitrary")),
        name="mlp",
    )(x3, h3, mod, w1, w2)


def _rms_head(x, w):
    x = x.astype(F32)
    return x * lax.rsqrt(jnp.mean(x * x, axis=-1, keepdims=True) + EPS) * w


def _natten_kernel(geo_ref, ql_ref, kl_ref, vl_ref, qc_ref, kc_ref, vc_ref, qw_ref, kw_ref, bias_ref, *rest,
                   need_ctx, rows, grp, span):
    if need_ctx:
        ol_ref, oc_ref, qs, ks = rest
    else:
        ol_ref, qs, ks = rest
    scale = HEAD_DIM ** -0.5
    qw = qw_ref[...]
    kw = kw_ref[...]
    qs[...] = (_rms_head(ql_ref[...], qw) * scale).astype(BF)
    ks[...] = _rms_head(kl_ref[...], kw).astype(BF)
    kc = _rms_head(kc_ref[...], kw).astype(BF)
    vc = vc_ref[...]
    if need_ctx:
        qc = (_rms_head(qc_ref[...], qw) * scale).astype(BF)
        s = _dot_nt(qc, kc)
        p = jnp.exp(s - jnp.max(s, axis=-1, keepdims=True))
        l = jnp.sum(p, axis=-1, keepdims=True)
        oc_ref[...] = (_dot(p.astype(BF), vc) / l).astype(oc_ref.dtype)

    nq = grp * GRID_W
    nk = span * GRID_W

    ngroups = rows // grp
    per_iter = 2 if ngroups % 2 == 0 else 1

    def body(it, carry):
        geo = []
        for j in range(per_iter):
            g = it * per_iter + j
            geo.append((pl.multiple_of(g * nq, nq), pl.multiple_of(geo_ref[0, g] * GRID_W, GRID_W), geo_ref[1, g]))
        qv = [qs[pl.ds(q0, nq), :] for (q0, k0, t) in geo]
        sws = [_dot_nt(q, ks[pl.ds(k0, nk), :]) + bias_ref[t] for q, (q0, k0, t) in zip(qv, geo)]
        scs = [_dot_nt(q, kc) for q in qv]
        probs = []
        for sw, sc in zip(sws, scs):
            m = jnp.maximum(jnp.max(sw, axis=-1, keepdims=True), jnp.max(sc, axis=-1, keepdims=True))
            pw = jnp.exp(sw - m)
            pc = jnp.exp(sc - m)
            l = jnp.sum(pw, axis=-1, keepdims=True) + jnp.sum(pc, axis=-1, keepdims=True)
            probs.append((pw.astype(BF), pc.astype(BF), l))
        for (pw, pc, l), (q0, k0, t) in zip(probs, geo):
            o = (_dot(pw, vl_ref[pl.ds(k0, nk), :]) + _dot(pc, vc)) / l
            ol_ref[pl.ds(q0, nq), :] = o.astype(ol_ref.dtype)
        return carry

    lax.fori_loop(0, ngroups // per_iter, body, 0)


def _natten_geometry(rows, kh):
    grp = NA_GROUP if rows % NA_GROUP == 0 else 1
    span = min(kh + grp - 1, rows)
    starts, type_ids, types = [], [], []
    for g in range(rows // grp):
        rs = [int(np.clip(g * grp + j - kh // 2, 0, rows - kh)) for j in range(grp)]
        us = int(np.clip(rs[0], 0, rows - span))
        sig = tuple((rs[j] - us, g * grp + j - rs[j]) for j in range(grp))
        assert all(0 <= off <= span - kh for off, _ in sig)
        if sig not in types:
            types.append(sig)
        starts.append(us)
        type_ids.append(types.index(sig))
    return grp, span, np.asarray([starts, type_ids], np.int32), types


def _natten_bias(rpb, kh, span, types):
    q = np.arange(GRID_W)[:, None]
    kc = np.arange(GRID_W)[None, :]
    col_off = np.clip(kc - q, -(NA_WIN_COLS - 1), NA_WIN_COLS - 1) + NA_WIN_COLS - 1
    onehot = (col_off[..., None] == np.arange(2 * NA_WIN_COLS - 1)).astype(np.float32)
    toe = jnp.einsum('lhrc,qkc->lhrqk', rpb.astype(F32), jnp.asarray(onehot), precision=lax.Precision.HIGHEST)
    cs = np.clip(q - NA_WIN_COLS // 2, 0, GRID_W - NA_WIN_COLS)
    valid = (kc >= cs) & (kc < cs + NA_WIN_COLS)
    toe = jnp.where(valid[None, None, None], toe, NEG_INF)
    L, H = rpb.shape[:2]
    pad = jnp.full((L, H, span, GRID_W, GRID_W), NEG_INF, F32)
    toe = jnp.concatenate([pad, toe, pad], axis=2).transpose(0, 1, 3, 2, 4)
    slabs, keep = [], []
    for sig in types:
        for off, d in sig:
            start = span + NA_WIN_ROWS - 1 - d - off
            slabs.append(toe[:, :, :, start:start + span])
            keep.append([0 <= i - off < kh for i in range(span)])
    slab = jnp.stack(slabs, axis=2)
    slab = jnp.where(np.asarray(keep)[None, None, :, None, :, None], slab, NEG_INF)
    grp = len(types[0])
    return slab.reshape(L, H, len(types), grp * GRID_W, span * GRID_W)


def _natten_tables(rpb_all, S):
    rows = S // GRID_W
    kh = min(NA_WIN_ROWS, rows)
    grp, span, geo, types = _natten_geometry(rows, kh)
    return grp, span, geo, _natten_bias(rpb_all, kh, span, types)


def _natten(p_l, p_c, qw, kw, tables, layer, need_ctx):
    B, S, _ = p_l.shape
    CT = p_c.shape[1]
    rows = S // GRID_W
    grp, span, geo, bias_all = tables
    n_types = bias_all.shape[2]
    lat = lambda off: pl.BlockSpec((None, S, LANE), lambda b, h: (b, 0, off + h))
    ctx = lambda off: pl.BlockSpec((None, CT, LANE), lambda b, h: (b, 0, off + h))
    vec = pl.BlockSpec((1, LANE), lambda b, h: (0, 0))
    out_specs = [pl.BlockSpec((None, S, LANE), lambda b, h: (b, 0, h))]
    out_shape = [jax.ShapeDtypeStruct((B, S, GROUP_W), BF)]
    if need_ctx:
        out_specs.append(pl.BlockSpec((None, CT, LANE), lambda b, h: (b, 0, h)))
        out_shape.append(jax.ShapeDtypeStruct((B, CT, GROUP_W), BF))
    res = pl.pallas_call(
        functools.partial(_natten_kernel, need_ctx=need_ctx, rows=rows, grp=grp, span=span),
        grid=(B, N_HEADS),
        in_specs=[pl.BlockSpec(memory_space=pltpu.SMEM),
                  lat(NA_Q), lat(NA_K), lat(NA_V), ctx(NA_Q), ctx(NA_K), ctx(NA_V), vec, vec,
                  pl.BlockSpec((None, None, n_types, grp * GRID_W, span * GRID_W),
                               lambda b, h: (layer, h, 0, 0, 0))],
        out_specs=out_specs,
        out_shape=out_shape,
        scratch_shapes=[pltpu.VMEM((S, LANE), BF), pltpu.VMEM((S, LANE), BF)],
        compiler_params=_cp(("parallel", "parallel")),
        name="natten",
    )(jnp.asarray(geo), p_l, p_l, p_l, p_c, p_c, p_c, qw.reshape(1, LANE), kw.reshape(1, LANE), bias_all)
    return (res[1] if need_ctx else None), res[0]


def _pair_consts():
    lane = lax.broadcasted_iota(jnp.int32, (CHUNK, LANE), 1)
    row = lax.broadcasted_iota(jnp.int32, (CHUNK, LANE), 0)
    col = lane & (CHUNK - 1)
    hmask = (lane < CHUNK, lane >= CHUNK)
    incl2 = (row >= col, row <= col)
    r2 = lax.broadcasted_iota(jnp.int32, (LANE, 2 * LANE), 0)
    c2 = lax.broadcasted_iota(jnp.int32, (LANE, 2 * LANE), 1)
    bmask = (r2 < CHUNK) == (c2 < LANE)
    eye = lax.broadcasted_iota(jnp.int32, (LANE, LANE), 0) == lax.broadcasted_iota(jnp.int32, (LANE, LANE), 1)
    return hmask, incl2, bmask, eye


def _finish_rms(o, nw, g):
    y = o * lax.rsqrt(jnp.mean(o * o, axis=-1, keepdims=True) + EPS) * nw
    return y * _silu(g.astype(F32))


def _gla_kernel(qc_ref, kc_ref, vc_ref, gc_ref, sc_ref, ql_ref, kl_ref, vl_ref, gl_ref, sl_ref,
                gup_ref, gb_ref, nw_ref, *rest, need_ctx, nc_ctx, nc_lat):
    if need_ctx:
        ol_ref, oc_ref, lg_ref, st_ref, of_ref, ob_ref = rest
    else:
        ol_ref, lg_ref, st_ref, of_ref, ob_ref = rest
        oc_ref = None
    CT = nc_ctx * CHUNK
    S = nc_lat * CHUNK
    _, _, _, _, tri_bf = _tri_consts()
    hmask, incl2, bmask, eye = _pair_consts()
    zero_v = jnp.zeros((CHUNK, LANE), BF)

    for d in range(2):
        for (s_ref, r0, n) in ((sc_ref, 0, CT), (sl_ref, CT, S)):
            z = _dot(s_ref[...].astype(BF), gup_ref[d]) + gb_ref[d]
            lg_ref[d, r0:r0 + n, :] = _log_sigmoid(z) * (1.0 / GLA_GATE_TAU)
    st_ref[...] = jnp.zeros_like(st_ref)

    def segment(q_ref, k_ref, v_ref, row0, nch, write):
        per_iter = next(p for p in (8, 4, 2, 1) if nch % p == 0)

        def body(it, carry):
            pre = []
            for j in range(per_iter):
                n = it * per_iter + j
                for d in range(2):
                    c = n if d == 0 else nch - 1 - n
                    r0 = pl.multiple_of(c * CHUNK, CHUNK)
                    pre.append((d, r0, q_ref[pl.ds(r0, CHUNK), :].astype(F32), k_ref[pl.ds(r0, CHUNK), :].astype(F32)))
            cums = [_exact_dot(tri_bf[d], lg_ref[d, pl.ds(row0 + r0, CHUNK), :]) for (d, r0, _, _) in pre]
            work = []
            for (d, r0, q2, k2), cum in zip(pre, cums):
                tot = cum[CHUNK - 1:CHUNK, :] if d == 0 else cum[0:1, :]
                ge_col = jnp.sum(jnp.where(eye, jnp.exp(tot), 0.0), axis=1, keepdims=True)
                ke = (k2 * jnp.exp(tot - cum)).astype(BF)
                vp = v_ref[pl.ds(r0, CHUNK), :]
                qd = kdm = None
                if write:
                    qd = (q2 * (jnp.exp(cum) * (GLA_DK ** -0.5))).astype(BF)
                    kd = k2 * jnp.exp(-cum)
                    kdm = jnp.concatenate([jnp.where(hmask[0], kd, 0.0), jnp.where(hmask[1], kd, 0.0)],
                                          axis=0).astype(BF)
                work.append((d, r0, vp, qd, kdm, ke, ge_col))
            if write:
                atts = [jnp.where(incl2[d], _dot_nt(qd, kdm), 0.0).astype(BF)
                        for (d, r0, vp, qd, kdm, ke, ge_col) in work]
                intras = [_dot(att, jnp.concatenate(
                              [jnp.concatenate([vp[:, 0:LANE], zero_v], axis=1),
                               jnp.concatenate([zero_v, vp[:, LANE:2 * LANE]], axis=1)], axis=0))
                          for att, (d, r0, vp, qd, kdm, ke, ge_col) in zip(atts, work)]
            upds = [jnp.where(bmask, _dot_tn(ke, vp), 0.0) for (d, r0, vp, qd, kdm, ke, ge_col) in work]
            state = {d: st_ref[d] for d in range(2)}
            for i, (d, r0, vp, qd, kdm, ke, ge_col) in enumerate(work):
                st = state[d]
                if write:
                    oref = of_ref if d == 0 else ob_ref
                    oref[pl.ds(row0 + r0, CHUNK), :] = intras[i] + _dot(qd, st.astype(BF))
                state[d] = st * ge_col + upds[i]
            for d, st in state.items():
                st_ref[d] = st
            return carry

        lax.fori_loop(0, nch // per_iter, body, 0)

    segment(qc_ref, kc_ref, vc_ref, 0, nc_ctx, need_ctx)
    segment(ql_ref, kl_ref, vl_ref, CT, nc_lat, True)

    nw = nw_ref[...]
    for hh in range(2):
        sl = slice(hh * LANE, (hh + 1) * LANE)
        o = of_ref[CT:CT + S, sl] + ob_ref[CT:CT + S, sl]
        ol_ref[:, sl] = _finish_rms(o, nw, gl_ref[:, sl]).astype(ol_ref.dtype)
        if need_ctx:
            o = of_ref[0:CT, sl] + ob_ref[0:CT, sl]
            oc_ref[:, sl] = _finish_rms(o, nw, gc_ref[:, sl]).astype(oc_ref.dtype)


def _gla(p_l, ps_l, p_c, ps_c, gate_up, gate_b, o_norm, need_ctx):
    B, S, _ = p_l.shape
    CT = p_c.shape[1]
    T = CT + S
    gup = jnp.zeros((2, N_SMALL, N_HEADS * GLA_DK), F32)
    for d in range(2):
        gup = gup.at[d, SM_RK + d * GLA_GATE_RANK:SM_RK + (d + 1) * GLA_GATE_RANK].set(gate_up[d])
    gup = gup.astype(BF)
    gb = gate_b.reshape(2, 1, N_HEADS * GLA_DK)

    def spec(n, width, off):
        return pl.BlockSpec((None, n, width), lambda b, p: (b, 0, off(p)))

    in_specs = []
    for n in (CT, S):
        in_specs += [spec(n, LANE, lambda p: GLA_Q + p), spec(n, LANE, lambda p: GLA_K + p),
                     spec(n, 2 * LANE, lambda p: GLA_V // 2 + p), spec(n, 2 * LANE, lambda p: GLA_G // 2 + p),
                     spec(n, N_SMALL, lambda p: 0)]
    in_specs += [pl.BlockSpec((2, N_SMALL, LANE), lambda b, p: (0, 0, p)),
                 pl.BlockSpec((2, 1, LANE), lambda b, p: (0, 0, p)),
                 pl.BlockSpec((1, LANE), lambda b, p: (0, 0))]
    out_specs = [pl.BlockSpec((None, S, 2 * LANE), lambda b, p: (b, 0, p))]
    out_shape = [jax.ShapeDtypeStruct((B, S, GROUP_W), BF)]
    if need_ctx:
        out_specs.append(pl.BlockSpec((None, CT, 2 * LANE), lambda b, p: (b, 0, p)))
        out_shape.append(jax.ShapeDtypeStruct((B, CT, GROUP_W), BF))
    res = pl.pallas_call(
        functools.partial(_gla_kernel, need_ctx=need_ctx, nc_ctx=CT // CHUNK, nc_lat=S // CHUNK),
        grid=(B, 2),
        in_specs=in_specs,
        out_specs=out_specs,
        out_shape=out_shape,
        scratch_shapes=[pltpu.VMEM((2, T, LANE), F32),
                        pltpu.VMEM((2, LANE, 2 * LANE), F32),
                        pltpu.VMEM((T, 2 * LANE), F32),
                        pltpu.VMEM((T, 2 * LANE), F32)],
        compiler_params=_cp(("parallel", "parallel")),
        name="gla",
    )(p_c, p_c, p_c, p_c, ps_c, p_l, p_l, p_l, p_l, ps_l, gup, gb, o_norm.reshape(1, LANE))
    return (res[1] if need_ctx else None), res[0]


def _rope_tables(S):
    pos = np.arange(S)
    half = RET_DK // 2
    quarter = half // 2
    freqs = ROPE_BASE ** (-np.arange(quarter, dtype=np.float64) / quarter)
    cos = np.zeros((S, LANE), np.float64)
    sin_dn = np.zeros((S, LANE), np.float64)
    sin_up = np.zeros((S, LANE), np.float64)
    for head in range(2):
        for part, p in enumerate((pos // GRID_W, pos % GRID_W)):
            ang = p[:, None].astype(np.float64) * freqs[None, :]
            base = head * RET_DK + part * half
            cos[:, base:base + quarter] = np.cos(ang)
            cos[:, base + quarter:base + half] = np.cos(ang)
            sin_dn[:, base:base + quarter] = -np.sin(ang)
            sin_up[:, base + quarter:base + half] = np.sin(ang)
    return tuple(jnp.asarray(t, F32) for t in (cos, sin_dn, sin_up))


def _ret_kernel(qc_ref, kc_ref, vc_ref, gc_ref, ql_ref, kl_ref, vl_ref, gl_ref,
                cos_ref, sdn_ref, sup_ref, dl_ref, gnw_ref, *rest, need_ctx, nc_ctx, nc_lat):
    if need_ctx:
        ol_ref, oc_ref, qr_ref, kr_ref, st_ref, of_ref, ob_ref = rest
    else:
        ol_ref, qr_ref, kr_ref, st_ref, of_ref, ob_ref = rest
        oc_ref = None
    CT = nc_ctx * CHUNK
    S = nc_lat * CHUNK
    hmask, incl2, bmask, _ = _pair_consts()
    zero_v = jnp.zeros((CHUNK, LANE), BF)
    pp = pl.program_id(1)

    def rope(x):
        quarter = RET_DK // 4
        return (x * cos_ref[...] + pltpu.roll(x, LANE - quarter, 1) * sdn_ref[...]
                + pltpu.roll(x, quarter, 1) * sup_ref[...])

    qr_ref[0:CT, :] = qc_ref[...].astype(F32)
    kr_ref[0:CT, :] = kc_ref[...].astype(F32) * (RET_DK ** -0.5)
    qr_ref[CT:CT + S, :] = rope(ql_ref[...].astype(F32))
    kr_ref[CT:CT + S, :] = rope(kl_ref[...].astype(F32) * (RET_DK ** -0.5))
    st_ref[...] = jnp.zeros_like(st_ref)

    dmat, qfac, kfac, gend = {}, {}, {}, {}
    rowf = lax.broadcasted_iota(jnp.int32, (CHUNK, LANE), 0).astype(F32)
    colf = (lax.broadcasted_iota(jnp.int32, (CHUNK, LANE), 1) & (CHUNK - 1)).astype(F32)
    lane_v = lax.broadcasted_iota(jnp.int32, (CHUNK, 2 * LANE), 1)
    row_v = lax.broadcasted_iota(jnp.int32, (CHUNK, 2 * LANE), 0).astype(F32)
    row_k = lax.broadcasted_iota(jnp.int32, (LANE, 1), 0)
    for d in range(2):
        lg_a = _log_sigmoid(dl_ref[pl.ds(d * N_HEADS + 2 * pp, 1), 0:1])
        lg_b = _log_sigmoid(dl_ref[pl.ds(d * N_HEADS + 2 * pp + 1, 1), 0:1])
        lg2 = jnp.where(hmask[0], lg_a, lg_b)
        dist = (rowf - colf) if d == 0 else (colf - rowf)
        dmat[d] = jnp.exp(jnp.where(incl2[d], dist * lg2, -jnp.inf))
        steps_v = (row_v + 1.0) if d == 0 else (CHUNK - row_v)
        qfac[d] = jnp.exp(steps_v * jnp.where(lane_v < LANE, lg_a, lg_b))
        steps_k = (rowf + 1.0) if d == 0 else (CHUNK - rowf)
        kfac[d] = jnp.exp((CHUNK - steps_k) * lg2)
        gend[d] = jnp.exp(CHUNK * jnp.where(row_k < CHUNK, lg_a, lg_b))

    def segment(v_ref, row0, nch, write):
        per_iter = next(p for p in (8, 4, 2, 1) if nch % p == 0)

        def body(it, carry):
            work = []
            for j in range(per_iter):
                n = it * per_iter + j
                for d in range(2):
                    c = n if d == 0 else nch - 1 - n
                    r0 = pl.multiple_of(c * CHUNK, CHUNK)
                    k2 = kr_ref[pl.ds(row0 + r0, CHUNK), :]
                    vp = v_ref[pl.ds(r0, CHUNK), :]
                    qb = km = None
                    if write:
                        qb = qr_ref[pl.ds(row0 + r0, CHUNK), :].astype(BF)
                        km = jnp.concatenate([jnp.where(hmask[0], k2, 0.0), jnp.where(hmask[1], k2, 0.0)],
                                             axis=0).astype(BF)
                    work.append((d, r0, vp, qb, km, (k2 * kfac[d]).astype(BF)))
            if write:
                atts = [(_dot_nt(qb, km) * dmat[d]).astype(BF) for (d, r0, vp, qb, km, ke) in work]
                intras = [_dot(att, jnp.concatenate(
                              [jnp.concatenate([vp[:, 0:LANE], zero_v], axis=1),
                               jnp.concatenate([zero_v, vp[:, LANE:2 * LANE]], axis=1)], axis=0))
                          for att, (d, r0, vp, qb, km, ke) in zip(atts, work)]
            upds = [jnp.where(bmask, _dot_tn(ke, vp), 0.0) for (d, r0, vp, qb, km, ke) in work]
            state = {d: st_ref[d] for d in range(2)}
            for i, (d, r0, vp, qb, km, ke) in enumerate(work):
                st = state[d]
                if write:
                    oref = of_ref if d == 0 else ob_ref
                    oref[pl.ds(row0 + r0, CHUNK), :] = intras[i] + _dot(qb, st.astype(BF)) * qfac[d]
                state[d] = st * gend[d] + upds[i]
            for d, st in state.items():
                st_ref[d] = st
            return carry

        lax.fori_loop(0, nch // per_iter, body, 0)

    segment(vc_ref, 0, nc_ctx, need_ctx)
    segment(vl_ref, CT, nc_lat, True)

    def finish(o, w, g):
        mu = jnp.mean(o, axis=-1, keepdims=True)
        oc = o - mu
        var = jnp.mean(oc * oc, axis=-1, keepdims=True)
        return oc * lax.rsqrt(var + EPS) * w * _silu(g.astype(F32))

    for hh in range(2):
        sl = slice(hh * LANE, (hh + 1) * LANE)
        w = gnw_ref[:, sl]
        o = of_ref[CT:CT + S, sl] + ob_ref[CT:CT + S, sl]
        ol_ref[:, sl] = finish(o, w, gl_ref[:, sl]).astype(ol_ref.dtype)
        if need_ctx:
            o = of_ref[0:CT, sl] + ob_ref[0:CT, sl]
            oc_ref[:, sl] = finish(o, w, gc_ref[:, sl]).astype(oc_ref.dtype)


def _ret(p_l, p_c, decay_logit, gn_w, need_ctx):
    B, S, _ = p_l.shape
    CT = p_c.shape[1]
    T = CT + S
    cos, sdn, sup = _rope_tables(S)
    dl = jnp.broadcast_to(decay_logit.reshape(2 * N_HEADS, 1).astype(F32), (2 * N_HEADS, LANE))

    def spec(n, width, off):
        return pl.BlockSpec((None, n, width), lambda b, p: (b, 0, off(p)))

    in_specs = []
    for n in (CT, S):
        in_specs += [spec(n, LANE, lambda p: RET_Q + p), spec(n, LANE, lambda p: RET_K + p),
                     spec(n, 2 * LANE, lambda p: RET_V // 2 + p), spec(n, 2 * LANE, lambda p: RET_G // 2 + p)]
    tab = pl.BlockSpec((S, LANE), lambda b, p: (0, 0))
    in_specs += [tab, tab, tab,
                 pl.BlockSpec((2 * N_HEADS, LANE), lambda b, p: (0, 0)),
                 pl.BlockSpec((1, 2 * LANE), lambda b, p: (0, p))]
    out_specs = [pl.BlockSpec((None, S, 2 * LANE), lambda b, p: (b, 0, p))]
    out_shape = [jax.ShapeDtypeStruct((B, S, GROUP_W), BF)]
    if need_ctx:
        out_specs.append(pl.BlockSpec((None, CT, 2 * LANE), lambda b, p: (b, 0, p)))
        out_shape.append(jax.ShapeDtypeStruct((B, CT, GROUP_W), BF))
    res = pl.pallas_call(
        functools.partial(_ret_kernel, need_ctx=need_ctx, nc_ctx=CT // CHUNK, nc_lat=S // CHUNK),
        grid=(B, 2),
        in_specs=in_specs,
        out_specs=out_specs,
        out_shape=out_shape,
        scratch_shapes=[pltpu.VMEM((T, LANE), F32),
                        pltpu.VMEM((T, LANE), F32),
                        pltpu.VMEM((2, LANE, 2 * LANE), F32),
                        pltpu.VMEM((T, 2 * LANE), F32),
                        pltpu.VMEM((T, 2 * LANE), F32)],
        compiler_params=_cp(("parallel", "parallel")),
        name="retention",
    )(p_c, p_c, p_c, p_c, p_l, p_l, p_l, p_l, cos, sdn, sup, dl, gn_w.reshape(1, GROUP_W))
    return (res[1] if need_ctx else None), res[0]


def _gdn_kernel(qc_ref, kc_ref, vc_ref, zc_ref, sc_ref, ql_ref, kl_ref, vl_ref, zl_ref, sl_ref,
                cwq_ref, cwk_ref, cwv_ref, alog_ref, dtb_ref, nw_ref, *rest, need_ctx, nc_ctx, nc_lat):
    if need_ctx:
        ol_ref, oc_ref = rest[:2]
        rest = rest[2:]
    else:
        ol_ref = rest[0]
        oc_ref = None
        rest = rest[1:]
    qs, ks, vs, sm_ref, u_ref, wq_ref, a_ref, ke_ref, ge_ref, st_ref, of_ref, ob_ref = rest
    CT = nc_ctx * CHUNK
    S = nc_lat * CHUNK
    pp = pl.program_id(1)
    sm_ref[0:CT, :] = sc_ref[...]
    sm_ref[CT:CT + S, :] = sl_ref[...]
    ri = lax.broadcasted_iota(jnp.int32, (GDN_BLK, GDN_BLK), 0)
    ci = lax.broadcasted_iota(jnp.int32, (GDN_BLK, GDN_BLK), 1)
    same = (ri // CHUNK) == (ci // CHUNK)
    incl = (same & (ri >= ci), same & (ri <= ci))
    strict = (same & (ri > ci), same & (ri < ci))
    tri_bf = [jnp.where(m, 1.0, 0.0).astype(BF) for m in incl]
    eye_f = jnp.where(ri == ci, 1.0, 0.0)
    lane = lax.broadcasted_iota(jnp.int32, (GDN_BLK, LANE), 1)
    rowi = lax.broadcasted_iota(jnp.int32, (GDN_BLK, 1), 0)
    lvl_masks = []
    s = 1
    while s < CHUNK:
        lvl_masks.append(jnp.where(((ri // (2 * s)) == (ci // (2 * s))) & ((ri // s) != (ci // s)), 1.0, 0.0))
        s *= 2

    def conv_silu(x_ref, w_ref, n):
        x = x_ref[...].astype(F32)
        row = lax.broadcasted_iota(jnp.int32, x.shape, 0)
        xp = jnp.where(row == 0, 0.0, pltpu.roll(x, 1, 0))
        xn = jnp.where(row == n - 1, 0.0, pltpu.roll(x, n - 1, 0))
        return _silu(xp * w_ref[0:1, :] + x * w_ref[1:2, :] + xn * w_ref[2:3, :])

    def l2n(x):
        return x * lax.rsqrt(jnp.sum(x * x, axis=-1, keepdims=True) + EPS)

    for (q_ref, k_ref, v_ref, r0, n) in ((qc_ref, kc_ref, vc_ref, 0, CT), (ql_ref, kl_ref, vl_ref, CT, S)):
        q = conv_silu(q_ref, cwq_ref, n)
        k = conv_silu(k_ref, cwk_ref, n)
        v = conv_silu(v_ref, cwv_ref, n)
        for hh in range(2):
            sl = slice(hh * LANE, (hh + 1) * LANE)
            qs[r0:r0 + n, sl] = l2n(q[:, sl]) * (GDN_DK ** -0.5)
            ks[r0:r0 + n, sl] = l2n(k[:, sl])
        vs[r0:r0 + n, :] = v.astype(BF)

    neg_a = -jnp.exp(alog_ref[...])
    dtb = dtb_ref[...]

    def phase1(fwd_rows, bwd_rows):
        pipes = []
        a_list = []
        for d, rows in ((0, fwd_rows), (1, bwd_rows)):
            for g0 in rows:
                sm = sm_ref[pl.ds(g0, GDN_BLK), :]
                lg_all = neg_a * _softplus(sm + dtb)
                lb_all = _log_sigmoid(sm)
                cum_all = _exact_dot(tri_bf[d], lg_all)
                for hh in range(2):
                    h = 2 * pp + hh
                    sl = slice(hh * LANE, (hh + 1) * LANE)
                    k = ks[pl.ds(g0, GDN_BLK), sl]
                    q = qs[pl.ds(g0, GDN_BLK), sl]
                    v = vs[pl.ds(g0, GDN_BLK), sl]
                    kb = k.astype(BF)
                    kkqk = _dot_nt(jnp.concatenate([kb, q.astype(BF)], axis=0), kb)
                    kk = kkqk[0:GDN_BLK, :]
                    qk = kkqk[GDN_BLK:2 * GDN_BLK, :]
                    g = jnp.sum(jnp.where(lane == SM_A + d * N_HEADS + h, cum_all, 0.0), axis=-1, keepdims=True)
                    lb = jnp.sum(jnp.where(lane == SM_BT + d * N_HEADS + h, lb_all, 0.0), axis=-1, keepdims=True)
                    ends = (CHUNK - 1, GDN_BLK - 1) if d == 0 else (0, CHUNK)
                    tot_lo = g[ends[0]:ends[0] + 1, :]
                    tot_hi = g[ends[1]:ends[1] + 1, :]
                    tot = jnp.where(rowi < CHUNK, tot_lo, tot_hi)
                    hrow = jnp.broadcast_to(g - lb, (GDN_BLK, GDN_BLK)).T
                    e_in = jnp.exp(jnp.where(incl[d], g - hrow, -jnp.inf))
                    a_list.append(kk * jnp.where(strict[d], e_in, 0.0))
                    pipes.append((g0, hh, d, k, q, v, qk, e_in, g, lb, tot, tot_lo, tot_hi))
                yield
        a4 = jnp.stack(a_list, axis=0)
        x4 = eye_f[None] - a4 * lvl_masks[0][None]
        bdot = lambda p, r: jnp.einsum('pij,pjk->pik', p, r, preferred_element_type=F32)
        for msk in lvl_masks[1:]:
            t4 = (a4 * msk[None]).astype(BF)
            xb = x4.astype(BF)
            x4 = x4 - bdot(xb, bdot(t4, xb).astype(BF))
            yield
        x4b = x4.astype(BF)
        for p, (g0, hh, d, k, q, v, qk, e_in, g, lb, tot, tot_lo, tot_hi) in enumerate(pipes):
            gam = jnp.exp(g)
            rhs = jnp.concatenate([v, (k * gam).astype(BF)], axis=1)
            uw = _dot(x4b[p], rhs)
            u_ref[d, hh, pl.ds(g0, GDN_BLK), :] = uw[:, 0:LANE]
            w = uw[:, LANE:2 * LANE].astype(BF)
            qg = (q * gam).astype(BF)
            wq0 = _aligned(2 * g0, 2 * GDN_BLK)
            wq_ref[d, hh, pl.ds(wq0, 2 * GDN_BLK), :] = jnp.concatenate(
                [w[0:CHUNK], qg[0:CHUNK], w[CHUNK:GDN_BLK], qg[CHUNK:GDN_BLK]], axis=0)
            a_ref[d, hh, pl.ds(g0, GDN_BLK), :] = (qk * e_in).astype(BF)
            ke_ref[d, hh, pl.ds(g0, GDN_BLK), :] = (k * jnp.exp(tot - g + lb)).astype(BF)
            ge0 = _aligned((g0 // CHUNK) * 8, 16)
            ge_ref[d, hh, pl.ds(ge0, 16), :] = jnp.concatenate(
                [jnp.broadcast_to(jnp.exp(tot_lo), (8, LANE)), jnp.broadcast_to(jnp.exp(tot_hi), (8, LANE))], axis=0)
            if p % 4 == 3:
                yield

    def phase2(fwd_rows, bwd_rows):
        steps = []
        for gf, gb in zip(fwd_rows, bwd_rows):
            steps.append((gf, gb + CHUNK))
            steps.append((gf + CHUNK, gb))
        state = {(d, hh): st_ref[d, hh] for d in range(2) for hh in range(2)}
        outs = []
        for cf, cb in steps:
            chains = [(d, hh, _aligned(c, CHUNK)) for d, c in ((0, cf), (1, cb)) for hh in range(2)]
            wss = [_dot(wq_ref[d, hh, pl.ds(_aligned(2 * g0, 2 * CHUNK), 2 * CHUNK), :],
                        state[d, hh].astype(BF)) for (d, hh, g0) in chains]
            dbs = [(u_ref[d, hh, pl.ds(g0, CHUNK), :] - ws[0:CHUNK, :]).astype(BF)
                   for (d, hh, g0), ws in zip(chains, wss)]
            upd = [_dot_tn(ke_ref[d, hh, pl.ds(g0, CHUNK), :], db) for (d, hh, g0), db in zip(chains, dbs)]
            for (d, hh, g0), up in zip(chains, upd):
                ge = ge_ref[d, hh, pl.ds(_aligned((g0 // CHUNK) * 8, 8), 1), :]
                state[d, hh] = state[d, hh] * ge[:, 0:1] + up
            outs.append((chains, wss, dbs))
            yield
        for (d, hh), st in state.items():
            st_ref[d, hh] = st
        for chains, wss, dbs in outs:
            for (d, hh, g0), ws, db in zip(chains, wss, dbs):
                o = ws[CHUNK:2 * CHUNK, :] + _dot(a_ref[d, hh, pl.ds(g0, CHUNK), :],
                                                  jnp.concatenate([db, db], axis=0))
                oref = of_ref if d == 0 else ob_ref
                oref[pl.ds(g0, CHUNK), hh * LANE:(hh + 1) * LANE] = o
            yield

    def run(*gens):
        gens = list(gens)
        while gens:
            for gen in list(gens):
                try:
                    next(gen)
                except StopIteration:
                    gens.remove(gen)

    def seg_rows(base, nblk, first, count):
        fwd = [_aligned(base + (first + j) * GDN_BLK, GDN_BLK) for j in range(count)]
        bwd = [_aligned(base + (nblk - 1 - first - j) * GDN_BLK, GDN_BLK) for j in range(count)]
        return fwd, bwd

    nb_ctx = CT // GDN_BLK
    nb_lat = S // GDN_BLK
    grp = 4 if nb_lat % 4 == 0 else (2 if nb_lat % 2 == 0 else 1)
    n_grp = nb_lat // grp
    st_ref[...] = jnp.zeros_like(st_ref)

    ctx_rows = seg_rows(0, nb_ctx, 0, nb_ctx)
    run(phase1(*ctx_rows))
    run(phase1(*seg_rows(CT, nb_lat, 0, grp)), phase2(*ctx_rows))

    def stage(i, carry):
        run(phase1(*seg_rows(CT, nb_lat, i * grp, grp)), phase2(*seg_rows(CT, nb_lat, (i - 1) * grp, grp)))
        return carry

    lax.fori_loop(1, n_grp, stage, 0)
    run(phase2(*seg_rows(CT, nb_lat, (n_grp - 1) * grp, grp)))

    nw = nw_ref[...]
    for hh in range(2):
        sl = slice(hh * LANE, (hh + 1) * LANE)
        o = of_ref[CT:CT + S, sl] + ob_ref[CT:CT + S, sl]
        ol_ref[:, sl] = _finish_rms(o, nw, zl_ref[:, sl]).astype(ol_ref.dtype)
        if need_ctx:
            o = of_ref[0:CT, sl] + ob_ref[0:CT, sl]
            oc_ref[:, sl] = _finish_rms(o, nw, zc_ref[:, sl]).astype(oc_ref.dtype)


def _gdn(p_l, ps_l, p_c, ps_c, conv_w, a_log, dt_bias, o_norm, need_ctx):
    B, S, _ = p_l.shape
    CT = p_c.shape[1]
    T = CT + S
    nch = T // CHUNK
    alog = jnp.zeros((1, N_SMALL), F32).at[0, SM_A:SM_A + 2 * N_HEADS].set(a_log.reshape(-1).astype(F32))
    dtb = jnp.zeros((1, N_SMALL), F32).at[0, SM_A:SM_A + 2 * N_HEADS].set(dt_bias.reshape(-1).astype(F32))

    def spec(n, width, off):
        return pl.BlockSpec((None, n, width), lambda b, p: (b, 0, off(p)))

    in_specs = []
    for n in (CT, S):
        in_specs += [spec(n, 2 * LANE, lambda p: GDN_Q // 2 + p), spec(n, 2 * LANE, lambda p: GDN_K // 2 + p),
                     spec(n, 2 * LANE, lambda p: GDN_V // 2 + p), spec(n, 2 * LANE, lambda p: GDN_Z // 2 + p),
                     spec(n, N_SMALL, lambda p: 0)]
    cw = lambda part: pl.BlockSpec((3, 2 * LANE), lambda b, p: (0, 2 * part + p))
    vec = pl.BlockSpec((1, LANE), lambda b, p: (0, 0))
    in_specs += [cw(0), cw(1), cw(2), vec, vec, vec]
    out_specs = [pl.BlockSpec((None, S, 2 * LANE), lambda b, p: (b, 0, p))]
    out_shape = [jax.ShapeDtypeStruct((B, S, GROUP_W), BF)]
    if need_ctx:
        out_specs.append(pl.BlockSpec((None, CT, 2 * LANE), lambda b, p: (b, 0, p)))
        out_shape.append(jax.ShapeDtypeStruct((B, CT, GROUP_W), BF))
    res = pl.pallas_call(
        functools.partial(_gdn_kernel, need_ctx=need_ctx, nc_ctx=CT // CHUNK, nc_lat=S // CHUNK),
        grid=(B, 2),
        in_specs=in_specs,
        out_specs=out_specs,
        out_shape=out_shape,
        scratch_shapes=[pltpu.VMEM((T, 2 * LANE), F32),
                        pltpu.VMEM((T, 2 * LANE), F32),
                        pltpu.VMEM((T, 2 * LANE), BF),
                        pltpu.VMEM((T, N_SMALL), F32),
                        pltpu.VMEM((2, 2, T, LANE), F32),
                        pltpu.VMEM((2, 2, 2 * T, LANE), BF),
                        pltpu.VMEM((2, 2, T, GDN_BLK), BF),
                        pltpu.VMEM((2, 2, T, LANE), BF),
                        pltpu.VMEM((2, 2, nch * 8, LANE), F32),
                        pltpu.VMEM((2, 2, LANE, LANE), F32),
                        pltpu.VMEM((T, 2 * LANE), F32),
                        pltpu.VMEM((T, 2 * LANE), F32)],
        compiler_params=_cp(("parallel", "parallel")),
        name="gdn",
    )(p_c, p_c, p_c, p_c, ps_c, p_l, p_l, p_l, p_l, ps_l,
      conv_w, conv_w, conv_w, alog, dtb, o_norm.reshape(1, LANE))
    return (res[1] if need_ctx else None), res[0]


def _align_w_in(w):
    big = jnp.concatenate([w[..., 0:3072], w[..., 3104:5152], w[..., 5168:6704]], axis=-1)
    small = jnp.concatenate([w[..., 3072:3104], w[..., 5152:5168],
                             jnp.zeros(w.shape[:-1] + (N_SMALL - 48,), w.dtype)], axis=-1)
    return big.astype(BF), small.astype(BF)


def _pick(n, prefs):
    for p in prefs:
        if n % p == 0:
            return p
    return n


def kernel(x, c, ctx, c_ctx, ada_w, ada_b, norm1_w, norm2_w, w_in, w_out, na_q_norm, na_k_norm, na_rpb,
           gla_gate_up, gla_gate_b, gla_o_norm, gdn_conv_w, gdn_a_log, gdn_dt_bias, gdn_o_norm,
           ret_decay_logit, ret_gn_w, mlp_w1, mlp_w2):
    B, S, D = x.shape
    CT = ctx.shape[1]
    depth = ada_w.shape[0]
    R = ((B + 1 + 7) // 8) * 8
    cc = jnp.concatenate([c, c_ctx[None, :], jnp.zeros((R - B - 1, D), F32)], axis=0)
    mod_all = _ada(cc, ada_w, ada_b).reshape(depth, R, 6, D)
    na_tables = _natten_tables(na_rpb, S)

    tm_l = _pick(S, (1024, 512, 256))
    tm_c = _pick(B * CT, (1024, 512, 256))
    tn = _pick(N_BIG, (1664, 512, 256, 128))
    tm_o = _pick(S, (512, 256))
    tm_oc = _pick(B * CT, (512, 256))
    th = _pick(mlp_w1.shape[2], (1024, 512, 256))

    w_big, w_small = _align_w_in(w_in)
    wo = w_out.astype(BF)
    w1 = mlp_w1.astype(BF)
    w2 = mlp_w2.astype(BF)

    xl = x
    xc = ctx.reshape(1, B * CT, D)
    for layer in range(depth):
        need_ctx = layer < depth - 1
        mod = mod_all[layer]
        nw1 = norm1_w[layer].reshape(1, D)
        nw2 = norm2_w[layer].reshape(1, D)

        p_l, ps_l = _in_proj(xl, mod, None, nw1, w_big, w_small, layer, tm_l, tn)
        p_c, ps_c = _in_proj(xc, mod, B, nw1, w_big, w_small, layer, tm_c, tn)
        p_c = p_c.reshape(B, CT, N_BIG)
        ps_c = ps_c.reshape(B, CT, N_SMALL)

        na_c, na_l = _natten(p_l, p_c, na_q_norm[layer], na_k_norm[layer], na_tables, layer, need_ctx)
        gl_c, gl_l = _gla(p_l, ps_l, p_c, ps_c, gla_gate_up[layer], gla_gate_b[layer], gla_o_norm[layer], need_ctx)
        gd_c, gd_l = _gdn(p_l, ps_l, p_c, ps_c, gdn_conv_w[layer], gdn_a_log[layer], gdn_dt_bias[layer],
                          gdn_o_norm[layer], need_ctx)
        rt_c, rt_l = _ret(p_l, p_c, ret_decay_logit[layer], ret_gn_w[layer], need_ctx)

        xl, hl = _out_proj(xl, (na_l, gl_l, gd_l, rt_l), wo, layer, mod, None, nw2, tm_o)
        xl = _mlp(xl, hl, mod, None, w1, w2, layer, tm_o, th)
        if need_ctx:
            ys = tuple(t.reshape(1, B * CT, GROUP_W) for t in (na_c, gl_c, gd_c, rt_c))
            xc, hc = _out_proj(xc, ys, wo, layer, mod, B, nw2, tm_oc)
            xc = _mlp(xc, hc, mod, B, w1, w2, layer, tm_oc, th)
    return xl
```

```python
import functools

import numpy as np
import jax
import jax.numpy as jnp
from jax import lax
from jax.experimental import pallas as pl
from jax.experimental.pallas import tpu as pltpu

BF = jnp.bfloat16
F32 = jnp.float32

N_HEADS = 4
HEAD_DIM = 128
GROUP_W = N_HEADS * HEAD_DIM
GRID_W = 64
NA_WIN_ROWS = 8
NA_WIN_COLS = 16
NA_GROUP = 4
GLA_DK = 64
GLA_GATE_RANK = 16
GLA_GATE_TAU = 16.0
GDN_DK = 128
RET_DK = 64
ROPE_BASE = 10000.0
CHUNK = 64
GDN_BLK = 2 * CHUNK
EPS = 1e-6
NEG_INF = -1e30

LANE = 128
N_BIG = 52 * LANE
N_SMALL = LANE
NA_Q, NA_K, NA_V = 0, 4, 8
GLA_Q, GLA_K, GLA_V, GLA_G = 12, 14, 16, 20
GDN_Q, GDN_K, GDN_V, GDN_Z = 24, 28, 32, 36
RET_Q, RET_K, RET_V, RET_G = 40, 42, 44, 48
SM_RK, SM_A, SM_BT = 0, 32, 40

VMEM_LIMIT = 56 * 1024 * 1024


def _cp(sem, vmem=VMEM_LIMIT):
    return pltpu.CompilerParams(dimension_semantics=sem, vmem_limit_bytes=vmem)


def _aligned(x, m):
    return x if isinstance(x, int) else pl.multiple_of(x, m)


def _dot(a, b):
    return jnp.dot(a, b, preferred_element_type=F32)


def _dot_nt(a, b):
    return lax.dot_general(a, b, (((1,), (1,)), ((), ())), preferred_element_type=F32)


def _dot_tn(a, b):
    return lax.dot_general(a, b, (((0,), (0,)), ((), ())), preferred_element_type=F32)


def _split3(x):
    hi = x.astype(BF)
    r1 = x - hi.astype(F32)
    mid = r1.astype(BF)
    lo = (r1 - mid.astype(F32)).astype(BF)
    return hi, mid, lo


def _exact_dot(m_bf, x):
    hi, mid, lo = _split3(x)
    n = x.shape[1]
    r = _dot(m_bf, jnp.concatenate([hi, mid, lo], axis=1))
    return r[:, 0:n] + r[:, n:2 * n] + r[:, 2 * n:3 * n]


def _sigmoid(x):
    return 1.0 / (1.0 + jnp.exp(-x))


def _silu(x):
    return x * _sigmoid(x)


def _log_sigmoid(x):
    return jnp.minimum(x, 0.0) - jnp.log(1.0 + jnp.exp(-jnp.abs(x)))


def _softplus(x):
    return jnp.maximum(x, 0.0) + jnp.log(1.0 + jnp.exp(-jnp.abs(x)))


def _ln_mod(x, nw, shift, scale):
    ms = jnp.mean(x * x, axis=-1, keepdims=True)
    return (x * lax.rsqrt(ms + EPS) * nw) * (1.0 + scale) + shift


def _tri_consts():
    ri = lax.broadcasted_iota(jnp.int32, (CHUNK, CHUNK), 0)
    ci = lax.broadcasted_iota(jnp.int32, (CHUNK, CHUNK), 1)
    incl = (ri >= ci, ri <= ci)
    strict = (ri > ci, ri < ci)
    tri_bf = tuple(jnp.where(m, 1.0, 0.0).astype(BF) for m in incl)
    return ri, ci, incl, strict, tri_bf


def _ada_kernel(c_ref, w_ref, b_ref, o_ref):
    sc = _silu(c_ref[...]).astype(BF)
    o_ref[...] = _dot(sc, w_ref[...].astype(BF)) + b_ref[...]


def _ada(cc, ada_w, ada_b):
    L, D, N6 = ada_w.shape
    R = cc.shape[0]
    tn = _pick(N6, (2048, 1024))
    return pl.pallas_call(
        _ada_kernel,
        grid=(L, N6 // tn),
        in_specs=[pl.BlockSpec((R, D), lambda l, j: (0, 0)),
                  pl.BlockSpec((None, D, tn), lambda l, j: (l, 0, j)),
                  pl.BlockSpec((None, 1, tn), lambda l, j: (l, 0, j))],
        out_specs=pl.BlockSpec((None, R, tn), lambda l, j: (l, 0, j)),
        out_shape=jax.ShapeDtypeStruct((L, R, N6), F32),
        compiler_params=_cp(("parallel", "parallel")),
        name="ada_ln",
    )(cc, ada_w, ada_b.reshape(L, 1, N6))


def _inproj_kernel(x_ref, mod_ref, nw_ref, w_ref, ws_ref, o_ref, os_ref, h_ref):
    @pl.when(pl.program_id(1) == 0)
    def _():
        h = _ln_mod(x_ref[...], nw_ref[...], mod_ref[0:1, :], mod_ref[1:2, :]).astype(BF)
        h_ref[...] = h
        os_ref[...] = _dot(h, ws_ref[...])

    o_ref[...] = _dot(h_ref[...], w_ref[...]).astype(o_ref.dtype)


def _in_proj(x3, mod, const_row, nw, w_big, w_small, layer, tm, tn):
    Bn, Tn, D = x3.shape
    nt = Tn // tm
    if const_row is None:
        mod_map = lambda i, j: (i // nt, 0, 0)
    else:
        mod_map = lambda i, j: (const_row, 0, 0)
    return pl.pallas_call(
        _inproj_kernel,
        grid=(Bn * nt, N_BIG // tn),
        in_specs=[pl.BlockSpec((None, tm, D), lambda i, j: (i // nt, i % nt, 0)),
                  pl.BlockSpec((None, 6, D), mod_map),
                  pl.BlockSpec((1, D), lambda i, j: (0, 0)),
                  pl.BlockSpec((None, D, tn), lambda i, j: (layer, 0, j)),
                  pl.BlockSpec((None, D, N_SMALL), lambda i, j: (layer, 0, 0))],
        out_specs=[pl.BlockSpec((None, tm, tn), lambda i, j: (i // nt, i % nt, j)),
                   pl.BlockSpec((None, tm, N_SMALL), lambda i, j: (i // nt, i % nt, 0))],
        out_shape=[jax.ShapeDtypeStruct((Bn, Tn, N_BIG), BF),
                   jax.ShapeDtypeStruct((Bn, Tn, N_SMALL), F32)],
        scratch_shapes=[pltpu.VMEM((tm, D), BF)],
        compiler_params=_cp(("parallel", "arbitrary")),
        name="in_proj",
    )(x3, mod, nw, w_big, w_small)


def _outproj_kernel(x_ref, y0, y1, y2, y3, w_ref, mod_ref, nw_ref, o_ref, h_ref):
    acc = _dot(y0[...], w_ref[0 * GROUP_W:1 * GROUP_W, :])
    acc += _dot(y1[...], w_ref[1 * GROUP_W:2 * GROUP_W, :])
    acc += _dot(y2[...], w_ref[2 * GROUP_W:3 * GROUP_W, :])
    acc += _dot(y3[...], w_ref[3 * GROUP_W:4 * GROUP_W, :])
    xn = x_ref[...] + mod_ref[2:3, :] * acc
    o_ref[...] = xn
    h_ref[...] = _ln_mod(xn, nw_ref[...], mod_ref[3:4, :], mod_ref[4:5, :]).astype(BF)


def _out_proj(x3, ys, w_out, layer, mod, const_row, nw2, tm):
    Bn, Tn, D = x3.shape
    nt = Tn // tm
    if const_row is None:
        mod_map = lambda i: (i // nt, 0, 0)
    else:
        mod_map = lambda i: (const_row, 0, 0)
    row_map = lambda i: (i // nt, i % nt, 0)
    return pl.pallas_call(
        _outproj_kernel,
        grid=(Bn * nt,),
        in_specs=[pl.BlockSpec((None, tm, D), row_map)]
                 + [pl.BlockSpec((None, tm, GROUP_W), row_map)] * 4
                 + [pl.BlockSpec((None, 4 * GROUP_W, D), lambda i: (layer, 0, 0)),
                    pl.BlockSpec((None, 6, D), mod_map),
                    pl.BlockSpec((1, D), lambda i: (0, 0))],
        out_specs=[pl.BlockSpec((None, tm, D), row_map), pl.BlockSpec((None, tm, D), row_map)],
        out_shape=[jax.ShapeDtypeStruct((Bn, Tn, D), F32), jax.ShapeDtypeStruct((Bn, Tn, D), BF)],
        compiler_params=_cp(("parallel",)),
        name="out_proj",
    )(x3, *ys, w_out, mod, nw2)


def _mlp_kernel(x_ref, h_ref, mod_ref, w1_ref, w2_ref, o_ref, *, nk):
    k = pl.program_id(1)

    @pl.when(k == 0)
    def _():
        o_ref[...] = jnp.zeros_like(o_ref)

    hid = jnp.maximum(_dot(h_ref[...], w1_ref[...]), 0.0)
    o_ref[...] += _dot((hid * hid).astype(BF), w2_ref[...])

    @pl.when(k == nk - 1)
    def _():
        o_ref[...] = x_ref[...] + mod_ref[5:6, :] * o_ref[...]


def _mlp(x3, h3, mod, const_row, w1, w2, layer, tm, th):
    Bn, Tn, D = x3.shape
    Hd = w1.shape[2]
    nt = Tn // tm
    nk = Hd // th
    if const_row is None:
        mod_map = lambda i, k: (i // nt, 0, 0)
    else:
        mod_map = lambda i, k: (const_row, 0, 0)
    row_map = lambda i, k: (i // nt, i % nt, 0)
    return pl.pallas_call(
        functools.partial(_mlp_kernel, nk=nk),
        grid=(Bn * nt, nk),
        in_specs=[pl.BlockSpec((None, tm, D), row_map),
                  pl.BlockSpec((None, tm, D), row_map),
                  pl.BlockSpec((None, 6, D), mod_map),
                  pl.BlockSpec((None, D, th), lambda i, k: (layer, 0, k)),
                  pl.BlockSpec((None, th, D), lambda i, k: (layer, k, 0))],
        out_specs=pl.BlockSpec((None, tm, D), row_map),
        out_shape=jax.ShapeDtypeStruct((Bn, Tn, D), F32),
        compiler_params=_cp(("parallel", "arbitrary")),
        name="mlp",
    )(x3, h3, mod, w1, w2)


def _rms_head(x, w):
    x = x.astype(F32)
    return x * lax.rsqrt(jnp.mean(x * x, axis=-1, keepdims=True) + EPS) * w


def _natten_kernel(geo_ref, ql_ref, kl_ref, vl_ref, qc_ref, kc_ref, vc_ref, qw_ref, kw_ref, bias_ref, *rest,
                   need_ctx, rows, grp, span):
    if need_ctx:
        ol_ref, oc_ref, qs, ks = rest
    else:
        ol_ref, qs, ks = rest
    scale = HEAD_DIM ** -0.5
    qw = qw_ref[...]
    kw = kw_ref[...]
    qs[...] = (_rms_head(ql_ref[...], qw) * scale).astype(BF)
    ks[...] = _rms_head(kl_ref[...], kw).astype(BF)
    kc = _rms_head(kc_ref[...], kw).astype(BF)
    vc = vc_ref[...]
    if need_ctx:
        qc = (_rms_head(qc_ref[...], qw) * scale).astype(BF)
        s = _dot_nt(qc, kc)
        p = jnp.exp(s - jnp.max(s, axis=-1, keepdims=True))
        l = jnp.sum(p, axis=-1, keepdims=True)
        oc_ref[...] = (_dot(p.astype(BF), vc) / l).astype(oc_ref.dtype)

    nq = grp * GRID_W
    nk = span * GRID_W

    ngroups = rows // grp
    per_iter = next(p for p in (4, 2, 1) if ngroups % p == 0)

    def body(it, carry):
        geo = []
        for j in range(per_iter):
            g = it * per_iter + j
            geo.append((pl.multiple_of(g * nq, nq), pl.multiple_of(geo_ref[0, g] * GRID_W, GRID_W), geo_ref[1, g]))
        qv = [qs[pl.ds(q0, nq), :] for (q0, k0, t) in geo]
        sws = [_dot_nt(q, ks[pl.ds(k0, nk), :]) + bias_ref[t] for q, (q0, k0, t) in zip(qv, geo)]
        scs = [_dot_nt(q, kc) for q in qv]
        probs = []
        for sw, sc in zip(sws, scs):
            m = jnp.maximum(jnp.max(sw, axis=-1, keepdims=True), jnp.max(sc, axis=-1, keepdims=True))
            pw = jnp.exp(sw - m)
            pc = jnp.exp(sc - m)
            l = jnp.sum(pw, axis=-1, keepdims=True) + jnp.sum(pc, axis=-1, keepdims=True)
            probs.append((pw.astype(BF), pc.astype(BF), l))
        for (pw, pc, l), (q0, k0, t) in zip(probs, geo):
            o = (_dot(pw, vl_ref[pl.ds(k0, nk), :]) + _dot(pc, vc)) / l
            ol_ref[pl.ds(q0, nq), :] = o.astype(ol_ref.dtype)
        return carry

    lax.fori_loop(0, ngroups // per_iter, body, 0)


def _natten_geometry(rows, kh):
    grp = NA_GROUP if rows % NA_GROUP == 0 else 1
    span = min(kh + grp - 1, rows)
    starts, type_ids, types = [], [], []
    for g in range(rows // grp):
        rs = [int(np.clip(g * grp + j - kh // 2, 0, rows - kh)) for j in range(grp)]
        us = int(np.clip(rs[0], 0, rows - span))
        sig = tuple((rs[j] - us, g * grp + j - rs[j]) for j in range(grp))
        assert all(0 <= off <= span - kh for off, _ in sig)
        if sig not in types:
            types.append(sig)
        starts.append(us)
        type_ids.append(types.index(sig))
    return grp, span, np.asarray([starts, type_ids], np.int32), types


def _natten_bias(rpb, kh, span, types):
    q = np.arange(GRID_W)[:, None]
    kc = np.arange(GRID_W)[None, :]
    col_off = np.clip(kc - q, -(NA_WIN_COLS - 1), NA_WIN_COLS - 1) + NA_WIN_COLS - 1
    onehot = (col_off[..., None] == np.arange(2 * NA_WIN_COLS - 1)).astype(np.float32)
    toe = jnp.einsum('lhrc,qkc->lhrqk', rpb.astype(F32), jnp.asarray(onehot), precision=lax.Precision.HIGHEST)
    cs = np.clip(q - NA_WIN_COLS // 2, 0, GRID_W - NA_WIN_COLS)
    valid = (kc >= cs) & (kc < cs + NA_WIN_COLS)
    toe = jnp.where(valid[None, None, None], toe, NEG_INF)
    L, H = rpb.shape[:2]
    pad = jnp.full((L, H, span, GRID_W, GRID_W), NEG_INF, F32)
    toe = jnp.concatenate([pad, toe, pad], axis=2).transpose(0, 1, 3, 2, 4)
    slabs, keep = [], []
    for sig in types:
        for off, d in sig:
            start = span + NA_WIN_ROWS - 1 - d - off
            slabs.append(toe[:, :, :, start:start + span])
            keep.append([0 <= i - off < kh for i in range(span)])
    slab = jnp.stack(slabs, axis=2)
    slab = jnp.where(np.asarray(keep)[None, None, :, None, :, None], slab, NEG_INF)
    grp = len(types[0])
    return slab.reshape(L, H, len(types), grp * GRID_W, span * GRID_W)


def _natten_tables(rpb_all, S):
    rows = S // GRID_W
    kh = min(NA_WIN_ROWS, rows)
    grp, span, geo, types = _natten_geometry(rows, kh)
    return grp, span, geo, _natten_bias(rpb_all, kh, span, types)


def _natten(p_l, p_c, qw, kw, tables, layer, need_ctx):
    B, S, _ = p_l.shape
    CT = p_c.shape[1]
    rows = S // GRID_W
    grp, span, geo, bias_all = tables
    n_types = bias_all.shape[2]
    lat = lambda off: pl.BlockSpec((None, S, LANE), lambda b, h: (b, 0, off + h))
    ctx = lambda off: pl.BlockSpec((None, CT, LANE), lambda b, h: (b, 0, off + h))
    vec = pl.BlockSpec((1, LANE), lambda b, h: (0, 0))
    out_specs = [pl.BlockSpec((None, S, LANE), lambda b, h: (b, 0, h))]
    out_shape = [jax.ShapeDtypeStruct((B, S, GROUP_W), BF)]
    if need_ctx:
        out_specs.append(pl.BlockSpec((None, CT, LANE), lambda b, h: (b, 0, h)))
        out_shape.append(jax.ShapeDtypeStruct((B, CT, GROUP_W), BF))
    res = pl.pallas_call(
        functools.partial(_natten_kernel, need_ctx=need_ctx, rows=rows, grp=grp, span=span),
        grid=(B, N_HEADS),
        in_specs=[pl.BlockSpec(memory_space=pltpu.SMEM),
                  lat(NA_Q), lat(NA_K), lat(NA_V), ctx(NA_Q), ctx(NA_K), ctx(NA_V), vec, vec,
                  pl.BlockSpec((None, None, n_types, grp * GRID_W, span * GRID_W),
                               lambda b, h: (layer, h, 0, 0, 0))],
        out_specs=out_specs,
        out_shape=out_shape,
        scratch_shapes=[pltpu.VMEM((S, LANE), BF), pltpu.VMEM((S, LANE), BF)],
        compiler_params=_cp(("parallel", "parallel")),
        name="natten",
    )(jnp.asarray(geo), p_l, p_l, p_l, p_c, p_c, p_c, qw.reshape(1, LANE), kw.reshape(1, LANE), bias_all)
    return (res[1] if need_ctx else None), res[0]


def _pair_consts():
    lane = lax.broadcasted_iota(jnp.int32, (CHUNK, LANE), 1)
    row = lax.broadcasted_iota(jnp.int32, (CHUNK, LANE), 0)
    col = lane & (CHUNK - 1)
    hmask = (lane < CHUNK, lane >= CHUNK)
    incl2 = (row >= col, row <= col)
    r2 = lax.broadcasted_iota(jnp.int32, (LANE, 2 * LANE), 0)
    c2 = lax.broadcasted_iota(jnp.int32, (LANE, 2 * LANE), 1)
    bmask = (r2 < CHUNK) == (c2 < LANE)
    eye = lax.broadcasted_iota(jnp.int32, (LANE, LANE), 0) == lax.broadcasted_iota(jnp.int32, (LANE, LANE), 1)
    return hmask, incl2, bmask, eye


def _finish_rms(o, nw, g):
    y = o * lax.rsqrt(jnp.mean(o * o, axis=-1, keepdims=True) + EPS) * nw
    return y * _silu(g.astype(F32))


def _gla_kernel(qc_ref, kc_ref, vc_ref, gc_ref, sc_ref, ql_ref, kl_ref, vl_ref, gl_ref, sl_ref,
                gup_ref, gb_ref, nw_ref, *rest, need_ctx, nc_ctx, nc_lat):
    if need_ctx:
        ol_ref, oc_ref, lg_ref, st_ref, of_ref, ob_ref = rest
    else:
        ol_ref, lg_ref, st_ref, of_ref, ob_ref = rest
        oc_ref = None
    CT = nc_ctx * CHUNK
    S = nc_lat * CHUNK
    _, _, _, _, tri_bf = _tri_consts()
    hmask, incl2, bmask, eye = _pair_consts()
    zero_v = jnp.zeros((CHUNK, LANE), BF)

    for d in range(2):
        for (s_ref, r0, n) in ((sc_ref, 0, CT), (sl_ref, CT, S)):
            z = _dot(s_ref[...].astype(BF), gup_ref[d]) + gb_ref[d]
            lg_ref[d, r0:r0 + n, :] = _log_sigmoid(z) * (1.0 / GLA_GATE_TAU)
    st_ref[...] = jnp.zeros_like(st_ref)

    def segment(q_ref, k_ref, v_ref, row0, nch, write):
        per_iter = next(p for p in (8, 4, 2, 1) if nch % p == 0)

        def body(it, carry):
            pre = []
            for j in range(per_iter):
                n = it * per_iter + j
                for d in range(2):
                    c = n if d == 0 else nch - 1 - n
                    r0 = pl.multiple_of(c * CHUNK, CHUNK)
                    pre.append((d, r0, q_ref[pl.ds(r0, CHUNK), :].astype(F32), k_ref[pl.ds(r0, CHUNK), :].astype(F32)))
            cums = [_exact_dot(tri_bf[d], lg_ref[d, pl.ds(row0 + r0, CHUNK), :]) for (d, r0, _, _) in pre]
            work = []
            for (d, r0, q2, k2), cum in zip(pre, cums):
                tot = cum[CHUNK - 1:CHUNK, :] if d == 0 else cum[0:1, :]
                ge_col = jnp.sum(jnp.where(eye, jnp.exp(tot), 0.0), axis=1, keepdims=True)
                ke = (k2 * jnp.exp(tot - cum)).astype(BF)
                vp = v_ref[pl.ds(r0, CHUNK), :]
                qd = kdm = None
                if write:
                    qd = (q2 * (jnp.exp(cum) * (GLA_DK ** -0.5))).astype(BF)
                    kd = k2 * jnp.exp(-cum)
                    kdm = jnp.concatenate([jnp.where(hmask[0], kd, 0.0), jnp.where(hmask[1], kd, 0.0)],
                                          axis=0).astype(BF)
                work.append((d, r0, vp, qd, kdm, ke, ge_col))
            if write:
                atts = [jnp.where(incl2[d], _dot_nt(qd, kdm), 0.0).astype(BF)
                        for (d, r0, vp, qd, kdm, ke, ge_col) in work]
                intras = [_dot(att, jnp.concatenate(
                              [jnp.concatenate([vp[:, 0:LANE], zero_v], axis=1),
                               jnp.concatenate([zero_v, vp[:, LANE:2 * LANE]], axis=1)], axis=0))
                          for att, (d, r0, vp, qd, kdm, ke, ge_col) in zip(atts, work)]
            upds = [jnp.where(bmask, _dot_tn(ke, vp), 0.0) for (d, r0, vp, qd, kdm, ke, ge_col) in work]
            state = {d: st_ref[d] for d in range(2)}
            for i, (d, r0, vp, qd, kdm, ke, ge_col) in enumerate(work):
                st = state[d]
                if write:
                    oref = of_ref if d == 0 else ob_ref
                    oref[pl.ds(row0 + r0, CHUNK), :] = intras[i] + _dot(qd, st.astype(BF))
                state[d] = st * ge_col + upds[i]
            for d, st in state.items():
                st_ref[d] = st
            return carry

        lax.fori_loop(0, nch // per_iter, body, 0)

    segment(qc_ref, kc_ref, vc_ref, 0, nc_ctx, need_ctx)
    segment(ql_ref, kl_ref, vl_ref, CT, nc_lat, True)

    nw = nw_ref[...]
    for hh in range(2):
        sl = slice(hh * LANE, (hh + 1) * LANE)
        o = of_ref[CT:CT + S, sl] + ob_ref[CT:CT + S, sl]
        ol_ref[:, sl] = _finish_rms(o, nw, gl_ref[:, sl]).astype(ol_ref.dtype)
        if need_ctx:
            o = of_ref[0:CT, sl] + ob_ref[0:CT, sl]
            oc_ref[:, sl] = _finish_rms(o, nw, gc_ref[:, sl]).astype(oc_ref.dtype)


def _gla(p_l, ps_l, p_c, ps_c, gate_up, gate_b, o_norm, need_ctx):
    B, S, _ = p_l.shape
    CT = p_c.shape[1]
    T = CT + S
    gup = jnp.zeros((2, N_SMALL, N_HEADS * GLA_DK), F32)
    for d in range(2):
        gup = gup.at[d, SM_RK + d * GLA_GATE_RANK:SM_RK + (d + 1) * GLA_GATE_RANK].set(gate_up[d])
    gup = gup.astype(BF)
    gb = gate_b.reshape(2, 1, N_HEADS * GLA_DK)

    def spec(n, width, off):
        return pl.BlockSpec((None, n, width), lambda b, p: (b, 0, off(p)))

    in_specs = []
    for n in (CT, S):
        in_specs += [spec(n, LANE, lambda p: GLA_Q + p), spec(n, LANE, lambda p: GLA_K + p),
                     spec(n, 2 * LANE, lambda p: GLA_V // 2 + p), spec(n, 2 * LANE, lambda p: GLA_G // 2 + p),
                     spec(n, N_SMALL, lambda p: 0)]
    in_specs += [pl.BlockSpec((2, N_SMALL, LANE), lambda b, p: (0, 0, p)),
                 pl.BlockSpec((2, 1, LANE), lambda b, p: (0, 0, p)),
                 pl.BlockSpec((1, LANE), lambda b, p: (0, 0))]
    out_specs = [pl.BlockSpec((None, S, 2 * LANE), lambda b, p: (b, 0, p))]
    out_shape = [jax.ShapeDtypeStruct((B, S, GROUP_W), BF)]
    if need_ctx:
        out_specs.append(pl.BlockSpec((None, CT, 2 * LANE), lambda b, p: (b, 0, p)))
        out_shape.append(jax.ShapeDtypeStruct((B, CT, GROUP_W), BF))
    res = pl.pallas_call(
        functools.partial(_gla_kernel, need_ctx=need_ctx, nc_ctx=CT // CHUNK, nc_lat=S // CHUNK),
        grid=(B, 2),
        in_specs=in_specs,
        out_specs=out_specs,
        out_shape=out_shape,
        scratch_shapes=[pltpu.VMEM((2, T, LANE), F32),
                        pltpu.VMEM((2, LANE, 2 * LANE), F32),
                        pltpu.VMEM((T, 2 * LANE), F32),
                        pltpu.VMEM((T, 2 * LANE), F32)],
        compiler_params=_cp(("parallel", "parallel")),
        name="gla",
    )(p_c, p_c, p_c, p_c, ps_c, p_l, p_l, p_l, p_l, ps_l, gup, gb, o_norm.reshape(1, LANE))
    return (res[1] if need_ctx else None), res[0]


def _rope_tables(S):
    pos = np.arange(S)
    half = RET_DK // 2
    quarter = half // 2
    freqs = ROPE_BASE ** (-np.arange(quarter, dtype=np.float64) / quarter)
    cos = np.zeros((S, LANE), np.float64)
    sin_dn = np.zeros((S, LANE), np.float64)
    sin_up = np.zeros((S, LANE), np.float64)
    for head in range(2):
        for part, p in enumerate((pos // GRID_W, pos % GRID_W)):
            ang = p[:, None].astype(np.float64) * freqs[None, :]
            base = head * RET_DK + part * half
            cos[:, base:base + quarter] = np.cos(ang)
            cos[:, base + quarter:base + half] = np.cos(ang)
            sin_dn[:, base:base + quarter] = -np.sin(ang)
            sin_up[:, base + quarter:base + half] = np.sin(ang)
    return tuple(jnp.asarray(t, F32) for t in (cos, sin_dn, sin_up))


def _ret_kernel(qc_ref, kc_ref, vc_ref, gc_ref, ql_ref, kl_ref, vl_ref, gl_ref,
                cos_ref, sdn_ref, sup_ref, dl_ref, gnw_ref, *rest, need_ctx, nc_ctx, nc_lat):
    if need_ctx:
        ol_ref, oc_ref, qr_ref, kr_ref, st_ref, of_ref, ob_ref = rest
    else:
        ol_ref, qr_ref, kr_ref, st_ref, of_ref, ob_ref = rest
        oc_ref = None
    CT = nc_ctx * CHUNK
    S = nc_lat * CHUNK
    hmask, incl2, bmask, _ = _pair_consts()
    zero_v = jnp.zeros((CHUNK, LANE), BF)
    pp = pl.program_id(1)

    def rope(x):
        quarter = RET_DK // 4
        return (x * cos_ref[...] + pltpu.roll(x, LANE - quarter, 1) * sdn_ref[...]
                + pltpu.roll(x, quarter, 1) * sup_ref[...])

    qr_ref[0:CT, :] = qc_ref[...].astype(F32)
    kr_ref[0:CT, :] = kc_ref[...].astype(F32) * (RET_DK ** -0.5)
    qr_ref[CT:CT + S, :] = rope(ql_ref[...].astype(F32))
    kr_ref[CT:CT + S, :] = rope(kl_ref[...].astype(F32) * (RET_DK ** -0.5))
    st_ref[...] = jnp.zeros_like(st_ref)

    dmat, qfac, kfac, gend = {}, {}, {}, {}
    rowf = lax.broadcasted_iota(jnp.int32, (CHUNK, LANE), 0).astype(F32)
    colf = (lax.broadcasted_iota(jnp.int32, (CHUNK, LANE), 1) & (CHUNK - 1)).astype(F32)
    lane_v = lax.broadcasted_iota(jnp.int32, (CHUNK, 2 * LANE), 1)
    row_v = lax.broadcasted_iota(jnp.int32, (CHUNK, 2 * LANE), 0).astype(F32)
    row_k = lax.broadcasted_iota(jnp.int32, (LANE, 1), 0)
    for d in range(2):
        lg_a = _log_sigmoid(dl_ref[pl.ds(d * N_HEADS + 2 * pp, 1), 0:1])
        lg_b = _log_sigmoid(dl_ref[pl.ds(d * N_HEADS + 2 * pp + 1, 1), 0:1])
        lg2 = jnp.where(hmask[0], lg_a, lg_b)
        dist = (rowf - colf) if d == 0 else (colf - rowf)
        dmat[d] = jnp.exp(jnp.where(incl2[d], dist * lg2, -jnp.inf))
        steps_v = (row_v + 1.0) if d == 0 else (CHUNK - row_v)
        qfac[d] = jnp.exp(steps_v * jnp.where(lane_v < LANE, lg_a, lg_b))
        steps_k = (rowf + 1.0) if d == 0 else (CHUNK - rowf)
        kfac[d] = jnp.exp((CHUNK - steps_k) * lg2)
        gend[d] = jnp.exp(CHUNK * jnp.where(row_k < CHUNK, lg_a, lg_b))

    def segment(v_ref, row0, nch, write):
        per_iter = next(p for p in (8, 4, 2, 1) if nch % p == 0)

        def body(it, carry):
            work = []
            for j in range(per_iter):
                n = it * per_iter + j
                for d in range(2):
                    c = n if d == 0 else nch - 1 - n
                    r0 = pl.multiple_of(c * CHUNK, CHUNK)
                    k2 = kr_ref[pl.ds(row0 + r0, CHUNK), :]
                    vp = v_ref[pl.ds(r0, CHUNK), :]
                    qb = km = None
                    if write:
                        qb = qr_ref[pl.ds(row0 + r0, CHUNK), :].astype(BF)
                        km = jnp.concatenate([jnp.where(hmask[0], k2, 0.0), jnp.where(hmask[1], k2, 0.0)],
                                             axis=0).astype(BF)
                    work.append((d, r0, vp, qb, km, (k2 * kfac[d]).astype(BF)))
            if write:
                atts = [(_dot_nt(qb, km) * dmat[d]).astype(BF) for (d, r0, vp, qb, km, ke) in work]
                intras = [_dot(att, jnp.concatenate(
                              [jnp.concatenate([vp[:, 0:LANE], zero_v], axis=1),
                               jnp.concatenate([zero_v, vp[:, LANE:2 * LANE]], axis=1)], axis=0))
                          for att, (d, r0, vp, qb, km, ke) in zip(atts, work)]
            upds = [jnp.where(bmask, _dot_tn(ke, vp), 0.0) for (d, r0, vp, qb, km, ke) in work]
            state = {d: st_ref[d] for d in range(2)}
            for i, (d, r0, vp, qb, km, ke) in enumerate(work):
                st = state[d]
                if write:
                    oref = of_ref if d == 0 else ob_ref
                    oref[pl.ds(row0 + r0, CHUNK), :] = intras[i] + _dot(qb, st.astype(BF)) * qfac[d]
                state[d] = st * gend[d] + upds[i]
            for d, st in state.items():
                st_ref[d] = st
            return carry

        lax.fori_loop(0, nch // per_iter, body, 0)

    segment(vc_ref, 0, nc_ctx, need_ctx)
    segment(vl_ref, CT, nc_lat, True)

    def finish(o, w, g):
        mu = jnp.mean(o, axis=-1, keepdims=True)
        oc = o - mu
        var = jnp.mean(oc * oc, axis=-1, keepdims=True)
        return oc * lax.rsqrt(var + EPS) * w * _silu(g.astype(F32))

    for hh in range(2):
        sl = slice(hh * LANE, (hh + 1) * LANE)
        w = gnw_ref[:, sl]
        o = of_ref[CT:CT + S, sl] + ob_ref[CT:CT + S, sl]
        ol_ref[:, sl] = finish(o, w, gl_ref[:, sl]).astype(ol_ref.dtype)
        if need_ctx:
            o = of_ref[0:CT, sl] + ob_ref[0:CT, sl]
            oc_ref[:, sl] = finish(o, w, gc_ref[:, sl]).astype(oc_ref.dtype)


def _ret(p_l, p_c, decay_logit, gn_w, need_ctx):
    B, S, _ = p_l.shape
    CT = p_c.shape[1]
    T = CT + S
    cos, sdn, sup = _rope_tables(S)
    dl = jnp.broadcast_to(decay_logit.reshape(2 * N_HEADS, 1).astype(F32), (2 * N_HEADS, LANE))

    def spec(n, width, off):
        return pl.BlockSpec((None, n, width), lambda b, p: (b, 0, off(p)))

    in_specs = []
    for n in (CT, S):
        in_specs += [spec(n, LANE, lambda p: RET_Q + p), spec(n, LANE, lambda p: RET_K + p),
                     spec(n, 2 * LANE, lambda p: RET_V // 2 + p), spec(n, 2 * LANE, lambda p: RET_G // 2 + p)]
    tab = pl.BlockSpec((S, LANE), lambda b, p: (0, 0))
    in_specs += [tab, tab, tab,
                 pl.BlockSpec((2 * N_HEADS, LANE), lambda b, p: (0, 0)),
                 pl.BlockSpec((1, 2 * LANE), lambda b, p: (0, p))]
    out_specs = [pl.BlockSpec((None, S, 2 * LANE), lambda b, p: (b, 0, p))]
    out_shape = [jax.ShapeDtypeStruct((B, S, GROUP_W), BF)]
    if need_ctx:
        out_specs.append(pl.BlockSpec((None, CT, 2 * LANE), lambda b, p: (b, 0, p)))
        out_shape.append(jax.ShapeDtypeStruct((B, CT, GROUP_W), BF))
    res = pl.pallas_call(
        functools.partial(_ret_kernel, need_ctx=need_ctx, nc_ctx=CT // CHUNK, nc_lat=S // CHUNK),
        grid=(B, 2),
        in_specs=in_specs,
        out_specs=out_specs,
        out_shape=out_shape,
        scratch_shapes=[pltpu.VMEM((T, LANE), F32),
                        pltpu.VMEM((T, LANE), F32),
                        pltpu.VMEM((2, LANE, 2 * LANE), F32),
                        pltpu.VMEM((T, 2 * LANE), F32),
                        pltpu.VMEM((T, 2 * LANE), F32)],
        compiler_params=_cp(("parallel", "parallel")),
        name="retention",
    )(p_c, p_c, p_c, p_c, p_l, p_l, p_l, p_l, cos, sdn, sup, dl, gn_w.reshape(1, GROUP_W))
    return (res[1] if need_ctx else None), res[0]


def _gdn_kernel(qc_ref, kc_ref, vc_ref, zc_ref, sc_ref, ql_ref, kl_ref, vl_ref, zl_ref, sl_ref,
                cwq_ref, cwk_ref, cwv_ref, alog_ref, dtb_ref, nw_ref, *rest, need_ctx, nc_ctx, nc_lat):
    if need_ctx:
        ol_ref, oc_ref = rest[:2]
        rest = rest[2:]
    else:
        ol_ref = rest[0]
        oc_ref = None
        rest = rest[1:]
    qs, ks, vs, sm_ref, u_ref, wq_ref, a_ref, ke_ref, ge_ref, st_ref, of_ref, ob_ref = rest
    CT = nc_ctx * CHUNK
    S = nc_lat * CHUNK
    pp = pl.program_id(1)
    sm_ref[0:CT, :] = sc_ref[...]
    sm_ref[CT:CT + S, :] = sl_ref[...]
    ri = lax.broadcasted_iota(jnp.int32, (GDN_BLK, GDN_BLK), 0)
    ci = lax.broadcasted_iota(jnp.int32, (GDN_BLK, GDN_BLK), 1)
    same = (ri // CHUNK) == (ci // CHUNK)
    incl = (same & (ri >= ci), same & (ri <= ci))
    strict = (same & (ri > ci), same & (ri < ci))
    tri_bf = [jnp.where(m, 1.0, 0.0).astype(BF) for m in incl]
    eye_f = jnp.where(ri == ci, 1.0, 0.0)
    lane = lax.broadcasted_iota(jnp.int32, (GDN_BLK, LANE), 1)
    rowi = lax.broadcasted_iota(jnp.int32, (GDN_BLK, 1), 0)
    lvl_masks = []
    s = 1
    while s < CHUNK:
        lvl_masks.append(jnp.where(((ri // (2 * s)) == (ci // (2 * s))) & ((ri // s) != (ci // s)), 1.0, 0.0))
        s *= 2

    def conv_silu(x_ref, w_ref, n):
        x = x_ref[...].astype(F32)
        row = lax.broadcasted_iota(jnp.int32, x.shape, 0)
        xp = jnp.where(row == 0, 0.0, pltpu.roll(x, 1, 0))
        xn = jnp.where(row == n - 1, 0.0, pltpu.roll(x, n - 1, 0))
        return _silu(xp * w_ref[0:1, :] + x * w_ref[1:2, :] + xn * w_ref[2:3, :])

    def l2n(x):
        return x * lax.rsqrt(jnp.sum(x * x, axis=-1, keepdims=True) + EPS)

    for (q_ref, k_ref, v_ref, r0, n) in ((qc_ref, kc_ref, vc_ref, 0, CT), (ql_ref, kl_ref, vl_ref, CT, S)):
        q = conv_silu(q_ref, cwq_ref, n)
        k = conv_silu(k_ref, cwk_ref, n)
        v = conv_silu(v_ref, cwv_ref, n)
        for hh in range(2):
            sl = slice(hh * LANE, (hh + 1) * LANE)
            qs[r0:r0 + n, sl] = l2n(q[:, sl]) * (GDN_DK ** -0.5)
            ks[r0:r0 + n, sl] = l2n(k[:, sl])
        vs[r0:r0 + n, :] = v.astype(BF)

    neg_a = -jnp.exp(alog_ref[...])
    dtb = dtb_ref[...]

    def phase1(fwd_rows, bwd_rows):
        pipes = []
        a_list = []
        for d, rows in ((0, fwd_rows), (1, bwd_rows)):
            for g0 in rows:
                sm = sm_ref[pl.ds(g0, GDN_BLK), :]
                lg_all = neg_a * _softplus(sm + dtb)
                lb_all = _log_sigmoid(sm)
                cum_all = _exact_dot(tri_bf[d], lg_all)
                for hh in range(2):
                    h = 2 * pp + hh
                    sl = slice(hh * LANE, (hh + 1) * LANE)
                    k = ks[pl.ds(g0, GDN_BLK), sl]
                    q = qs[pl.ds(g0, GDN_BLK), sl]
                    v = vs[pl.ds(g0, GDN_BLK), sl]
                    kb = k.astype(BF)
                    kkqk = _dot_nt(jnp.concatenate([kb, q.astype(BF)], axis=0), kb)
                    kk = kkqk[0:GDN_BLK, :]
                    qk = kkqk[GDN_BLK:2 * GDN_BLK, :]
                    g = jnp.sum(jnp.where(lane == SM_A + d * N_HEADS + h, cum_all, 0.0), axis=-1, keepdims=True)
                    lb = jnp.sum(jnp.where(lane == SM_BT + d * N_HEADS + h, lb_all, 0.0), axis=-1, keepdims=True)
                    ends = (CHUNK - 1, GDN_BLK - 1) if d == 0 else (0, CHUNK)
                    tot_lo = g[ends[0]:ends[0] + 1, :]
                    tot_hi = g[ends[1]:ends[1] + 1, :]
                    tot = jnp.where(rowi < CHUNK, tot_lo, tot_hi)
                    hrow = jnp.broadcast_to(g - lb, (GDN_BLK, GDN_BLK)).T
                    e_in = jnp.exp(jnp.where(incl[d], g - hrow, -jnp.inf))
                    a_list.append(kk * jnp.where(strict[d], e_in, 0.0))
                    pipes.append((g0, hh, d, k, q, v, qk, e_in, g, lb, tot, tot_lo, tot_hi))
                yield
        a4 = jnp.stack(a_list, axis=0)
        x4 = eye_f[None] - a4 * lvl_masks[0][None]
        bdot = lambda p, r: jnp.einsum('pij,pjk->pik', p, r, preferred_element_type=F32)
        for msk in lvl_masks[1:]:
            t4 = (a4 * msk[None]).astype(BF)
            xb = x4.astype(BF)
            x4 = x4 - bdot(xb, bdot(t4, xb).astype(BF))
            yield
        x4b = x4.astype(BF)
        for p, (g0, hh, d, k, q, v, qk, e_in, g, lb, tot, tot_lo, tot_hi) in enumerate(pipes):
            gam = jnp.exp(g)
            rhs = jnp.concatenate([v, (k * gam).astype(BF)], axis=1)
            uw = _dot(x4b[p], rhs)
            u_ref[d, hh, pl.ds(g0, GDN_BLK), :] = uw[:, 0:LANE]
            w = uw[:, LANE:2 * LANE].astype(BF)
            qg = (q * gam).astype(BF)
            wq0 = _aligned(2 * g0, 2 * GDN_BLK)
            wq_ref[d, hh, pl.ds(wq0, 2 * GDN_BLK), :] = jnp.concatenate(
                [w[0:CHUNK], qg[0:CHUNK], w[CHUNK:GDN_BLK], qg[CHUNK:GDN_BLK]], axis=0)
            a_ref[d, hh, pl.ds(g0, GDN_BLK), :] = (qk * e_in).astype(BF)
            ke_ref[d, hh, pl.ds(g0, GDN_BLK), :] = (k * jnp.exp(tot - g + lb)).astype(BF)
            ge0 = _aligned((g0 // CHUNK) * 8, 16)
            ge_ref[d, hh, pl.ds(ge0, 16), :] = jnp.concatenate(
                [jnp.broadcast_to(jnp.exp(tot_lo), (8, LANE)), jnp.broadcast_to(jnp.exp(tot_hi), (8, LANE))], axis=0)
            if p % 4 == 3:
                yield

    def phase2(fwd_rows, bwd_rows):
        steps = []
        for gf, gb in zip(fwd_rows, bwd_rows):
            steps.append((gf, gb + CHUNK))
            steps.append((gf + CHUNK, gb))
        state = {(d, hh): st_ref[d, hh] for d in range(2) for hh in range(2)}
        outs = []
        for cf, cb in steps:
            chains = [(d, hh, _aligned(c, CHUNK)) for d, c in ((0, cf), (1, cb)) for hh in range(2)]
            wss = [_dot(wq_ref[d, hh, pl.ds(_aligned(2 * g0, 2 * CHUNK), 2 * CHUNK), :],
                        state[d, hh].astype(BF)) for (d, hh, g0) in chains]
            dbs = [(u_ref[d, hh, pl.ds(g0, CHUNK), :] - ws[0:CHUNK, :]).astype(BF)
                   for (d, hh, g0), ws in zip(chains, wss)]
            upd = [_dot_tn(ke_ref[d, hh, pl.ds(g0, CHUNK), :], db) for (d, hh, g0), db in zip(chains, dbs)]
            for (d, hh, g0), up in zip(chains, upd):
                ge = ge_ref[d, hh, pl.ds(_aligned((g0 // CHUNK) * 8, 8), 1), :]
                state[d, hh] = state[d, hh] * ge[:, 0:1] + up
            outs.append((chains, wss, dbs))
            yield
        for (d, hh), st in state.items():
            st_ref[d, hh] = st
        for chains, wss, dbs in outs:
            for (d, hh, g0), ws, db in zip(chains, wss, dbs):
                o = ws[CHUNK:2 * CHUNK, :] + _dot(a_ref[d, hh, pl.ds(g0, CHUNK), :],
                                                  jnp.concatenate([db, db], axis=0))
                oref = of_ref if d == 0 else ob_ref
                oref[pl.ds(g0, CHUNK), hh * LANE:(hh + 1) * LANE] = o
            yield

    def run(*gens):
        gens = list(gens)
        while gens:
            for gen in list(gens):
                try:
                    next(gen)
                except StopIteration:
                    gens.remove(gen)

    def seg_rows(base, nblk, first, count):
        fwd = [_aligned(base + (first + j) * GDN_BLK, GDN_BLK) for j in range(count)]
        bwd = [_aligned(base + (nblk - 1 - first - j) * GDN_BLK, GDN_BLK) for j in range(count)]
        return fwd, bwd

    nb_ctx = CT // GDN_BLK
    nb_lat = S // GDN_BLK
    grp = 4 if nb_lat % 4 == 0 else (2 if nb_lat % 2 == 0 else 1)
    n_grp = nb_lat // grp
    st_ref[...] = jnp.zeros_like(st_ref)

    ctx_rows = seg_rows(0, nb_ctx, 0, nb_ctx)
    run(phase1(*ctx_rows))
    run(phase1(*seg_rows(CT, nb_lat, 0, grp)), phase2(*ctx_rows))

    def stage(i, carry):
        run(phase1(*seg_rows(CT, nb_lat, i * grp, grp)), phase2(*seg_rows(CT, nb_lat, (i - 1) * grp, grp)))
        return carry

    lax.fori_loop(1, n_grp, stage, 0)
    run(phase2(*seg_rows(CT, nb_lat, (n_grp - 1) * grp, grp)))

    nw = nw_ref[...]
    for hh in range(2):
        sl = slice(hh * LANE, (hh + 1) * LANE)
        o = of_ref[CT:CT + S, sl] + ob_ref[CT:CT + S, sl]
        ol_ref[:, sl] = _finish_rms(o, nw, zl_ref[:, sl]).astype(ol_ref.dtype)
        if need_ctx:
            o = of_ref[0:CT, sl] + ob_ref[0:CT, sl]
            oc_ref[:, sl] = _finish_rms(o, nw, zc_ref[:, sl]).astype(oc_ref.dtype)


def _gdn(p_l, ps_l, p_c, ps_c, conv_w, a_log, dt_bias, o_norm, need_ctx):
    B, S, _ = p_l.shape
    CT = p_c.shape[1]
    T = CT + S
    nch = T // CHUNK
    alog = jnp.zeros((1, N_SMALL), F32).at[0, SM_A:SM_A + 2 * N_HEADS].set(a_log.reshape(-1).astype(F32))
    dtb = jnp.zeros((1, N_SMALL), F32).at[0, SM_A:SM_A + 2 * N_HEADS].set(dt_bias.reshape(-1).astype(F32))

    def spec(n, width, off):
        return pl.BlockSpec((None, n, width), lambda b, p: (b, 0, off(p)))

    in_specs = []
    for n in (CT, S):
        in_specs += [spec(n, 2 * LANE, lambda p: GDN_Q // 2 + p), spec(n, 2 * LANE, lambda p: GDN_K // 2 + p),
                     spec(n, 2 * LANE, lambda p: GDN_V // 2 + p), spec(n, 2 * LANE, lambda p: GDN_Z // 2 + p),
                     spec(n, N_SMALL, lambda p: 0)]
    cw = lambda part: pl.BlockSpec((3, 2 * LANE), lambda b, p: (0, 2 * part + p))
    vec = pl.BlockSpec((1, LANE), lambda b, p: (0, 0))
    in_specs += [cw(0), cw(1), cw(2), vec, vec, vec]
    out_specs = [pl.BlockSpec((None, S, 2 * LANE), lambda b, p: (b, 0, p))]
    out_shape = [jax.ShapeDtypeStruct((B, S, GROUP_W), BF)]
    if need_ctx:
        out_specs.append(pl.BlockSpec((None, CT, 2 * LANE), lambda b, p: (b, 0, p)))
        out_shape.append(jax.ShapeDtypeStruct((B, CT, GROUP_W), BF))
    res = pl.pallas_call(
        functools.partial(_gdn_kernel, need_ctx=need_ctx, nc_ctx=CT // CHUNK, nc_lat=S // CHUNK),
        grid=(B, 2),
        in_specs=in_specs,
        out_specs=out_specs,
        out_shape=out_shape,
        scratch_shapes=[pltpu.VMEM((T, 2 * LANE), F32),
                        pltpu.VMEM((T, 2 * LANE), F32),
                        pltpu.VMEM((T, 2 * LANE), BF),
                        pltpu.VMEM((T, N_SMALL), F32),
                        pltpu.VMEM((2, 2, T, LANE), F32),
                        pltpu.VMEM((2, 2, 2 * T, LANE), BF),
                        pltpu.VMEM((2, 2, T, GDN_BLK), BF),
                        pltpu.VMEM((2, 2, T, LANE), BF),
                        pltpu.VMEM((2, 2, nch * 8, LANE), F32),
                        pltpu.VMEM((2, 2, LANE, LANE), F32),
                        pltpu.VMEM((T, 2 * LANE), F32),
                        pltpu.VMEM((T, 2 * LANE), F32)],
        compiler_params=_cp(("parallel", "parallel")),
        name="gdn",
    )(p_c, p_c, p_c, p_c, ps_c, p_l, p_l, p_l, p_l, ps_l,
      conv_w, conv_w, conv_w, alog, dtb, o_norm.reshape(1, LANE))
    return (res[1] if need_ctx else None), res[0]


def _align_w_in(w):
    big = jnp.concatenate([w[..., 0:3072], w[..., 3104:5152], w[..., 5168:6704]], axis=-1)
    small = jnp.concatenate([w[..., 3072:3104], w[..., 5152:5168],
                             jnp.zeros(w.shape[:-1] + (N_SMALL - 48,), w.dtype)], axis=-1)
    return big.astype(BF), small.astype(BF)


def _pick(n, prefs):
    for p in prefs:
        if n % p == 0:
            return p
    return n


def kernel(x, c, ctx, c_ctx, ada_w, ada_b, norm1_w, norm2_w, w_in, w_out, na_q_norm, na_k_norm, na_rpb,
           gla_gate_up, gla_gate_b, gla_o_norm, gdn_conv_w, gdn_a_log, gdn_dt_bias, gdn_o_norm,
           ret_decay_logit, ret_gn_w, mlp_w1, mlp_w2):
    B, S, D = x.shape
    CT = ctx.shape[1]
    depth = ada_w.shape[0]
    R = ((B + 1 + 7) // 8) * 8
    cc = jnp.concatenate([c, c_ctx[None, :], jnp.zeros((R - B - 1, D), F32)], axis=0)
    mod_all = _ada(cc, ada_w, ada_b).reshape(depth, R, 6, D)
    na_tables = _natten_tables(na_rpb, S)

    tm_l = _pick(S, (1024, 512, 256))
    tm_c = _pick(B * CT, (1024, 512, 256))
    tn = _pick(N_BIG, (1664, 512, 256, 128))
    tm_o = _pick(S, (512, 256))
    tm_oc = _pick(B * CT, (512, 256))
    th = _pick(mlp_w1.shape[2], (1024, 512, 256))

    w_big, w_small = _align_w_in(w_in)
    wo = w_out.astype(BF)
    w1 = mlp_w1.astype(BF)
    w2 = mlp_w2.astype(BF)

    xl = x
    xc = ctx.reshape(1, B * CT, D)
    for layer in range(depth):
        need_ctx = layer < depth - 1
        mod = mod_all[layer]
        nw1 = norm1_w[layer].reshape(1, D)
        nw2 = norm2_w[layer].reshape(1, D)

        p_l, ps_l = _in_proj(xl, mod, None, nw1, w_big, w_small, layer, tm_l, tn)
        p_c, ps_c = _in_proj(xc, mod, B, nw1, w_big, w_small, layer, tm_c, tn)
        p_c = p_c.reshape(B, CT, N_BIG)
        ps_c = ps_c.reshape(B, CT, N_SMALL)

        na_c, na_l = _natten(p_l, p_c, na_q_norm[layer], na_k_norm[layer], na_tables, layer, need_ctx)
        gl_c, gl_l = _gla(p_l, ps_l, p_c, ps_c, gla_gate_up[layer], gla_gate_b[layer], gla_o_norm[layer], need_ctx)
        gd_c, gd_l = _gdn(p_l, ps_l, p_c, ps_c, gdn_conv_w[layer], gdn_a_log[layer], gdn_dt_bias[layer],
                          gdn_o_norm[layer], need_ctx)
        rt_c, rt_l = _ret(p_l, p_c, ret_decay_logit[layer], ret_gn_w[layer], need_ctx)

        xl, hl = _out_proj(xl, (na_l, gl_l, gd_l, rt_l), wo, layer, mod, None, nw2, tm_o)
        xl = _mlp(xl, hl, mod, None, w1, w2, layer, tm_o, th)
        if need_ctx:
            ys = tuple(t.reshape(1, B * CT, GROUP_W) for t in (na_c, gl_c, gd_c, rt_c))
            xc, hc = _out_proj(xc, ys, wo, layer, mod, B, nw2, tm_oc)
            xc = _mlp(xc, hc, mod, B, w1, w2, layer, tm_oc, th)
    return xl
```
